```python
import math
import jax
import jax.numpy as jnp
from jax import lax
import numpy as np

D_MODEL = 2048
BATCH = 4
SEQ = 2048
DEPTH = 4

GRID_W = 64
CTX_LEN = 256
HEAD_DIM = 128
N_HEADS_TOTAL = D_MODEL // HEAD_DIM
MLA_HEADS = N_HEADS_TOTAL // 2
MLA_NOPE_DIM = 128
MLA_ROPE_DIM = 64
MLA_QK_DIM = MLA_NOPE_DIM + MLA_ROPE_DIM
MLA_V_DIM = HEAD_DIM
MLA_KV_RANK = 512
NA_HEADS = N_HEADS_TOTAL - MLA_HEADS
NA_DIM = HEAD_DIM
NA_KH = 8
NA_KW = 16
EV_IN_WIDTH = MLA_HEADS * MLA_QK_DIM + MLA_KV_RANK + MLA_ROPE_DIM + 3 * NA_HEADS * NA_DIM
ATTN_BLOCK = 128
ROPE_THETA = 10000.0
HGRN_WIDTH = D_MODEL // 2
HGRN_HEADS = HGRN_WIDTH // HEAD_DIM
HGRN_DK = HEAD_DIM
HGRN_DV = HEAD_DIM
HGRN_CHUNK = 64
FORGET_FLOOR = 1e-30
HYENA_WIDTH = D_MODEL - HGRN_WIDTH
HYENA_SHORT = 3
HYENA_EMB = 33
HYENA_BANDS = (HYENA_EMB - 1) // 2
HYENA_FILT_HIDDEN = 64
HYENA_DECAY_TARGET = 1e-2
HYENA_FAST_PCT = 0.3
HYENA_SLOW_PCT = 1.5
OD_IN_WIDTH = 5 * HGRN_WIDTH + 3 * HYENA_WIDTH
MLP_HIDDEN = 4 * D_MODEL
N_EVEN = (DEPTH + 1) // 2
N_ODD = DEPTH // 2
NORM_EPS = 1e-6
NEG_INF = -1e30
F32 = jnp.float32

kernel_name = "hybrid_mla_natten_hgrn2_hyena_dit"


def rmsnorm(x, g):
    xf = x.astype(F32)
    y = xf * lax.rsqrt(jnp.mean(xf * xf, axis=-1, keepdims=True) + NORM_EPS)
    return (y * g.astype(F32)).astype(x.dtype)


def modulate(x, shift, scale):
    return x * (1.0 + scale) + shift


def rope_1d(x, pos):
    d = x.shape[-1]
    inv_freq = ROPE_THETA ** (-jnp.arange(0, d, 2, dtype=F32) / d)
    ang = pos.astype(F32)[:, None] * inv_freq[None, :]
    cos, sin = jnp.cos(ang)[:, None, :], jnp.sin(ang)[:, None, :]
    x1, x2 = x[..., : d // 2].astype(F32), x[..., d // 2:].astype(F32)
    return jnp.concatenate([x1 * cos - x2 * sin, x2 * cos + x1 * sin], -1).astype(x.dtype)


def rope_2d(x, rows, cols):
    half = x.shape[-1] // 2
    return jnp.concatenate([rope_1d(x[..., :half], rows), rope_1d(x[..., half:], cols)], -1)


def softmax_attend(q, k, v, scale):
    s = jnp.einsum("bhqd,bhkd->bhqk", q, k).astype(F32) * scale
    p = jax.nn.softmax(s, axis=-1).astype(v.dtype)
    return jnp.einsum("bhqk,bhke->bhqe", p, v)


def blocked_attend(q, k, v, scale):
    b, h, n_q, d = q.shape
    n_blk = n_q // ATTN_BLOCK
    qb = q.reshape(b, h, n_blk, ATTN_BLOCK, d).transpose(2, 0, 1, 3, 4)
    ob = lax.map(lambda qi: softmax_attend(qi, k, v, scale), qb)
    return ob.transpose(1, 2, 0, 3, 4).reshape(b, h, n_q, v.shape[-1])


def merge_heads(*outs):
    return jnp.concatenate([o.transpose(0, 2, 1, 3).reshape(o.shape[0], o.shape[2], -1) for o in outs], -1)


def even_heads(p, kv_norm_g, w_ukv, grid_pos):
    b, n, _ = p.shape
    sizes = [MLA_HEADS * MLA_QK_DIM, MLA_KV_RANK, MLA_ROPE_DIM, NA_HEADS * NA_DIM, NA_HEADS * NA_DIM]
    cuts = [int(s) for s in np.cumsum(sizes)]
    q_mla, c_kv, k_pe, q_na, k_na, v_na = jnp.split(p, cuts, axis=-1)
    q = q_mla.reshape(b, n, MLA_HEADS, MLA_QK_DIM)
    q_nope, q_pe = q[..., :MLA_NOPE_DIM], q[..., MLA_NOPE_DIM:]
    k_pe = k_pe[:, :, None, :]
    if grid_pos is not None:
        q_pe = rope_2d(q_pe, *grid_pos)
        k_pe = rope_2d(k_pe, *grid_pos)
    kv = (rmsnorm(c_kv, kv_norm_g) @ w_ukv).reshape(b, n, MLA_HEADS, MLA_NOPE_DIM + MLA_V_DIM)
    k_nope, v = kv[..., :MLA_NOPE_DIM], kv[..., MLA_NOPE_DIM:]
    q = jnp.concatenate([q_nope, q_pe], -1)
    k = jnp.concatenate([k_nope, jnp.broadcast_to(k_pe, (b, n, MLA_HEADS, MLA_ROPE_DIM))], -1)
    bhld = lambda t: t.transpose(0, 2, 1, 3)
    na = lambda t: t.reshape(b, n, NA_HEADS, NA_DIM).transpose(0, 2, 1, 3)
    return bhld(q), bhld(k), bhld(v), na(q_na), na(k_na), na(v_na)


def neighbourhood_attend(q, k, v, k_ctx, v_ctx, rel_bias):
    b, h, n, d = q.shape
    n_rows = n // GRID_W
    kh = min(NA_KH, n_rows)
    r = np.arange(n_rows)
    col = np.arange(GRID_W)
    row_idx = np.clip(r - kh // 2, 0, n_rows - kh)[:, None] + np.arange(kh)[None, :]
    col_start = np.clip(col - NA_KW // 2, 0, GRID_W - NA_KW)
    col_mask = (col[None, :] >= col_start[:, None]) & (col[None, :] < col_start[:, None] + NA_KW)
    row_off = row_idx - r[:, None] + (NA_KH - 1)
    col_off = np.clip(col[None, :] - col[:, None], 1 - NA_KW, NA_KW - 1) + (NA_KW - 1)
    qg = q.reshape(b, h, n_rows, GRID_W, d)
    k_band = k.reshape(b, h, n_rows, GRID_W, d)[:, :, row_idx]
    v_band = v.reshape(b, h, n_rows, GRID_W, d)[:, :, row_idx]
    scale = d ** -0.5
    bias = rel_bias[:, row_off[:, None, :, None], col_off[None, :, None, :]].astype(F32)
    s_nb = jnp.einsum("bhrqd,bhrkwd->bhrqkw", qg, k_band).astype(F32) * scale + bias
    s_nb = jnp.where(col_mask[:, None, :], s_nb, NEG_INF)
    s_ctx = jnp.einsum("bhrqd,bhcd->bhrqc", qg, k_ctx).astype(F32) * scale
    s = jnp.concatenate([s_nb.reshape(b, h, n_rows, GRID_W, kh * GRID_W), s_ctx], -1)
    p = jax.nn.softmax(s, axis=-1).astype(v.dtype)
    p_nb = p[..., : kh * GRID_W].reshape(b, h, n_rows, GRID_W, kh, GRID_W)
    o = (jnp.einsum("bhrqkw,bhrkwd->bhrqd", p_nb, v_band)
         + jnp.einsum("bhrqc,bhcd->bhrqd", p[..., kh * GRID_W:], v_ctx))
    return o.reshape(b, h, n, d)


def even_mixer(a_lat, a_ctx, w_in, kv_norm_g, w_ukv, rel_bias, ctx_out):
    n = a_lat.shape[1]
    pos = jnp.arange(n)
    grid = (pos // GRID_W, pos % GRID_W)
    q_l, k_l, v_l, qn_l, kn_l, vn_l = even_heads(a_lat @ w_in, kv_norm_g, w_ukv, grid)
    q_c, k_c, v_c, qn_c, kn_c, vn_c = even_heads(a_ctx @ w_in, kv_norm_g, w_ukv, None)
    mla_scale = MLA_QK_DIM ** -0.5
    o_mla = blocked_attend(q_l, jnp.concatenate([k_c, k_l], 2), jnp.concatenate([v_c, v_l], 2), mla_scale)
    o_na = neighbourhood_attend(qn_l, kn_l, vn_l, kn_c, vn_c, rel_bias)
    y_lat = merge_heads(o_mla, o_na)
    if not ctx_out:
        return y_lat, None
    y_ctx = merge_heads(softmax_attend(q_c, k_c, v_c, mla_scale),
                        softmax_attend(qn_c, kn_c, vn_c, NA_DIM ** -0.5))
    return y_lat, y_ctx


def gla_chunked(q, v, log_f, s0):
    b, h, n, dk = q.shape
    dv = v.shape[-1]
    n_chunks = n // HGRN_CHUNK
    k = -jnp.expm1(log_f)
    chunks = lambda t: t.reshape(b, h, n_chunks, HGRN_CHUNK, t.shape[-1]).transpose(2, 0, 1, 3, 4)
    causal = jnp.tril(jnp.ones((HGRN_CHUNK, HGRN_CHUNK), dtype=bool))[:, :, None]

    def step(state, inp):
        qc, kc, vc, fc = inp
        cum = jnp.cumsum(fc, axis=2)
        diff = jnp.where(causal, cum[:, :, :, None, :] - cum[:, :, None, :, :], 0.0)
        decay = jnp.where(causal, jnp.exp(diff), 0.0)
        scores = jnp.einsum("bhtd,bhsd,bhtsd->bhts", qc, kc, decay)
        o = (jnp.einsum("bhtd,bhde->bhte", qc * jnp.exp(cum), state)
             + jnp.einsum("bhts,bhse->bhte", scores, vc))
        last = cum[:, :, -1:, :]
        state = (jnp.exp(last[:, :, 0, :])[..., None] * state
                 + jnp.einsum("bhsd,bhse->bhde", kc * jnp.exp(last - cum), vc))
        return state, o

    s_fin, o = lax.scan(step, s0, (chunks(q), chunks(k), chunks(v), chunks(log_f)))
    return o.transpose(1, 2, 0, 3, 4).reshape(b, h, n, dv), s_fin


def log_forget(z, lb):
    f = lb + (1.0 - lb) * jax.nn.sigmoid(z.astype(F32))
    return jnp.log(jnp.maximum(f, FORGET_FLOOR))


def hgrn2_bidir(p_lat, p_ctx, lb_fwd, lb_bwd, norm_g, ctx_out):
    def heads(t):
        b_, n_, _ = t.shape
        return t.astype(F32).reshape(b_, n_, HGRN_HEADS, -1).transpose(0, 2, 1, 3)

    def prep(p):
        q, i, zf, zb, g = jnp.split(p, 5, axis=-1)
        return (heads(jax.nn.silu(q)), heads(i), heads(log_forget(zf, lb_fwd)),
                heads(log_forget(zb, lb_bwd)), g)

    def readout(o, g):
        b_, h_, n_, dv = o.shape
        o = rmsnorm(o, norm_g.reshape(HGRN_HEADS, 1, HGRN_DV)).transpose(0, 2, 1, 3).reshape(b_, n_, h_ * dv)
        return (o * jax.nn.silu(g.astype(F32))).astype(g.dtype)

    rev = lambda t: jnp.flip(t, axis=2)
    q_l, i_l, lff_l, lfb_l, g_l = prep(p_lat)
    q_c, i_c, lff_c, lfb_c, g_c = prep(p_ctx)
    s0 = jnp.zeros((p_ctx.shape[0], HGRN_HEADS, HGRN_DK, HGRN_DV), F32)
    o_cf, s_cf = gla_chunked(q_c, i_c, lff_c, s0)
    o_lf, _ = gla_chunked(q_l, i_l, lff_l, s_cf)
    o_cb, s_cb = gla_chunked(rev(q_c), rev(i_c), rev(lfb_c), s0)
    o_lb, _ = gla_chunked(rev(q_l), rev(i_l), rev(lfb_l), s_cb)
    y_lat = readout(o_lf + rev(o_lb), g_l)
    y_ctx = readout(o_cf + rev(o_cb), g_c) if ctx_out else None
    return y_lat, y_ctx


def short_conv(u, w, bias):
    n = u.shape[1]
    pad = HYENA_SHORT // 2
    up = jnp.pad(u, ((0, 0), (pad, HYENA_SHORT - 1 - pad), (0, 0)))
    out = bias
    for j in range(HYENA_SHORT):
        out = out + up[:, j:j + n] * w[j]
    return out


def hyena_filter(n, w1, b1, w2, b2, w3, b3, freq, w_out):
    pos = jnp.arange(n, dtype=F32)
    t = pos / max(n - 1, 1)
    bands = jnp.linspace(1e-4, HYENA_BANDS - 1, HYENA_BANDS, dtype=F32)
    ang = (2.0 * math.pi / n) * pos[:, None] * bands[None, :]
    z = jnp.concatenate([t[:, None], jnp.cos(ang), -jnp.sin(ang)], -1)
    fr = freq.astype(F32)
    hdn = jnp.sin(fr * (z @ w1.astype(F32) + b1.astype(F32)))
    hdn = jnp.sin(fr * (hdn @ w2.astype(F32) + b2.astype(F32)))
    hdn = jnp.sin(fr * (hdn @ w3.astype(F32) + b3.astype(F32)))
    filt = hdn @ w_out.astype(F32)
    max_decay = math.log(HYENA_DECAY_TARGET) / HYENA_FAST_PCT
    min_decay = math.log(HYENA_DECAY_TARGET) / HYENA_SLOW_PCT
    deltas = jnp.abs(jnp.linspace(min_decay, max_decay, HYENA_WIDTH, dtype=F32))
    window = jnp.exp(-t[:, None] * deltas[None, :])
    return filt[:, :HYENA_WIDTH] * window, filt[:, HYENA_WIDTH:] * window


def long_conv_bidir(v, h_fwd, h_bwd, skip):
    n, ch = h_fwd.shape
    k = jnp.concatenate([h_fwd, jnp.zeros((1, ch), F32), h_bwd[:0:-1]], axis=0)
    vf = v.astype(F32)
    y = jnp.fft.irfft(jnp.fft.rfft(vf, n=2 * n, axis=1) * jnp.fft.rfft(k, axis=0)[None],
                      n=2 * n, axis=1)[:, :n]
    return (y + vf * skip.astype(F32)).astype(v.dtype)


def hyena(u, conv_w, conv_b, filt, skip):
    n = u.shape[1]
    x0, x1, v = jnp.split(short_conv(u, conv_w, conv_b), 3, axis=-1)
    h_fwd, h_bwd = hyena_filter(n, *filt)
    return x0 * long_conv_bidir(v * x1, h_fwd, h_bwd, skip)


def odd_mixer(a_lat, a_ctx, w_in, lb_fwd, lb_bwd, norm_g, conv_w, conv_b, filt, skip, ctx_out):
    split_at = 5 * HGRN_WIDTH
    p_lat = a_lat @ w_in
    p_ctx = a_ctx @ (w_in if ctx_out else w_in[:, :split_at])
    hg_lat, hg_ctx = hgrn2_bidir(p_lat[..., :split_at], p_ctx[..., :split_at], lb_fwd, lb_bwd, norm_g, ctx_out)
    y_lat = jnp.concatenate([hg_lat, hyena(p_lat[..., split_at:], conv_w, conv_b, filt, skip)], -1)
    if not ctx_out:
        return y_lat, None
    y_ctx = jnp.concatenate([hg_ctx, hyena(p_ctx[..., split_at:], conv_w, conv_b, filt, skip)], -1)
    return y_lat, y_ctx


def sq_relu_mlp(h, w1, w2):
    return jnp.square(jax.nn.relu(h @ w1)) @ w2


def setup_inputs(seed: int = 0) -> dict:
    key = jax.random.key(seed)
    ks = jax.random.split(key, 32)
    D = D_MODEL

    def nrm(k, shape, scale):
        return jax.random.normal(k, shape, F32) * scale

    return {
        "x": nrm(ks[0], (BATCH, SEQ, D), 1.0),
        "c": nrm(ks[1], (BATCH, D), 1.0),
        "ctx": nrm(ks[2], (BATCH, CTX_LEN, D), 1.0),
        "c_ctx": nrm(ks[3], (D,), 1.0),
        "ada_w": nrm(ks[4], (DEPTH, D, 6 * D), 0.5 * D ** -0.5),
        "ada_b": nrm(ks[5], (DEPTH, 6 * D), 0.02),
        "norm_mix_g": 1.0 + nrm(ks[6], (DEPTH, D), 0.02),
        "norm_mlp_g": 1.0 + nrm(ks[7], (DEPTH, D), 0.02),
        "w_out": nrm(ks[8], (DEPTH, D, D), D ** -0.5),
        "mlp_w1": nrm(ks[9], (DEPTH, D, MLP_HIDDEN), D ** -0.5),
        "mlp_w2": nrm(ks[10], (DEPTH, MLP_HIDDEN, D), MLP_HIDDEN ** -0.5),
        "final_norm_g": 1.0 + nrm(ks[11], (D,), 0.02),
        "ev_w_in": nrm(ks[12], (N_EVEN, D, EV_IN_WIDTH), D ** -0.5),
        "mla_kv_norm_g": 1.0 + nrm(ks[13], (N_EVEN, MLA_KV_RANK), 0.02),
        "mla_w_ukv": nrm(ks[14], (N_EVEN, MLA_KV_RANK, MLA_HEADS * (MLA_NOPE_DIM + MLA_V_DIM)), MLA_KV_RANK ** -0.5),
        "na_rel_bias": nrm(ks[15], (N_EVEN, NA_HEADS, 2 * NA_KH - 1, 2 * NA_KW - 1), 0.1),
        "od_w_in": nrm(ks[16], (N_ODD, D, OD_IN_WIDTH), D ** -0.5),
        "hgrn_lb_logits": nrm(ks[17], (2, N_ODD, HGRN_WIDTH), 0.5),
        "hgrn_norm_g": 1.0 + nrm(ks[18], (N_ODD, HGRN_WIDTH), 0.02),
        "hy_conv_w": nrm(ks[19], (N_ODD, HYENA_SHORT, 3 * HYENA_WIDTH), HYENA_SHORT ** -0.5),
        "hy_conv_b": nrm(ks[20], (N_ODD, 3 * HYENA_WIDTH), 0.02),
        "hy_filt_w1": nrm(ks[21], (N_ODD, HYENA_EMB, HYENA_FILT_HIDDEN), HYENA_EMB ** -0.5),
        "hy_filt_b1": nrm(ks[22], (N_ODD, HYENA_FILT_HIDDEN), 0.1),
        "hy_filt_w2": nrm(ks[23], (N_ODD, HYENA_FILT_HIDDEN, HYENA_FILT_HIDDEN), HYENA_FILT_HIDDEN ** -0.5),
        "hy_filt_b2": nrm(ks[24], (N_ODD, HYENA_FILT_HIDDEN), 0.1),
        "hy_filt_w3": nrm(ks[25], (N_ODD, HYENA_FILT_HIDDEN, HYENA_FILT_HIDDEN), HYENA_FILT_HIDDEN ** -0.5),
        "hy_filt_b3": nrm(ks[26], (N_ODD, HYENA_FILT_HIDDEN), 0.1),
        "hy_filt_freq": 1.0 + nrm(ks[27], (N_ODD, HYENA_FILT_HIDDEN), 0.1),
        "hy_filt_wout": nrm(ks[28], (N_ODD, HYENA_FILT_HIDDEN, 2 * HYENA_WIDTH), 0.1 * HYENA_FILT_HIDDEN ** -0.5),
        "hy_skip": nrm(ks[29], (N_ODD, HYENA_WIDTH), 0.5),
    }


def reference(x, c, ctx, c_ctx, ada_w, ada_b, norm_mix_g, norm_mlp_g, w_out, mlp_w1, mlp_w2,
              final_norm_g, ev_w_in, mla_kv_norm_g, mla_w_ukv, na_rel_bias, od_w_in,
              hgrn_lb_logits, hgrn_norm_g, hy_conv_w, hy_conv_b, hy_filt_w1, hy_filt_b1,
              hy_filt_w2, hy_filt_b2, hy_filt_w3, hy_filt_b3, hy_filt_freq, hy_filt_wout, hy_skip):
    lb_p = jax.nn.softmax(hgrn_lb_logits.astype(F32), axis=1)
    lower_bounds = jnp.cumsum(lb_p, axis=1) - lb_p[:, :1]
    h_lat, h_ctx = x, ctx
    for l in range(DEPTH):
        ctx_out = l < DEPTH - 1
        mod_lat = jax.nn.silu(c) @ ada_w[l] + ada_b[l]
        sh_a, sc_a, g_a, sh_m, sc_m, g_m = [t[:, None, :] for t in jnp.split(mod_lat, 6, axis=-1)]
        mod_ctx = jax.nn.silu(c_ctx) @ ada_w[l] + ada_b[l]
        csh_a, csc_a, cg_a, csh_m, csc_m, cg_m = jnp.split(mod_ctx, 6, axis=-1)
        a_lat = modulate(rmsnorm(h_lat, norm_mix_g[l]), sh_a, sc_a)
        a_ctx = modulate(rmsnorm(h_ctx, norm_mix_g[l]), csh_a, csc_a)
        if l % 2 == 0:
            e = l // 2
            y_lat, y_ctx = even_mixer(a_lat, a_ctx, ev_w_in[e], mla_kv_norm_g[e], mla_w_ukv[e],
                                      na_rel_bias[e], ctx_out)
        else:
            o = l // 2
            filt = (hy_filt_w1[o], hy_filt_b1[o], hy_filt_w2[o], hy_filt_b2[o], hy_filt_w3[o],
                    hy_filt_b3[o], hy_filt_freq[o], hy_filt_wout[o])
            y_lat, y_ctx = odd_mixer(a_lat, a_ctx, od_w_in[o], lower_bounds[0, o], lower_bounds[1, o],
                                     hgrn_norm_g[o], hy_conv_w[o], hy_conv_b[o], filt, hy_skip[o], ctx_out)
        h_lat = h_lat + g_a * (y_lat @ w_out[l])
        h_lat = h_lat + g_m * sq_relu_mlp(modulate(rmsnorm(h_lat, norm_mlp_g[l]), sh_m, sc_m),
                                          mlp_w1[l], mlp_w2[l])
        if ctx_out:
            h_ctx = h_ctx + cg_a * (y_ctx @ w_out[l])
            h_ctx = h_ctx + cg_m * sq_relu_mlp(modulate(rmsnorm(h_ctx, norm_mlp_g[l]), csh_m, csc_m),
                                               mlp_w1[l], mlp_w2[l])
    return rmsnorm(h_lat, final_norm_g)
```

```python
import functools
import math

import numpy as np
import jax
import jax.numpy as jnp
from jax import lax
from jax.experimental import pallas as pl
from jax.experimental.pallas import tpu as pltpu

F32 = jnp.float32
BF16 = jnp.bfloat16

D_MODEL = 2048
DEPTH = 4
GRID_W = 64
HEAD_DIM = 128
MLA_HEADS = 8
MLA_NOPE_DIM = 128
MLA_ROPE_DIM = 64
MLA_QK_DIM = MLA_NOPE_DIM + MLA_ROPE_DIM
MLA_KV_RANK = 512
NA_HEADS = 8
NA_DIM = 128
NA_KH = 8
NA_KW = 16
ROPE_THETA = 10000.0
HGRN_WIDTH = 1024
HGRN_HEADS = 8
FORGET_FLOOR = 1e-30
HYENA_WIDTH = 1024
HYENA_SHORT = 3
HYENA_EMB = 33
HYENA_BANDS = (HYENA_EMB - 1) // 2
HYENA_FILT_HIDDEN = 64
HYENA_DECAY_TARGET = 1e-2
HYENA_FAST_PCT = 0.3
HYENA_SLOW_PCT = 1.5
MLP_HIDDEN = 4 * D_MODEL
NORM_EPS = 1e-6
NEG_INF = -1e30

LANES = 128
VMEM_LIMIT_BYTES = 56 * 1024 * 1024

EV_QMLA_BLK = 0
EV_QNA_BLK = 16
EV_KNA_BLK = 24
EV_VNA_BLK = 32
EV_CKV_BLK = 40
EV_KPE_BLK = 44
EV_WIDTH = 48 * LANES
HGRN_CHUNK = 16


def _cparams(sem):
    return pltpu.CompilerParams(dimension_semantics=sem, vmem_limit_bytes=VMEM_LIMIT_BYTES)


def _dot(a, b):
    return jnp.dot(a, b, preferred_element_type=F32)


def _dot_t(a, b):
    return lax.dot_general(a, b, (((1,), (1,)), ((), ())), preferred_element_type=F32)


def _dot_tn(a, b):
    return lax.dot_general(a, b, (((0,), (0,)), ((), ())), preferred_element_type=F32)


def _dot_f32(a, b):
    return jnp.dot(a, b, preferred_element_type=F32, precision=lax.Precision.HIGHEST)


def _sigmoid(x):
    return 1.0 / (1.0 + jnp.exp(-x))


def _silu(x):
    return x * _sigmoid(x)


def _rms(x):
    return x * lax.rsqrt(jnp.mean(x * x, axis=-1, keepdims=True) + NORM_EPS)


def _ada_kernel(s_ref, w_ref, b_ref, o_ref):
    s = _silu(s_ref[...]).astype(BF16)
    o_ref[...] = _dot(s, w_ref[...].astype(BF16)) + b_ref[...]


def _ada_modulation(cond, ada_w, ada_b, tn=1024):
    depth, d, n = ada_w.shape
    rows = cond.shape[0]
    return pl.pallas_call(
        _ada_kernel,
        grid=(depth, n // tn),
        in_specs=[
            pl.BlockSpec((rows, d), lambda l, j: (0, 0)),
            pl.BlockSpec((None, d, tn), lambda l, j: (l, 0, j)),
            pl.BlockSpec((None, 1, tn), lambda l, j: (l, 0, j)),
        ],
        out_specs=pl.BlockSpec((None, rows, tn), lambda l, j: (l, 0, j)),
        out_shape=jax.ShapeDtypeStruct((depth, rows, n), F32),
        compiler_params=_cparams(("arbitrary", "arbitrary")),
    )(cond, ada_w, ada_b.reshape(depth, 1, n))


def _mod_spec(chunk, tiles_per_group, group0, d):
    return pl.BlockSpec((None, 1, d), lambda i, *_: (group0 + i // tiles_per_group, 0, chunk))


def _proj_kernel(*refs, modulated):
    if modulated:
        x_ref, g_ref, sh_ref, sc_ref, w_ref, o_ref, a_ref = refs
    else:
        x_ref, g_ref, w_ref, o_ref, a_ref = refs

    @pl.when(pl.program_id(1) == 0)
    def _():
        y = _rms(x_ref[...].astype(F32)) * g_ref[...]
        if modulated:
            y = y * (1.0 + sc_ref[...]) + sh_ref[...]
        a_ref[...] = y.astype(BF16)

    o_ref[...] = _dot(a_ref[...], w_ref[...]).astype(o_ref.dtype)


def _norm_proj(x, x_col_blk, k, g, w, out_dtype, mod=None, tm=512, tn=1024):
    m = x.shape[0]
    n = w.shape[1]
    tm = min(tm, m)
    tn = min(tn, n)
    assert m % tm == 0 and n % tn == 0
    in_specs = [pl.BlockSpec((tm, k), lambda i, j: (i, x_col_blk)),
                pl.BlockSpec((1, k), lambda i, j: (0, 0))]
    args = [x, g.reshape(1, k)]
    if mod is not None:
        mod3, sh_chunk, sc_chunk, tiles_per_group, group0 = mod
        in_specs += [_mod_spec(sh_chunk, tiles_per_group // tm, group0, k),
                     _mod_spec(sc_chunk, tiles_per_group // tm, group0, k)]
        args += [mod3, mod3]
    in_specs.append(pl.BlockSpec((k, tn), lambda i, j: (0, j)))
    args.append(w)
    return pl.pallas_call(
        functools.partial(_proj_kernel, modulated=mod is not None),
        grid=(m // tm, n // tn),
        in_specs=in_specs,
        out_specs=pl.BlockSpec((tm, tn), lambda i, j: (i, j)),
        out_shape=jax.ShapeDtypeStruct((m, n), out_dtype),
        scratch_shapes=[pltpu.VMEM((tm, k), BF16)],
        compiler_params=_cparams(("arbitrary", "arbitrary")),
    )(*args)


def _outproj_kernel(y1_ref, y2_ref, w_ref, h_ref, gate_ref, o_ref):
    k1 = y1_ref.shape[1]
    acc = _dot(y1_ref[...], w_ref[:k1, :]) + _dot(y2_ref[...], w_ref[k1:, :])
    o_ref[...] = h_ref[...] + gate_ref[...] * acc


def _out_proj(y1, y2, w, h, mod, tm=512):
    m, d = h.shape
    k1, k2 = y1.shape[1], y2.shape[1]
    tm = min(tm, m)
    mod3, gate_chunk, rows_per_group, group0 = mod
    return pl.pallas_call(
        _outproj_kernel,
        grid=(m // tm,),
        in_specs=[
            pl.BlockSpec((tm, k1), lambda i: (i, 0)),
            pl.BlockSpec((tm, k2), lambda i: (i, 0)),
            pl.BlockSpec((k1 + k2, d), lambda i: (0, 0)),
            pl.BlockSpec((tm, d), lambda i: (i, 0)),
            _mod_spec(gate_chunk, rows_per_group // tm, group0, d),
        ],
        out_specs=pl.BlockSpec((tm, d), lambda i: (i, 0)),
        out_shape=jax.ShapeDtypeStruct((m, d), F32),
        compiler_params=_cparams(("arbitrary",)),
    )(y1, y2, w, h, mod3)


def _mlp_kernel(*refs, final_norm):
    if final_norm:
        h_ref, g_ref, sh_ref, sc_ref, gate_ref, w1_ref, w2_ref, fg_ref, o_ref, a_ref, acc_ref = refs
    else:
        h_ref, g_ref, sh_ref, sc_ref, gate_ref, w1_ref, w2_ref, o_ref, a_ref, acc_ref = refs
    k = pl.program_id(1)

    @pl.when(k == 0)
    def _():
        y = _rms(h_ref[...]) * g_ref[...]
        a_ref[...] = (y * (1.0 + sc_ref[...]) + sh_ref[...]).astype(BF16)
        acc_ref[...] = jnp.zeros_like(acc_ref)

    u = jnp.maximum(_dot(a_ref[...], w1_ref[...]), 0.0)
    acc_ref[...] += _dot((u * u).astype(BF16), w2_ref[...])

    @pl.when(k == pl.num_programs(1) - 1)
    def _():
        out = h_ref[...] + gate_ref[...] * acc_ref[...]
        if final_norm:
            out = _rms(out) * fg_ref[...]
        o_ref[...] = out


def _mlp(h, g, w1, w2, mod, final_g=None, tm=512, th=1024):
    m, d = h.shape
    hid = w1.shape[1]
    tm = min(tm, m)
    mod3, sh_chunk, sc_chunk, gate_chunk, rows_per_group, group0 = mod
    tpg = rows_per_group // tm
    in_specs = [
        pl.BlockSpec((tm, d), lambda i, k: (i, 0)),
        pl.BlockSpec((1, d), lambda i, k: (0, 0)),
        _mod_spec(sh_chunk, tpg, group0, d),
        _mod_spec(sc_chunk, tpg, group0, d),
        _mod_spec(gate_chunk, tpg, group0, d),
        pl.BlockSpec((d, th), lambda i, k: (0, k)),
        pl.BlockSpec((th, d), lambda i, k: (k, 0)),
    ]
    args = [h, g.reshape(1, d), mod3, mod3, mod3, w1, w2]
    if final_g is not None:
        in_specs.append(pl.BlockSpec((1, d), lambda i, k: (0, 0)))
        args.append(final_g.reshape(1, d))
    return pl.pallas_call(
        functools.partial(_mlp_kernel, final_norm=final_g is not None),
        grid=(m // tm, hid // th),
        in_specs=in_specs,
        out_specs=pl.BlockSpec((tm, d), lambda i, k: (i, 0)),
        out_shape=jax.ShapeDtypeStruct((m, d), F32),
        scratch_shapes=[pltpu.VMEM((tm, d), BF16), pltpu.VMEM((tm, d), F32)],
        compiler_params=_cparams(("arbitrary", "arbitrary")),
    )(*args)


def _softmax_pv(scores, values):
    m = functools.reduce(jnp.maximum, [jnp.max(s, axis=-1, keepdims=True) for s in scores])
    ps = [jnp.exp(s - m) for s in scores]
    denom = functools.reduce(jnp.add, [jnp.sum(p, axis=-1, keepdims=True) for p in ps])
    o = functools.reduce(jnp.add, [_dot(p.astype(BF16), v) for p, v in zip(ps, values)])
    return o / denom


def _rope_kernel(x_ref, cos_ref, sin_ref, o_ref):
    x = x_ref[...].astype(F32)
    lane = lax.broadcasted_iota(jnp.int32, x.shape, 1)
    partner = jnp.where((lane % 32) < 16, pltpu.roll(x, LANES - 16, 1), pltpu.roll(x, 16, 1))
    o_ref[...] = (x * cos_ref[...] + partner * sin_ref[...]).astype(o_ref.dtype)


def _rope(p, first_blk, n_blk, cos_tab, sin_tab, tm=1024):
    m = p.shape[0]
    seq = cos_tab.shape[0]
    tm = min(tm, seq)
    per_seq = seq // tm
    return pl.pallas_call(
        _rope_kernel,
        grid=(m // tm, n_blk),
        in_specs=[
            pl.BlockSpec((tm, LANES), lambda i, j: (i, first_blk + 2 * j)),
            pl.BlockSpec((tm, LANES), lambda i, j: (i % per_seq, 0)),
            pl.BlockSpec((tm, LANES), lambda i, j: (i % per_seq, 0)),
        ],
        out_specs=pl.BlockSpec((tm, LANES), lambda i, j: (i, j)),
        out_shape=jax.ShapeDtypeStruct((m, n_blk * LANES), BF16),
        compiler_params=_cparams(("arbitrary", "arbitrary")),
    )(p, cos_tab, sin_tab)


def _mla_lat_kernel(qn_ref, qpe_ref, knc_ref, kpec_ref, vc_ref, knl_ref, kpel_ref, vl_ref,
                    o_ref, k_scr, v_scr, *, scale):
    lc = knc_ref.shape[0]

    @pl.when(pl.program_id(2) == 0)
    def _():
        k_scr[:lc, :LANES] = knc_ref[...]
        k_scr[:lc, LANES:] = kpec_ref[...]
        k_scr[lc:, :LANES] = knl_ref[...]
        k_scr[lc:, LANES:] = kpel_ref[...]
        v_scr[:lc, :] = vc_ref[...]
        v_scr[lc:, :] = vl_ref[...]

    q = jnp.concatenate([qn_ref[...], qpe_ref[...]], axis=1)
    s = _dot_t(q, k_scr[...]) * scale
    o_ref[...] = _softmax_pv([s], [v_scr[...]]).astype(o_ref.dtype)


def _mla_latent(p_lat, p_ctx, kv_lat, kv_ctx, qpe_rot, kpe_rot, batch, tq=512):
    n = p_lat.shape[0] // batch
    lc = p_ctx.shape[0] // batch
    nq = n // tq
    h = MLA_HEADS
    blk = lambda rows, f: pl.BlockSpec((rows, LANES), f)
    return pl.pallas_call(
        functools.partial(_mla_lat_kernel, scale=MLA_QK_DIM ** -0.5),
        grid=(batch, h, nq),
        in_specs=[
            blk(tq, lambda b, hh, i: (b * nq + i, EV_QMLA_BLK + 2 * hh)),
            blk(tq, lambda b, hh, i: (b * nq + i, hh)),
            blk(lc, lambda b, hh, i: (b, 2 * hh)),
            blk(lc, lambda b, hh, i: (b, EV_KPE_BLK)),
            blk(lc, lambda b, hh, i: (b, 2 * hh + 1)),
            blk(n, lambda b, hh, i: (b, 2 * hh)),
            blk(n, lambda b, hh, i: (b, 0)),
            blk(n, lambda b, hh, i: (b, 2 * hh + 1)),
        ],
        out_specs=blk(tq, lambda b, hh, i: (b * nq + i, hh)),
        out_shape=jax.ShapeDtypeStruct((batch * n, h * LANES), BF16),
        scratch_shapes=[pltpu.VMEM((lc + n, 2 * LANES), BF16), pltpu.VMEM((lc + n, LANES), BF16)],
        compiler_params=_cparams(("arbitrary", "arbitrary", "arbitrary")),
    )(p_lat, qpe_rot, kv_ctx, p_ctx, kv_ctx, kv_lat, kpe_rot, kv_lat)


def _ctx_attn_kernel(qm_ref, kn_ref, kpe_ref, vm_ref, qn_ref, kna_ref, vna_ref, om_ref, on_ref,
                     *, mla_scale, na_scale):
    k = jnp.concatenate([kn_ref[...], kpe_ref[...]], axis=1)
    s = _dot_t(qm_ref[...], k) * mla_scale
    om_ref[...] = _softmax_pv([s], [vm_ref[...]]).astype(om_ref.dtype)
    s = _dot_t(qn_ref[...], kna_ref[...]) * na_scale
    on_ref[...] = _softmax_pv([s], [vna_ref[...]]).astype(on_ref.dtype)


def _ctx_attention(p_ctx, kv_ctx, batch):
    lc = p_ctx.shape[0] // batch
    h = MLA_HEADS
    blk = lambda f: pl.BlockSpec((lc, LANES), f)
    out = jax.ShapeDtypeStruct((batch * lc, h * LANES), BF16)
    return pl.pallas_call(
        functools.partial(_ctx_attn_kernel, mla_scale=MLA_QK_DIM ** -0.5, na_scale=NA_DIM ** -0.5),
        grid=(batch, h),
        in_specs=[
            pl.BlockSpec((lc, 2 * LANES), lambda b, hh: (b, hh)),
            blk(lambda b, hh: (b, 2 * hh)),
            blk(lambda b, hh: (b, EV_KPE_BLK)),
            blk(lambda b, hh: (b, 2 * hh + 1)),
            blk(lambda b, hh: (b, EV_QNA_BLK + hh)),
            blk(lambda b, hh: (b, EV_KNA_BLK + hh)),
            blk(lambda b, hh: (b, EV_VNA_BLK + hh)),
        ],
        out_specs=[blk(lambda b, hh: (b, hh)), blk(lambda b, hh: (b, hh))],
        out_shape=[out, out],
        compiler_params=_cparams(("arbitrary", "arbitrary")),
    )(p_ctx, kv_ctx, p_ctx, kv_ctx, p_ctx, p_ctx, p_ctx)


def _na_kernel(q_ref, k_ref, v_ref, kc_ref, vc_ref, bias_ref, o_ref, *, scale, n_rows):
    kc = kc_ref[...]
    vc = vc_ref[...]
    win = NA_KH * GRID_W
    for r in range(n_rows):
        ws = min(max(r - NA_KH // 2, 0), n_rows - NA_KH)
        qr = q_ref[r * GRID_W:(r + 1) * GRID_W, :]
        s = _dot_t(qr, k_ref[ws * GRID_W:ws * GRID_W + win, :]) * scale + bias_ref[r - ws]
        sc = _dot_t(qr, kc) * scale
        o = _softmax_pv([s, sc], [v_ref[ws * GRID_W:ws * GRID_W + win, :], vc])
        o_ref[r * GRID_W:(r + 1) * GRID_W, :] = o.astype(o_ref.dtype)


def _na_bias_table(rel_bias):
    col = np.arange(GRID_W)
    col_start = np.clip(col - NA_KW // 2, 0, GRID_W - NA_KW)
    col_mask = (col[None, :] >= col_start[:, None]) & (col[None, :] < col_start[:, None] + NA_KW)
    col_off = np.clip(col[None, :] - col[:, None], 1 - NA_KW, NA_KW - 1) + (NA_KW - 1)
    row_off = np.arange(NA_KH)[None, :] - np.arange(NA_KH)[:, None] + (NA_KH - 1)
    t = rel_bias[:, row_off[:, None, :, None], col_off[None, :, None, :]].astype(F32)
    t = jnp.where(col_mask[None, None, :, None, :], t, NEG_INF)
    return t.reshape(rel_bias.shape[0], NA_KH, GRID_W, NA_KH * GRID_W)


def _na_latent(p_lat, p_ctx, bias_tab, batch):
    n = p_lat.shape[0] // batch
    lc = p_ctx.shape[0] // batch
    h = NA_HEADS
    n_rows = n // GRID_W
    assert n_rows >= NA_KH
    blk = lambda rows, f: pl.BlockSpec((rows, LANES), f)
    return pl.pallas_call(
        functools.partial(_na_kernel, scale=NA_DIM ** -0.5, n_rows=n_rows),
        grid=(batch, h),
        in_specs=[
            blk(n, lambda b, hh: (b, EV_QNA_BLK + hh)),
            blk(n, lambda b, hh: (b, EV_KNA_BLK + hh)),
            blk(n, lambda b, hh: (b, EV_VNA_BLK + hh)),
            blk(lc, lambda b, hh: (b, EV_KNA_BLK + hh)),
            blk(lc, lambda b, hh: (b, EV_VNA_BLK + hh)),
            pl.BlockSpec((None, NA_KH, GRID_W, NA_KH * GRID_W), lambda b, hh: (hh, 0, 0, 0)),
        ],
        out_specs=blk(n, lambda b, hh: (b, hh)),
        out_shape=jax.ShapeDtypeStruct((batch * n, h * LANES), BF16),
        compiler_params=_cparams(("arbitrary", "arbitrary")),
    )(p_lat, p_lat, p_lat, p_ctx, p_ctx, bias_tab)


def _rope_tables(n):
    pos = np.arange(n)
    rows, cols = pos // GRID_W, pos % GRID_W
    half = MLA_ROPE_DIM // 2
    inv_freq = ROPE_THETA ** (-np.arange(0, half, 2, dtype=np.float64) / half)
    cos = np.zeros((n, LANES), np.float64)
    sin = np.zeros((n, LANES), np.float64)
    for base, p in ((0, rows), (half, cols)):
        ang = p[:, None].astype(np.float64) * inv_freq[None, :]
        q = half // 2
        cos[:, base:base + q] = np.cos(ang)
        cos[:, base + q:base + half] = np.cos(ang)
        sin[:, base:base + q] = -np.sin(ang)
        sin[:, base + q:base + half] = np.sin(ang)
    return jnp.asarray(cos, F32), jnp.asarray(sin, F32)


def _even_w_in(w):
    d = w.shape[0]
    z64 = jnp.zeros((d, LANES - MLA_ROPE_DIM), w.dtype)
    pieces = []
    for h in range(MLA_HEADS):
        pieces += [w[:, h * MLA_QK_DIM:h * MLA_QK_DIM + MLA_NOPE_DIM],
                   w[:, h * MLA_QK_DIM + MLA_NOPE_DIM:(h + 1) * MLA_QK_DIM], z64]
    q_end = MLA_HEADS * MLA_QK_DIM
    ckv_end = q_end + MLA_KV_RANK
    kpe_end = ckv_end + MLA_ROPE_DIM
    pieces += [w[:, kpe_end:], w[:, q_end:ckv_end], w[:, ckv_end:kpe_end], z64]
    out = jnp.concatenate(pieces, axis=1)
    pad = EV_WIDTH - out.shape[1]
    return jnp.concatenate([out, jnp.zeros((d, pad), w.dtype)], axis=1).astype(BF16)


def _hgrn_step(q_raw, v, z, lb, st, tri, ones_bf, reverse):
    c = HGRN_CHUNK
    q = _silu(q_raw)
    f = jnp.maximum(lb + (1.0 - lb) * _sigmoid(z), FORGET_FLOOR)
    lf = jnp.log(f)
    k = 1.0 - f
    cum = _dot_f32(tri, lf)
    tot = cum[0:1] if reverse else cum[c - 1:c]
    o = _dot_t((q * jnp.exp(cum)).astype(BF16), st.astype(BF16))
    kd = (k * jnp.exp(tot - cum)).astype(BF16)
    st_new = st * jnp.exp(tot) + _dot_tn(v.astype(BF16), kd)
    rows = lax.broadcasted_iota(jnp.int32, (c, LANES), 0)
    pieces = []
    for s in range(c):
        mask = (rows <= s) if reverse else (rows >= s)
        e = jnp.exp(jnp.where(mask, cum - cum[s:s + 1], NEG_INF))
        pieces.append((q * e * k[s:s + 1]).astype(BF16))
    r = _dot(jnp.concatenate(pieces, axis=0), ones_bf)
    for s in range(c):
        o = o + r[s * c:(s + 1) * c] * v[s:s + 1]
    return o, st_new


def _hgrn_kernel(ql_ref, il_ref, zfl_ref, zbl_ref, gl_ref, qc_ref, ic_ref, zfc_ref, zbc_ref, gc_ref,
                 lbl_ref, ng_ref, yl_ref, yc_ref, ol_scr, oc_scr, *, layer):
    c = HGRN_CHUNK
    lbs = []
    for d in range(2):
        lg = lbl_ref[d]
        ex = jnp.exp(lg - jnp.max(lg, axis=0, keepdims=True))
        p = ex / jnp.sum(ex, axis=0, keepdims=True)
        lbs.append(jnp.sum(p[:layer + 1], axis=0, keepdims=True) - p[0:1])
    ri = lax.broadcasted_iota(jnp.int32, (c, c), 0)
    ci = lax.broadcasted_iota(jnp.int32, (c, c), 1)
    ones_bf = jnp.ones((LANES, LANES), BF16)

    for reverse in (False, True):
        tri = jnp.where((ci >= ri) if reverse else (ci <= ri), 1.0, 0.0).astype(F32)
        lb = lbs[1 if reverse else 0]
        st = jnp.zeros((LANES, LANES), F32)
        for q_ref, i_ref, z_ref, o_scr in ((qc_ref, ic_ref, zbc_ref if reverse else zfc_ref, oc_scr),
                                           (ql_ref, il_ref, zbl_ref if reverse else zfl_ref, ol_scr)):
            n_steps = q_ref.shape[0] // c

            def body(j, st, q_ref=q_ref, i_ref=i_ref, z_ref=z_ref, o_scr=o_scr, n_steps=n_steps,
                     reverse=reverse, tri=tri, lb=lb):
                idx = (n_steps - 1 - j) if reverse else j
                sl = pl.ds(pl.multiple_of(idx * c, c), c)
                o, st = _hgrn_step(q_ref[sl, :], i_ref[sl, :], z_ref[sl, :], lb, st, tri, ones_bf, reverse)
                if reverse:
                    o_scr[sl, :] += o
                else:
                    o_scr[sl, :] = o
                return st

            st = lax.fori_loop(0, n_steps, body, st)

    ng = ng_ref[...]
    yl_ref[...] = (_rms(ol_scr[...]) * ng * _silu(gl_ref[...])).astype(yl_ref.dtype)
    yc_ref[...] = (_rms(oc_scr[...]) * ng * _silu(gc_ref[...])).astype(yc_ref.dtype)


def _hgrn2(p_lat, p_ctx, lb_logits, norm_g, layer, batch):
    n = p_lat.shape[0] // batch
    lc = p_ctx.shape[0] // batch
    h = HGRN_HEADS
    n_layers = lb_logits.shape[1]
    lat = lambda part: pl.BlockSpec((n, LANES), lambda b, hh: (b, part * h + hh))
    ctx = lambda part: pl.BlockSpec((lc, LANES), lambda b, hh: (b, part * h + hh))
    return pl.pallas_call(
        functools.partial(_hgrn_kernel, layer=layer),
        grid=(batch, h),
        in_specs=[lat(0), lat(1), lat(2), lat(3), lat(4), ctx(0), ctx(1), ctx(2), ctx(3), ctx(4),
                  pl.BlockSpec((2, n_layers, LANES), lambda b, hh: (0, 0, hh)),
                  pl.BlockSpec((1, LANES), lambda b, hh: (0, hh))],
        out_specs=[pl.BlockSpec((n, LANES), lambda b, hh: (b, hh)),
                   pl.BlockSpec((lc, LANES), lambda b, hh: (b, hh))],
        out_shape=[jax.ShapeDtypeStruct((batch * n, h * LANES), BF16),
                   jax.ShapeDtypeStruct((batch * lc, h * LANES), BF16)],
        scratch_shapes=[pltpu.VMEM((n, LANES), F32), pltpu.VMEM((lc, LANES), F32)],
        compiler_params=_cparams(("arbitrary", "arbitrary")),
    )(p_lat, p_lat, p_lat, p_lat, p_lat, p_ctx, p_ctx, p_ctx, p_ctx, p_ctx,
      lb_logits, norm_g[layer].reshape(1, -1))


def _dft_matrices(n):
    idx = (np.arange(n)[:, None] * np.arange(n)[None, :]) % (2 * n)
    ang = idx.astype(np.float64) * (math.pi / n)
    cm = np.cos(ang)
    sf = np.sin(ang)
    sf[0, :] = (-1.0) ** np.arange(n)
    return (jnp.asarray(cm, F32).astype(BF16), jnp.asarray(sf, F32).astype(BF16),
            jnp.asarray(sf.T, F32).astype(BF16))


def _filter_features(n):
    pos = np.arange(n, dtype=np.float64)
    t = pos / max(n - 1, 1)
    bands = np.linspace(1e-4, HYENA_BANDS - 1, HYENA_BANDS)
    ang = (2.0 * math.pi / n) * pos[:, None] * bands[None, :]
    z = np.concatenate([t[:, None], np.cos(ang), -np.sin(ang)], -1)
    max_decay = math.log(HYENA_DECAY_TARGET) / HYENA_FAST_PCT
    min_decay = math.log(HYENA_DECAY_TARGET) / HYENA_SLOW_PCT
    deltas = np.abs(np.linspace(min_decay, max_decay, HYENA_WIDTH))
    return jnp.asarray(z, F32), jnp.asarray(t[:, None], F32), jnp.asarray(deltas[None, :], F32)


def _filter_kernel(z_ref, t_ref, dl_ref, w1_ref, b1_ref, w2_ref, b2_ref, w3_ref, b3_ref, fr_ref, wo_ref,
                   o_ref, hdn_ref):
    j = pl.program_id(0)

    @pl.when(j == 0)
    def _():
        fr = fr_ref[...]
        hdn = jnp.sin(fr * (_dot_f32(z_ref[...], w1_ref[...]) + b1_ref[...]))
        hdn = jnp.sin(fr * (_dot_f32(hdn, w2_ref[...]) + b2_ref[...]))
        hdn_ref[...] = jnp.sin(fr * (_dot_f32(hdn, w3_ref[...]) + b3_ref[...]))

    filt = _dot_f32(hdn_ref[...], wo_ref[...]) * jnp.exp(-t_ref[...] * dl_ref[...])
    row = lax.broadcasted_iota(jnp.int32, filt.shape, 0)
    is_bwd = j >= pl.num_programs(0) // 2
    o_ref[...] = jnp.where(jnp.logical_and(is_bwd, row == 0), 0.0, filt).astype(o_ref.dtype)


def _hyena_filters(n, w1, b1, w2, b2, w3, b3, freq, w_out, tc=512):
    z, t, deltas = _filter_features(n)
    hid = HYENA_FILT_HIDDEN
    nct = HYENA_WIDTH // tc
    full = lambda shape: pl.BlockSpec(shape, lambda j: (0,) * len(shape))
    return pl.pallas_call(
        _filter_kernel,
        grid=(2 * nct,),
        in_specs=[full((n, HYENA_EMB)), full((n, 1)),
                  pl.BlockSpec((1, tc), lambda j: (0, j % nct)),
                  full((HYENA_EMB, hid)), full((1, hid)), full((hid, hid)), full((1, hid)),
                  full((hid, hid)), full((1, hid)), full((1, hid)),
                  pl.BlockSpec((hid, tc), lambda j: (0, j))],
        out_specs=pl.BlockSpec((n, tc), lambda j: (0, j)),
        out_shape=jax.ShapeDtypeStruct((n, 2 * HYENA_WIDTH), BF16),
        scratch_shapes=[pltpu.VMEM((n, hid), F32)],
        compiler_params=_cparams(("arbitrary",)),
    )(z, t, deltas, w1, b1.reshape(1, hid), w2, b2.reshape(1, hid), w3, b3.reshape(1, hid),
      freq.reshape(1, hid), w_out)


def _spectrum_kernel(cm_ref, sf_ref, hf_ref, hb_ref, a_ref, b_ref, *, inv_len):
    cm, sf, hf, hb = cm_ref[...], sf_ref[...], hf_ref[...], hb_ref[...]
    kr = _dot(cm, hf) + _dot(cm, hb)
    d1 = _dot(sf, hf)
    d2 = _dot(sf, hb)
    row = lax.broadcasted_iota(jnp.int32, kr.shape, 0) + pl.program_id(0) * kr.shape[0]
    first = row == 0
    w = jnp.where(first, inv_len, 2.0 * inv_len)
    a_ref[...] = kr * w
    b_ref[...] = jnp.where(first, d1 + d2, d1 - d2) * w


def _filter_spectrum(filt, cm, sf, tk=512, tc=512):
    n = filt.shape[0]
    tk = min(tk, n)
    nct = HYENA_WIDTH // tc
    out = jax.ShapeDtypeStruct((n, HYENA_WIDTH), F32)
    return pl.pallas_call(
        functools.partial(_spectrum_kernel, inv_len=1.0 / (2 * n)),
        grid=(n // tk, nct),
        in_specs=[pl.BlockSpec((tk, n), lambda i, j: (i, 0)),
                  pl.BlockSpec((tk, n), lambda i, j: (i, 0)),
                  pl.BlockSpec((n, tc), lambda i, j: (0, j)),
                  pl.BlockSpec((n, tc), lambda i, j: (0, nct + j))],
        out_specs=[pl.BlockSpec((tk, tc), lambda i, j: (i, j)), pl.BlockSpec((tk, tc), lambda i, j: (i, j))],
        out_shape=[out, out],
        compiler_params=_cparams(("arbitrary", "arbitrary")),
    )(cm, sf, filt, filt)


def _hyena_gate_kernel(u0_ref, u1_ref, uv_ref, w0_ref, w1_ref, wv_ref, b0_ref, b1_ref, bv_ref,
                       x0_ref, z_ref):
    n = u0_ref.shape[0]
    row = lax.broadcasted_iota(jnp.int32, u0_ref.shape, 0)

    def conv(u_ref, w_ref, b_ref):
        u = u_ref[...]
        prev = jnp.where(row == 0, 0.0, pltpu.roll(u, 1, 0))
        nxt = jnp.where(row == n - 1, 0.0, pltpu.roll(u, n - 1, 0))
        return b_ref[...] + prev * w_ref[0:1] + u * w_ref[1:2] + nxt * w_ref[2:3]

    x0_ref[...] = conv(u0_ref, w0_ref, b0_ref)
    z_ref[...] = conv(uv_ref, wv_ref, bv_ref) * conv(u1_ref, w1_ref, b1_ref)


def _hyena_gate(p, first_blk, conv_w, conv_b, batch, tc=256):
    n = p.shape[0] // batch
    nct = HYENA_WIDTH // tc
    c0 = first_blk * LANES // tc
    u = lambda part: pl.BlockSpec((n, tc), lambda b, j: (b, c0 + part * nct + j))
    w = lambda part: pl.BlockSpec((HYENA_SHORT, tc), lambda b, j: (0, part * nct + j))
    bb = lambda part: pl.BlockSpec((1, tc), lambda b, j: (0, part * nct + j))
    out = jax.ShapeDtypeStruct((batch * n, HYENA_WIDTH), F32)
    cb = conv_b.reshape(1, -1)
    return pl.pallas_call(
        _hyena_gate_kernel,
        grid=(batch, nct),
        in_specs=[u(0), u(1), u(2), w(0), w(1), w(2), bb(0), bb(1), bb(2)],
        out_specs=[pl.BlockSpec((n, tc), lambda b, j: (b, j)), pl.BlockSpec((n, tc), lambda b, j: (b, j))],
        out_shape=[out, out],
        compiler_params=_cparams(("arbitrary", "arbitrary")),
    )(p, p, p, conv_w, conv_w, conv_w, cb, cb, cb)


def _dft_fwd_kernel(cm_ref, sf_ref, z_ref, a_ref, b_ref, pr_ref, ps_ref):
    z = z_ref[...].astype(BF16)
    zr = _dot(cm_ref[...], z)
    zs = _dot(sf_ref[...], z)
    a, b = a_ref[...], b_ref[...]
    row = lax.broadcasted_iota(jnp.int32, zr.shape, 0) + pl.program_id(1) * zr.shape[0]
    first = row == 0
    pr_ref[...] = (zr * a - jnp.where(first, 0.0, zs * b)).astype(pr_ref.dtype)
    ps_ref[...] = (jnp.where(first, 0.0, zr * b) + zs * jnp.where(first, b, a)).astype(ps_ref.dtype)


def _dft_inv_kernel(cm_ref, si_ref, pr_ref, ps_ref, z_ref, x0_ref, skip_ref, o_ref):
    y = _dot(cm_ref[...], pr_ref[...]) + _dot(si_ref[...], ps_ref[...])
    o_ref[...] = (x0_ref[...] * (y + z_ref[...] * skip_ref[...])).astype(o_ref.dtype)


def _long_conv(z, x0, spec_a, spec_b, skip, cm, sf, si, batch, tk=512, tc=512):
    n = z.shape[0] // batch
    tk = min(tk, n)
    nk = n // tk
    nct = HYENA_WIDTH // tc
    mat = pl.BlockSpec((tk, n), lambda b, i, j: (i, 0))
    col = pl.BlockSpec((n, tc), lambda b, i, j: (b, j))
    tile_nb = pl.BlockSpec((tk, tc), lambda b, i, j: (i, j))
    tile = pl.BlockSpec((tk, tc), lambda b, i, j: (b * nk + i, j))
    spec_shape = jax.ShapeDtypeStruct((batch * n, HYENA_WIDTH), BF16)
    pr, ps = pl.pallas_call(
        _dft_fwd_kernel,
        grid=(batch, nk, nct),
        in_specs=[mat, mat, col, tile_nb, tile_nb],
        out_specs=[tile, tile],
        out_shape=[spec_shape, spec_shape],
        compiler_params=_cparams(("arbitrary", "arbitrary", "arbitrary")),
    )(cm, sf, z, spec_a, spec_b)
    return pl.pallas_call(
        _dft_inv_kernel,
        grid=(batch, nk, nct),
        in_specs=[mat, mat, col, col, tile, tile, pl.BlockSpec((1, tc), lambda b, i, j: (0, j))],
        out_specs=tile,
        out_shape=jax.ShapeDtypeStruct((batch * n, HYENA_WIDTH), BF16),
        compiler_params=_cparams(("arbitrary", "arbitrary", "arbitrary")),
    )(cm, si, pr, ps, z, x0, skip.reshape(1, -1))


def _hyena(p, first_blk, conv_w, conv_b, filt_params, skip, batch):
    n = p.shape[0] // batch
    cm, sf, si = _dft_matrices(n)
    filt = _hyena_filters(n, *filt_params)
    spec_a, spec_b = _filter_spectrum(filt, cm, sf)
    x0, z = _hyena_gate(p, first_blk, conv_w, conv_b, batch)
    return _long_conv(z, x0, spec_a, spec_b, skip, cm, sf, si, batch)


def kernel(x, c, ctx, c_ctx, ada_w, ada_b, norm_mix_g, norm_mlp_g, w_out, mlp_w1, mlp_w2, final_norm_g, ev_w_in, mla_kv_norm_g, mla_w_ukv, na_rel_bias, od_w_in, hgrn_lb_logits, hgrn_norm_g, hy_conv_w, hy_conv_b, hy_filt_w1, hy_filt_b1, hy_filt_w2, hy_filt_b2, hy_filt_w3, hy_filt_b3, hy_filt_freq, hy_filt_wout, hy_skip):
    batch, seq, d = x.shape
    lc = ctx.shape[1]
    depth = ada_w.shape[0]
    h_lat = x.reshape(batch * seq, d)
    h_ctx = ctx.reshape(batch * lc, d)

    cond = jnp.concatenate([c, c_ctx[None, :], jnp.zeros((8 - batch - 1, d), F32)], axis=0)
    mod_all = _ada_modulation(cond, ada_w, ada_b)
    cos_tab, sin_tab = _rope_tables(seq)
    hgrn_cols = 5 * HGRN_WIDTH

    for l in range(depth):
        ctx_out = l < depth - 1
        mod3 = mod_all[l].reshape(8, 1, 6 * d)
        lat_mod = lambda *chunks: (mod3, *chunks, seq, 0)
        ctx_mod = lambda *chunks: (mod3, *chunks, batch * lc, batch)
        if l % 2 == 0:
            e = l // 2
            w_in = _even_w_in(ev_w_in[e])
            p_lat = _norm_proj(h_lat, 0, d, norm_mix_g[l], w_in, BF16, lat_mod(0, 1))
            p_ctx = _norm_proj(h_ctx, 0, d, norm_mix_g[l], w_in, BF16, ctx_mod(0, 1))
            w_ukv = mla_w_ukv[e].astype(BF16)
            ckv_blk = EV_CKV_BLK * LANES // MLA_KV_RANK
            kv_lat = _norm_proj(p_lat, ckv_blk, MLA_KV_RANK, mla_kv_norm_g[e], w_ukv, BF16)
            kv_ctx = _norm_proj(p_ctx, ckv_blk, MLA_KV_RANK, mla_kv_norm_g[e], w_ukv, BF16)
            qpe_rot = _rope(p_lat, EV_QMLA_BLK + 1, MLA_HEADS, cos_tab, sin_tab)
            kpe_rot = _rope(p_lat, EV_KPE_BLK, 1, cos_tab, sin_tab)
            y1_lat = _mla_latent(p_lat, p_ctx, kv_lat, kv_ctx, qpe_rot, kpe_rot, batch)
            y2_lat = _na_latent(p_lat, p_ctx, _na_bias_table(na_rel_bias[e]), batch)
            if ctx_out:
                y1_ctx, y2_ctx = _ctx_attention(p_ctx, kv_ctx, batch)
        else:
            o = l // 2
            w_in = od_w_in[o].astype(BF16)
            p_lat = _norm_proj(h_lat, 0, d, norm_mix_g[l], w_in, F32, lat_mod(0, 1))
            w_in_ctx = w_in if ctx_out else w_in[:, :hgrn_cols]
            p_ctx = _norm_proj(h_ctx, 0, d, norm_mix_g[l], w_in_ctx, F32, ctx_mod(0, 1))
            y1_lat, y1_ctx = _hgrn2(p_lat, p_ctx, hgrn_lb_logits, hgrn_norm_g, o, batch)
            filt_params = (hy_filt_w1[o], hy_filt_b1[o], hy_filt_w2[o], hy_filt_b2[o], hy_filt_w3[o],
                           hy_filt_b3[o], hy_filt_freq[o], hy_filt_wout[o])
            y2_lat = _hyena(p_lat, hgrn_cols // LANES, hy_conv_w[o], hy_conv_b[o], filt_params,
                            hy_skip[o], batch)
            if ctx_out:
                y2_ctx = _hyena(p_ctx, hgrn_cols // LANES, hy_conv_w[o], hy_conv_b[o], filt_params,
                                hy_skip[o], batch)
        wo = w_out[l].astype(BF16)
        w1 = mlp_w1[l].astype(BF16)
        w2 = mlp_w2[l].astype(BF16)
        h_lat = _out_proj(y1_lat, y2_lat, wo, h_lat, lat_mod(2))
        h_lat = _mlp(h_lat, norm_mlp_g[l], w1, w2, lat_mod(3, 4, 5),
                     final_g=None if ctx_out else final_norm_g)
        if ctx_out:
            h_ctx = _out_proj(y1_ctx, y2_ctx, wo, h_ctx, ctx_mod(2))
            h_ctx = _mlp(h_ctx, norm_mlp_g[l], w1, w2, ctx_mod(3, 4, 5))
    return h_lat.reshape(batch, seq, d)
```

```python
import functools
import math

import numpy as np
import jax
import jax.numpy as jnp
from jax import lax
from jax.experimental import pallas as pl
from jax.experimental.pallas import tpu as pltpu

F32 = jnp.float32
BF16 = jnp.bfloat16

D_MODEL = 2048
DEPTH = 4
GRID_W = 64
HEAD_DIM = 128
MLA_HEADS = 8
MLA_NOPE_DIM = 128
MLA_ROPE_DIM = 64
MLA_QK_DIM = MLA_NOPE_DIM + MLA_ROPE_DIM
MLA_KV_RANK = 512
NA_HEADS = 8
NA_DIM = 128
NA_KH = 8
NA_KW = 16
ROPE_THETA = 10000.0
HGRN_WIDTH = 1024
HGRN_HEADS = 8
FORGET_FLOOR = 1e-30
HYENA_WIDTH = 1024
HYENA_SHORT = 3
HYENA_EMB = 33
HYENA_BANDS = (HYENA_EMB - 1) // 2
HYENA_FILT_HIDDEN = 64
HYENA_DECAY_TARGET = 1e-2
HYENA_FAST_PCT = 0.3
HYENA_SLOW_PCT = 1.5
MLP_HIDDEN = 4 * D_MODEL
NORM_EPS = 1e-6
NEG_INF = -1e30

LANES = 128
VMEM_LIMIT_BYTES = 56 * 1024 * 1024

EV_QMLA_BLK = 0
EV_QNA_BLK = 16
EV_KNA_BLK = 24
EV_VNA_BLK = 32
EV_CKV_BLK = 40
EV_KPE_BLK = 44
EV_WIDTH = 48 * LANES
HGRN_CHUNK = 16


def _cparams(sem):
    return pltpu.CompilerParams(dimension_semantics=sem, vmem_limit_bytes=VMEM_LIMIT_BYTES)


def _dot(a, b):
    return jnp.dot(a, b, preferred_element_type=F32)


def _dot_t(a, b):
    return lax.dot_general(a, b, (((1,), (1,)), ((), ())), preferred_element_type=F32)


def _dot_tn(a, b):
    return lax.dot_general(a, b, (((0,), (0,)), ((), ())), preferred_element_type=F32)


def _dot_f32(a, b):
    return jnp.dot(a, b, preferred_element_type=F32, precision=lax.Precision.HIGHEST)


def _sigmoid(x):
    return 1.0 / (1.0 + jnp.exp(-x))


def _silu(x):
    return x * _sigmoid(x)


def _rms(x):
    return x * lax.rsqrt(jnp.mean(x * x, axis=-1, keepdims=True) + NORM_EPS)


def _ada_kernel(s_ref, w_ref, b_ref, o_ref):
    s = _silu(s_ref[...]).astype(BF16)
    o_ref[...] = _dot(s, w_ref[...].astype(BF16)) + b_ref[...]


def _ada_modulation(cond, ada_w, ada_b, tn=1024):
    depth, d, n = ada_w.shape
    rows = cond.shape[0]
    return pl.pallas_call(
        _ada_kernel,
        grid=(depth, n // tn),
        in_specs=[
            pl.BlockSpec((rows, d), lambda l, j: (0, 0)),
            pl.BlockSpec((None, d, tn), lambda l, j: (l, 0, j)),
            pl.BlockSpec((None, 1, tn), lambda l, j: (l, 0, j)),
        ],
        out_specs=pl.BlockSpec((None, rows, tn), lambda l, j: (l, 0, j)),
        out_shape=jax.ShapeDtypeStruct((depth, rows, n), F32),
        compiler_params=_cparams(("arbitrary", "arbitrary")),
    )(cond, ada_w, ada_b.reshape(depth, 1, n))


def _mod_spec(chunk, tiles_per_group, group0, d):
    return pl.BlockSpec((None, 1, d), lambda i, *_: (group0 + i // tiles_per_group, 0, chunk))


def _proj_kernel(*refs, modulated):
    if modulated:
        x_ref, g_ref, sh_ref, sc_ref, w_ref, o_ref, a_ref = refs
    else:
        x_ref, g_ref, w_ref, o_ref, a_ref = refs

    @pl.when(pl.program_id(1) == 0)
    def _():
        y = _rms(x_ref[...].astype(F32)) * g_ref[...]
        if modulated:
            y = y * (1.0 + sc_ref[...]) + sh_ref[...]
        a_ref[...] = y.astype(BF16)

    o_ref[...] = _dot(a_ref[...], w_ref[...]).astype(o_ref.dtype)


def _layer_spec(w, layer, block, index_map):
    if w.ndim == 2:
        return pl.BlockSpec(block, index_map)
    return pl.BlockSpec((None,) + block, lambda *idx: (layer,) + index_map(*idx))


def _norm_proj(x, x_col_blk, k, g, w, out_dtype, mod=None, layer=None, n=None, tm=512, tn=1024):
    m = x.shape[0]
    n = w.shape[-1] if n is None else n
    tm = min(tm, m)
    tn = min(tn, n)
    assert m % tm == 0 and n % tn == 0
    in_specs = [pl.BlockSpec((tm, k), lambda i, j: (i, x_col_blk)),
                pl.BlockSpec((1, k), lambda i, j: (0, 0))]
    args = [x, g.reshape(1, k)]
    if mod is not None:
        mod3, sh_chunk, sc_chunk, tiles_per_group, group0 = mod
        in_specs += [_mod_spec(sh_chunk, tiles_per_group // tm, group0, k),
                     _mod_spec(sc_chunk, tiles_per_group // tm, group0, k)]
        args += [mod3, mod3]
    in_specs.append(_layer_spec(w, layer, (k, tn), lambda i, j: (0, j)))
    args.append(w)
    return pl.pallas_call(
        functools.partial(_proj_kernel, modulated=mod is not None),
        grid=(m // tm, n // tn),
        in_specs=in_specs,
        out_specs=pl.BlockSpec((tm, tn), lambda i, j: (i, j)),
        out_shape=jax.ShapeDtypeStruct((m, n), out_dtype),
        scratch_shapes=[pltpu.VMEM((tm, k), BF16)],
        compiler_params=_cparams(("arbitrary", "arbitrary")),
    )(*args)


def _outproj_kernel(y1_ref, y2_ref, w_ref, h_ref, gate_ref, o_ref):
    k1 = y1_ref.shape[1]
    acc = _dot(y1_ref[...], w_ref[:k1, :]) + _dot(y2_ref[...], w_ref[k1:, :])
    o_ref[...] = h_ref[...] + gate_ref[...] * acc


def _out_proj(y1, y2, w, layer, h, mod, tm=512):
    m, d = h.shape
    k1, k2 = y1.shape[1], y2.shape[1]
    tm = min(tm, m)
    mod3, gate_chunk, rows_per_group, group0 = mod
    return pl.pallas_call(
        _outproj_kernel,
        grid=(m // tm,),
        in_specs=[
            pl.BlockSpec((tm, k1), lambda i: (i, 0)),
            pl.BlockSpec((tm, k2), lambda i: (i, 0)),
            _layer_spec(w, layer, (k1 + k2, d), lambda i: (0, 0)),
            pl.BlockSpec((tm, d), lambda i: (i, 0)),
            _mod_spec(gate_chunk, rows_per_group // tm, group0, d),
        ],
        out_specs=pl.BlockSpec((tm, d), lambda i: (i, 0)),
        out_shape=jax.ShapeDtypeStruct((m, d), F32),
        compiler_params=_cparams(("arbitrary",)),
    )(y1, y2, w, h, mod3)


def _mlp_kernel(*refs, final_norm):
    if final_norm:
        h_ref, g_ref, sh_ref, sc_ref, gate_ref, w1_ref, w2_ref, fg_ref, o_ref, a_ref, acc_ref = refs
    else:
        h_ref, g_ref, sh_ref, sc_ref, gate_ref, w1_ref, w2_ref, o_ref, a_ref, acc_ref = refs
    k = pl.program_id(1)

    @pl.when(k == 0)
    def _():
        y = _rms(h_ref[...]) * g_ref[...]
        a_ref[...] = (y * (1.0 + sc_ref[...]) + sh_ref[...]).astype(BF16)
        acc_ref[...] = jnp.zeros_like(acc_ref)

    u = jnp.maximum(_dot(a_ref[...], w1_ref[...]), 0.0)
    acc_ref[...] += _dot((u * u).astype(BF16), w2_ref[...])

    @pl.when(k == pl.num_programs(1) - 1)
    def _():
        out = h_ref[...] + gate_ref[...] * acc_ref[...]
        if final_norm:
            out = _rms(out) * fg_ref[...]
        o_ref[...] = out


def _mlp(h, g, w1, w2, layer, mod, final_g=None, tm=512, th=1024):
    m, d = h.shape
    hid = w1.shape[-1]
    tm = min(tm, m)
    mod3, sh_chunk, sc_chunk, gate_chunk, rows_per_group, group0 = mod
    tpg = rows_per_group // tm
    in_specs = [
        pl.BlockSpec((tm, d), lambda i, k: (i, 0)),
        pl.BlockSpec((1, d), lambda i, k: (0, 0)),
        _mod_spec(sh_chunk, tpg, group0, d),
        _mod_spec(sc_chunk, tpg, group0, d),
        _mod_spec(gate_chunk, tpg, group0, d),
        _layer_spec(w1, layer, (d, th), lambda i, k: (0, k)),
        _layer_spec(w2, layer, (th, d), lambda i, k: (k, 0)),
    ]
    args = [h, g.reshape(1, d), mod3, mod3, mod3, w1, w2]
    if final_g is not None:
        in_specs.append(pl.BlockSpec((1, d), lambda i, k: (0, 0)))
        args.append(final_g.reshape(1, d))
    return pl.pallas_call(
        functools.partial(_mlp_kernel, final_norm=final_g is not None),
        grid=(m // tm, hid // th),
        in_specs=in_specs,
        out_specs=pl.BlockSpec((tm, d), lambda i, k: (i, 0)),
        out_shape=jax.ShapeDtypeStruct((m, d), F32),
        scratch_shapes=[pltpu.VMEM((tm, d), BF16), pltpu.VMEM((tm, d), F32)],
        compiler_params=_cparams(("arbitrary", "arbitrary")),
    )(*args)


def _softmax_pv(scores, values):
    m = functools.reduce(jnp.maximum, [jnp.max(s, axis=-1, keepdims=True) for s in scores])
    ps = [jnp.exp(s - m) for s in scores]
    denom = functools.reduce(jnp.add, [jnp.sum(p, axis=-1, keepdims=True) for p in ps])
    o = functools.reduce(jnp.add, [_dot(p.astype(BF16), v) for p, v in zip(ps, values)])
    return o / denom


def _rope_kernel(x_ref, cos_ref, sin_ref, o_ref):
    x = x_ref[...].astype(F32)
    lane = lax.broadcasted_iota(jnp.int32, x.shape, 1)
    partner = jnp.where((lane % 32) < 16, pltpu.roll(x, LANES - 16, 1), pltpu.roll(x, 16, 1))
    o_ref[...] = (x * cos_ref[...] + partner * sin_ref[...]).astype(o_ref.dtype)


def _rope(p, first_blk, n_blk, cos_tab, sin_tab, tm=1024):
    m = p.shape[0]
    seq = cos_tab.shape[0]
    tm = min(tm, seq)
    per_seq = seq // tm
    return pl.pallas_call(
        _rope_kernel,
        grid=(m // tm, n_blk),
        in_specs=[
            pl.BlockSpec((tm, LANES), lambda i, j: (i, first_blk + 2 * j)),
            pl.BlockSpec((tm, LANES), lambda i, j: (i % per_seq, 0)),
            pl.BlockSpec((tm, LANES), lambda i, j: (i % per_seq, 0)),
        ],
        out_specs=pl.BlockSpec((tm, LANES), lambda i, j: (i, j)),
        out_shape=jax.ShapeDtypeStruct((m, n_blk * LANES), BF16),
        compiler_params=_cparams(("arbitrary", "arbitrary")),
    )(p, cos_tab, sin_tab)


def _mla_lat_kernel(qn_ref, qpe_ref, knc_ref, kpec_ref, vc_ref, knl_ref, kpel_ref, vl_ref,
                    o_ref, k_scr, v_scr, *, scale):
    lc = knc_ref.shape[0]

    @pl.when(pl.program_id(2) == 0)
    def _():
        k_scr[:lc, :LANES] = knc_ref[...]
        k_scr[:lc, LANES:] = kpec_ref[...]
        k_scr[lc:, :LANES] = knl_ref[...]
        k_scr[lc:, LANES:] = kpel_ref[...]
        v_scr[:lc, :] = vc_ref[...]
        v_scr[lc:, :] = vl_ref[...]

    sub = 128
    for i in range(qn_ref.shape[0] // sub):
        rows = slice(i * sub, (i + 1) * sub)
        q = jnp.concatenate([qn_ref[rows, :], qpe_ref[rows, :]], axis=1)
        s = _dot_t(q, k_scr[...]) * scale
        o_ref[rows, :] = _softmax_pv([s], [v_scr[...]]).astype(o_ref.dtype)


def _mla_latent(p_lat, p_ctx, kv_lat, kv_ctx, qpe_rot, kpe_rot, batch, tq=512):
    n = p_lat.shape[0] // batch
    lc = p_ctx.shape[0] // batch
    nq = n // tq
    h = MLA_HEADS
    blk = lambda rows, f: pl.BlockSpec((rows, LANES), f)
    return pl.pallas_call(
        functools.partial(_mla_lat_kernel, scale=MLA_QK_DIM ** -0.5),
        grid=(batch, h, nq),
        in_specs=[
            blk(tq, lambda b, hh, i: (b * nq + i, EV_QMLA_BLK + 2 * hh)),
            blk(tq, lambda b, hh, i: (b * nq + i, hh)),
            blk(lc, lambda b, hh, i: (b, 2 * hh)),
            blk(lc, lambda b, hh, i: (b, EV_KPE_BLK)),
            blk(lc, lambda b, hh, i: (b, 2 * hh + 1)),
            blk(n, lambda b, hh, i: (b, 2 * hh)),
            blk(n, lambda b, hh, i: (b, 0)),
            blk(n, lambda b, hh, i: (b, 2 * hh + 1)),
        ],
        out_specs=blk(tq, lambda b, hh, i: (b * nq + i, hh)),
        out_shape=jax.ShapeDtypeStruct((batch * n, h * LANES), BF16),
        scratch_shapes=[pltpu.VMEM((lc + n, 2 * LANES), BF16), pltpu.VMEM((lc + n, LANES), BF16)],
        compiler_params=_cparams(("arbitrary", "arbitrary", "arbitrary")),
    )(p_lat, qpe_rot, kv_ctx, p_ctx, kv_ctx, kv_lat, kpe_rot, kv_lat)


def _ctx_attn_kernel(qm_ref, kn_ref, kpe_ref, vm_ref, qn_ref, kna_ref, vna_ref, om_ref, on_ref,
                     *, mla_scale, na_scale):
    k = jnp.concatenate([kn_ref[...], kpe_ref[...]], axis=1)
    s = _dot_t(qm_ref[...], k) * mla_scale
    om_ref[...] = _softmax_pv([s], [vm_ref[...]]).astype(om_ref.dtype)
    s = _dot_t(qn_ref[...], kna_ref[...]) * na_scale
    on_ref[...] = _softmax_pv([s], [vna_ref[...]]).astype(on_ref.dtype)


def _ctx_attention(p_ctx, kv_ctx, batch):
    lc = p_ctx.shape[0] // batch
    h = MLA_HEADS
    blk = lambda f: pl.BlockSpec((lc, LANES), f)
    out = jax.ShapeDtypeStruct((batch * lc, h * LANES), BF16)
    return pl.pallas_call(
        functools.partial(_ctx_attn_kernel, mla_scale=MLA_QK_DIM ** -0.5, na_scale=NA_DIM ** -0.5),
        grid=(batch, h),
        in_specs=[
            pl.BlockSpec((lc, 2 * LANES), lambda b, hh: (b, hh)),
            blk(lambda b, hh: (b, 2 * hh)),
            blk(lambda b, hh: (b, EV_KPE_BLK)),
            blk(lambda b, hh: (b, 2 * hh + 1)),
            blk(lambda b, hh: (b, EV_QNA_BLK + hh)),
            blk(lambda b, hh: (b, EV_KNA_BLK + hh)),
            blk(lambda b, hh: (b, EV_VNA_BLK + hh)),
        ],
        out_specs=[blk(lambda b, hh: (b, hh)), blk(lambda b, hh: (b, hh))],
        out_shape=[out, out],
        compiler_params=_cparams(("arbitrary", "arbitrary")),
    )(p_ctx, kv_ctx, p_ctx, kv_ctx, p_ctx, p_ctx, p_ctx)


def _na_kernel(q_ref, k_ref, v_ref, kc_ref, vc_ref, bias_ref, o_ref, *, scale, n_rows):
    kc = kc_ref[...]
    vc = vc_ref[...]
    win = NA_KH * GRID_W
    for r in range(n_rows):
        ws = min(max(r - NA_KH // 2, 0), n_rows - NA_KH)
        qr = q_ref[r * GRID_W:(r + 1) * GRID_W, :]
        s = _dot_t(qr, k_ref[ws * GRID_W:ws * GRID_W + win, :]) * scale + bias_ref[r - ws]
        sc = _dot_t(qr, kc) * scale
        o = _softmax_pv([s, sc], [v_ref[ws * GRID_W:ws * GRID_W + win, :], vc])
        o_ref[r * GRID_W:(r + 1) * GRID_W, :] = o.astype(o_ref.dtype)


def _na_bias_kernel(rb_ref, onehot_ref, mask_ref, o_ref):
    o_ref[...] = _dot_f32(rb_ref[...], onehot_ref[...]) + mask_ref[...]


def _na_bias_table(rel_bias):
    n_heads, n_ro, n_co = rel_bias.shape
    col = np.arange(GRID_W)
    col_start = np.clip(col - NA_KW // 2, 0, GRID_W - NA_KW)
    col_mask = (col[None, :] >= col_start[:, None]) & (col[None, :] < col_start[:, None] + NA_KW)
    col_off = np.clip(col[None, :] - col[:, None], 1 - NA_KW, NA_KW - 1) + (NA_KW - 1)
    onehot = (col_off.reshape(1, -1) == np.arange(n_co)[:, None]).astype(np.float32)
    mask_add = np.where(col_mask.reshape(1, -1), 0.0, NEG_INF).astype(np.float32)
    qw = GRID_W * GRID_W
    full = lambda shape: pl.BlockSpec(shape, lambda: (0,) * len(shape))
    cols = pl.pallas_call(
        _na_bias_kernel,
        in_specs=[full((n_heads * n_ro, n_co)), full((n_co, qw)), full((1, qw))],
        out_specs=full((n_heads * n_ro, qw)),
        out_shape=jax.ShapeDtypeStruct((n_heads * n_ro, qw), F32),
    )(rel_bias.reshape(n_heads * n_ro, n_co), jnp.asarray(onehot), jnp.asarray(mask_add))
    cols = cols.reshape(n_heads, n_ro, GRID_W, GRID_W)
    t = jnp.stack([cols[:, NA_KH - 1 - e:2 * NA_KH - 1 - e] for e in range(NA_KH)], axis=1)
    return t.transpose(0, 1, 3, 2, 4).reshape(n_heads, NA_KH, GRID_W, NA_KH * GRID_W)


def _na_latent(p_lat, p_ctx, bias_tab, batch):
    n = p_lat.shape[0] // batch
    lc = p_ctx.shape[0] // batch
    h = NA_HEADS
    n_rows = n // GRID_W
    assert n_rows >= NA_KH
    blk = lambda rows, f: pl.BlockSpec((rows, LANES), f)
    return pl.pallas_call(
        functools.partial(_na_kernel, scale=NA_DIM ** -0.5, n_rows=n_rows),
        grid=(batch, h),
        in_specs=[
            blk(n, lambda b, hh: (b, EV_QNA_BLK + hh)),
            blk(n, lambda b, hh: (b, EV_KNA_BLK + hh)),
            blk(n, lambda b, hh: (b, EV_VNA_BLK + hh)),
            blk(lc, lambda b, hh: (b, EV_KNA_BLK + hh)),
            blk(lc, lambda b, hh: (b, EV_VNA_BLK + hh)),
            pl.BlockSpec((None, NA_KH, GRID_W, NA_KH * GRID_W), lambda b, hh: (hh, 0, 0, 0)),
        ],
        out_specs=blk(n, lambda b, hh: (b, hh)),
        out_shape=jax.ShapeDtypeStruct((batch * n, h * LANES), BF16),
        compiler_params=_cparams(("arbitrary", "arbitrary")),
    )(p_lat, p_lat, p_lat, p_ctx, p_ctx, bias_tab)


def _rope_tables(n):
    pos = np.arange(n)
    rows, cols = pos // GRID_W, pos % GRID_W
    half = MLA_ROPE_DIM // 2
    inv_freq = ROPE_THETA ** (-np.arange(0, half, 2, dtype=np.float64) / half)
    cos = np.zeros((n, LANES), np.float64)
    sin = np.zeros((n, LANES), np.float64)
    for base, p in ((0, rows), (half, cols)):
        ang = p[:, None].astype(np.float64) * inv_freq[None, :]
        q = half // 2
        cos[:, base:base + q] = np.cos(ang)
        cos[:, base + q:base + half] = np.cos(ang)
        sin[:, base:base + q] = -np.sin(ang)
        sin[:, base + q:base + half] = np.sin(ang)
    return jnp.asarray(cos, F32), jnp.asarray(sin, F32)


def _even_w_in(w):
    d = w.shape[0]
    z64 = jnp.zeros((d, LANES - MLA_ROPE_DIM), w.dtype)
    pieces = []
    for h in range(MLA_HEADS):
        pieces += [w[:, h * MLA_QK_DIM:h * MLA_QK_DIM + MLA_NOPE_DIM],
                   w[:, h * MLA_QK_DIM + MLA_NOPE_DIM:(h + 1) * MLA_QK_DIM], z64]
    q_end = MLA_HEADS * MLA_QK_DIM
    ckv_end = q_end + MLA_KV_RANK
    kpe_end = ckv_end + MLA_ROPE_DIM
    pieces += [w[:, kpe_end:], w[:, q_end:ckv_end], w[:, ckv_end:kpe_end], z64]
    out = jnp.concatenate(pieces, axis=1)
    pad = EV_WIDTH - out.shape[1]
    return jnp.concatenate([out, jnp.zeros((d, pad), w.dtype)], axis=1).astype(BF16)


def _hgrn_gates(q_ref, i_ref, z_refs, lbs, row0, q_s, v_s, k_s, cum_s, qd_s, kd_s, dec_s):
    c = HGRN_CHUNK
    grp = LANES
    ri = lax.broadcasted_iota(jnp.int32, (grp, grp), 0)
    ci = lax.broadcasted_iota(jnp.int32, (grp, grp), 1)
    same = (ri // c) == (ci // c)
    blk = jnp.where(same, 1.0, 0.0).astype(F32)
    tris = (jnp.where(jnp.logical_and(same, ci <= ri), 1.0, 0.0).astype(F32),
            jnp.where(jnp.logical_and(same, ci >= ri), 1.0, 0.0).astype(F32))

    def body(g, carry):
        src = pl.ds(pl.multiple_of(g * grp, grp), grp)
        dst = pl.ds(pl.multiple_of(row0 + g * grp, grp), grp)
        q = _silu(q_ref[src, :])
        q_s[dst, :] = q
        v_s[dst, :] = i_ref[src, :]
        for d in range(2):
            f = jnp.maximum(lbs[d] + (1.0 - lbs[d]) * _sigmoid(z_refs[d][src, :]), FORGET_FLOOR)
            lf = jnp.log(f)
            k = 1.0 - f
            cum = _dot_f32(tris[d], lf)
            tot = _dot_f32(blk, lf)
            k_s[d, dst, :] = k
            cum_s[d, dst, :] = cum
            qd_s[d, dst, :] = (q * jnp.exp(cum)).astype(BF16)
            kd_s[d, dst, :] = (k * jnp.exp(tot - cum)).astype(BF16)
            dec_s[d, dst, :] = jnp.exp(tot)
        return carry

    lax.fori_loop(0, q_ref.shape[0] // grp, body, 0)


def _hgrn_chunk(q, k, cum, v, qd, kd, dec, st, ones_bf, reverse):
    c = HGRN_CHUNK
    o = _dot_t(qd, st.astype(BF16))
    st_new = st * dec + _dot_tn(v.astype(BF16), kd)
    rows = lax.broadcasted_iota(jnp.int32, (c, LANES), 0)
    pieces = []
    for s in range(c):
        mask = (rows <= s) if reverse else (rows >= s)
        e = jnp.exp(jnp.where(mask, cum - cum[s:s + 1], NEG_INF))
        pieces.append((q * e * k[s:s + 1]).astype(BF16))
    r = _dot(jnp.concatenate(pieces, axis=0), ones_bf)
    for s in range(c):
        o = o + r[s * c:(s + 1) * c] * v[s:s + 1]
    return o, st_new


def _hgrn_kernel(ql_ref, il_ref, zfl_ref, zbl_ref, gl_ref, qc_ref, ic_ref, zfc_ref, zbc_ref, gc_ref,
                 lbl_ref, ng_ref, yl_ref, yc_ref,
                 q_s, v_s, k_s, cum_s, qd_s, kd_s, dec_s, o_s, *, layer):
    c = HGRN_CHUNK
    lc, n = qc_ref.shape[0], ql_ref.shape[0]
    lbs = []
    for d in range(2):
        lg = lbl_ref[d]
        ex = jnp.exp(lg - jnp.max(lg, axis=0, keepdims=True))
        p = ex / jnp.sum(ex, axis=0, keepdims=True)
        lbs.append(jnp.sum(p[:layer + 1], axis=0, keepdims=True) - p[0:1])

    scr = (q_s, v_s, k_s, cum_s, qd_s, kd_s, dec_s)
    _hgrn_gates(qc_ref, ic_ref, (zfc_ref, zbc_ref), lbs, 0, *scr)
    _hgrn_gates(ql_ref, il_ref, (zfl_ref, zbl_ref), lbs, lc, *scr)

    ones_bf = jnp.ones((LANES, LANES), BF16)
    nc_ctx = lc // c
    n_chunks = (lc + n) // c

    def body(j, sts):
        r_fwd = j * c
        r_bwd = jnp.where(j < nc_ctx, lc - c - j * c, 2 * lc + n - c - j * c)
        new = []
        for d, r in enumerate((r_fwd, r_bwd)):
            sl = pl.ds(pl.multiple_of(r, c), c)
            o, st = _hgrn_chunk(q_s[sl, :], k_s[d, sl, :], cum_s[d, sl, :], v_s[sl, :], qd_s[d, sl, :],
                                kd_s[d, sl, :], dec_s[d, sl, :][0:1], sts[d], ones_bf, reverse=d == 1)
            o_s[d, sl, :] = o
            new.append(st)
        return tuple(new)

    zero = jnp.zeros((LANES, LANES), F32)
    lax.fori_loop(0, n_chunks, body, (zero, zero), unroll=8)

    ng = ng_ref[...]
    yc_ref[...] = (_rms(o_s[0, :lc, :] + o_s[1, :lc, :]) * ng * _silu(gc_ref[...])).astype(yc_ref.dtype)
    yl_ref[...] = (_rms(o_s[0, lc:, :] + o_s[1, lc:, :]) * ng * _silu(gl_ref[...])).astype(yl_ref.dtype)


def _hgrn2(p_lat, p_ctx, lb_logits, norm_g, layer, batch):
    n = p_lat.shape[0] // batch
    lc = p_ctx.shape[0] // batch
    h = HGRN_HEADS
    n_layers = lb_logits.shape[1]
    rows = lc + n
    assert lc % LANES == 0 and n % LANES == 0
    lat = lambda part: pl.BlockSpec((n, LANES), lambda b, hh: (b, part * h + hh))
    ctx = lambda part: pl.BlockSpec((lc, LANES), lambda b, hh: (b, part * h + hh))
    return pl.pallas_call(
        functools.partial(_hgrn_kernel, layer=layer),
        grid=(batch, h),
        in_specs=[lat(0), lat(1), lat(2), lat(3), lat(4), ctx(0), ctx(1), ctx(2), ctx(3), ctx(4),
                  pl.BlockSpec((2, n_layers, LANES), lambda b, hh: (0, 0, hh)),
                  pl.BlockSpec((1, LANES), lambda b, hh: (0, hh))],
        out_specs=[pl.BlockSpec((n, LANES), lambda b, hh: (b, hh)),
                   pl.BlockSpec((lc, LANES), lambda b, hh: (b, hh))],
        out_shape=[jax.ShapeDtypeStruct((batch * n, h * LANES), BF16),
                   jax.ShapeDtypeStruct((batch * lc, h * LANES), BF16)],
        scratch_shapes=[pltpu.VMEM((rows, LANES), F32), pltpu.VMEM((rows, LANES), F32),
                        pltpu.VMEM((2, rows, LANES), F32), pltpu.VMEM((2, rows, LANES), F32),
                        pltpu.VMEM((2, rows, LANES), BF16), pltpu.VMEM((2, rows, LANES), BF16),
                        pltpu.VMEM((2, rows, LANES), F32), pltpu.VMEM((2, rows, LANES), F32)],
        compiler_params=_cparams(("arbitrary", "arbitrary")),
    )(p_lat, p_lat, p_lat, p_lat, p_lat, p_ctx, p_ctx, p_ctx, p_ctx, p_ctx,
      lb_logits, norm_g[layer].reshape(1, -1))


def _dft_matrices(n):
    idx = (np.arange(n)[:, None] * np.arange(n)[None, :]) % (2 * n)
    ang = idx.astype(np.float64) * (math.pi / n)
    cm = np.cos(ang)
    sf = np.sin(ang)
    sf[0, :] = (-1.0) ** np.arange(n)
    return (jnp.asarray(cm, F32).astype(BF16), jnp.asarray(sf, F32).astype(BF16),
            jnp.asarray(sf.T, F32).astype(BF16))


def _filter_features(n):
    pos = np.arange(n, dtype=np.float64)
    t = pos / max(n - 1, 1)
    bands = np.linspace(1e-4, HYENA_BANDS - 1, HYENA_BANDS)
    ang = (2.0 * math.pi / n) * pos[:, None] * bands[None, :]
    z = np.concatenate([t[:, None], np.cos(ang), -np.sin(ang)], -1)
    max_decay = math.log(HYENA_DECAY_TARGET) / HYENA_FAST_PCT
    min_decay = math.log(HYENA_DECAY_TARGET) / HYENA_SLOW_PCT
    deltas = np.abs(np.linspace(min_decay, max_decay, HYENA_WIDTH))
    return jnp.asarray(z, F32), jnp.asarray(t[:, None], F32), jnp.asarray(deltas[None, :], F32)


def _filter_kernel(z_ref, t_ref, dl_ref, w1_ref, b1_ref, w2_ref, b2_ref, w3_ref, b3_ref, fr_ref, wo_ref,
                   o_ref, hdn_ref):
    j = pl.program_id(0)

    @pl.when(j == 0)
    def _():
        fr = fr_ref[...]
        hdn = jnp.sin(fr * (_dot_f32(z_ref[...], w1_ref[...]) + b1_ref[...]))
        hdn = jnp.sin(fr * (_dot_f32(hdn, w2_ref[...]) + b2_ref[...]))
        hdn_ref[...] = jnp.sin(fr * (_dot_f32(hdn, w3_ref[...]) + b3_ref[...]))

    filt = _dot_f32(hdn_ref[...], wo_ref[...]) * jnp.exp(-t_ref[...] * dl_ref[...])
    row = lax.broadcasted_iota(jnp.int32, filt.shape, 0)
    is_bwd = j >= pl.num_programs(0) // 2
    o_ref[...] = jnp.where(jnp.logical_and(is_bwd, row == 0), 0.0, filt).astype(o_ref.dtype)


def _hyena_filters(n, w1, b1, w2, b2, w3, b3, freq, w_out, tc=512):
    z, t, deltas = _filter_features(n)
    hid = HYENA_FILT_HIDDEN
    nct = HYENA_WIDTH // tc
    full = lambda shape: pl.BlockSpec(shape, lambda j: (0,) * len(shape))
    return pl.pallas_call(
        _filter_kernel,
        grid=(2 * nct,),
        in_specs=[full((n, HYENA_EMB)), full((n, 1)),
                  pl.BlockSpec((1, tc), lambda j: (0, j % nct)),
                  full((HYENA_EMB, hid)), full((1, hid)), full((hid, hid)), full((1, hid)),
                  full((hid, hid)), full((1, hid)), full((1, hid)),
                  pl.BlockSpec((hid, tc), lambda j: (0, j))],
        out_specs=pl.BlockSpec((n, tc), lambda j: (0, j)),
        out_shape=jax.ShapeDtypeStruct((n, 2 * HYENA_WIDTH), BF16),
        scratch_shapes=[pltpu.VMEM((n, hid), F32)],
        compiler_params=_cparams(("arbitrary",)),
    )(z, t, deltas, w1, b1.reshape(1, hid), w2, b2.reshape(1, hid), w3, b3.reshape(1, hid),
      freq.reshape(1, hid), w_out)


def _spectrum_kernel(cm_ref, sf_ref, hf_ref, hb_ref, a_ref, b_ref, *, inv_len):
    cm, sf, hf, hb = cm_ref[...], sf_ref[...], hf_ref[...], hb_ref[...]
    kr = _dot(cm, hf) + _dot(cm, hb)
    d1 = _dot(sf, hf)
    d2 = _dot(sf, hb)
    row = lax.broadcasted_iota(jnp.int32, kr.shape, 0) + pl.program_id(0) * kr.shape[0]
    first = row == 0
    w = jnp.where(first, inv_len, 2.0 * inv_len)
    a_ref[...] = kr * w
    b_ref[...] = jnp.where(first, d1 + d2, d1 - d2) * w


def _filter_spectrum(filt, cm, sf, tk=512, tc=512):
    n = filt.shape[0]
    tk = min(tk, n)
    nct = HYENA_WIDTH // tc
    out = jax.ShapeDtypeStruct((n, HYENA_WIDTH), F32)
    return pl.pallas_call(
        functools.partial(_spectrum_kernel, inv_len=1.0 / (2 * n)),
        grid=(n // tk, nct),
        in_specs=[pl.BlockSpec((tk, n), lambda i, j: (i, 0)),
                  pl.BlockSpec((tk, n), lambda i, j: (i, 0)),
                  pl.BlockSpec((n, tc), lambda i, j: (0, j)),
                  pl.BlockSpec((n, tc), lambda i, j: (0, nct + j))],
        out_specs=[pl.BlockSpec((tk, tc), lambda i, j: (i, j)), pl.BlockSpec((tk, tc), lambda i, j: (i, j))],
        out_shape=[out, out],
        compiler_params=_cparams(("arbitrary", "arbitrary")),
    )(cm, sf, filt, filt)


def _hyena_gate_kernel(u0_ref, u1_ref, uv_ref, w0_ref, w1_ref, wv_ref, b0_ref, b1_ref, bv_ref,
                       x0_ref, z_ref):
    n = u0_ref.shape[0]
    row = lax.broadcasted_iota(jnp.int32, u0_ref.shape, 0)

    def conv(u_ref, w_ref, b_ref):
        u = u_ref[...]
        prev = jnp.where(row == 0, 0.0, pltpu.roll(u, 1, 0))
        nxt = jnp.where(row == n - 1, 0.0, pltpu.roll(u, n - 1, 0))
        return b_ref[...] + prev * w_ref[0:1] + u * w_ref[1:2] + nxt * w_ref[2:3]

    x0_ref[...] = conv(u0_ref, w0_ref, b0_ref)
    z_ref[...] = conv(uv_ref, wv_ref, bv_ref) * conv(u1_ref, w1_ref, b1_ref)


def _hyena_gate(p, first_blk, conv_w, conv_b, batch, tc=256):
    n = p.shape[0] // batch
    nct = HYENA_WIDTH // tc
    c0 = first_blk * LANES // tc
    u = lambda part: pl.BlockSpec((n, tc), lambda b, j: (b, c0 + part * nct + j))
    w = lambda part: pl.BlockSpec((HYENA_SHORT, tc), lambda b, j: (0, part * nct + j))
    bb = lambda part: pl.BlockSpec((1, tc), lambda b, j: (0, part * nct + j))
    out = jax.ShapeDtypeStruct((batch * n, HYENA_WIDTH), F32)
    cb = conv_b.reshape(1, -1)
    return pl.pallas_call(
        _hyena_gate_kernel,
        grid=(batch, nct),
        in_specs=[u(0), u(1), u(2), w(0), w(1), w(2), bb(0), bb(1), bb(2)],
        out_specs=[pl.BlockSpec((n, tc), lambda b, j: (b, j)), pl.BlockSpec((n, tc), lambda b, j: (b, j))],
        out_shape=[out, out],
        compiler_params=_cparams(("arbitrary", "arbitrary")),
    )(p, p, p, conv_w, conv_w, conv_w, cb, cb, cb)


def _dft_fwd_kernel(cm_ref, sf_ref, z_ref, a_ref, b_ref, pr_ref, ps_ref):
    z = z_ref[...].astype(BF16)
    zr = _dot(cm_ref[...], z)
    zs = _dot(sf_ref[...], z)
    a, b = a_ref[...], b_ref[...]
    row = lax.broadcasted_iota(jnp.int32, zr.shape, 0) + pl.program_id(1) * zr.shape[0]
    first = row == 0
    pr_ref[...] = (zr * a - jnp.where(first, 0.0, zs * b)).astype(pr_ref.dtype)
    ps_ref[...] = (jnp.where(first, 0.0, zr * b) + zs * jnp.where(first, b, a)).astype(ps_ref.dtype)


def _dft_inv_kernel(cm_ref, si_ref, pr_ref, ps_ref, z_ref, x0_ref, skip_ref, o_ref):
    y = _dot(cm_ref[...], pr_ref[...]) + _dot(si_ref[...], ps_ref[...])
    o_ref[...] = (x0_ref[...] * (y + z_ref[...] * skip_ref[...])).astype(o_ref.dtype)


def _long_conv(z, x0, spec_a, spec_b, skip, cm, sf, si, batch, tk=512, tc=512):
    n = z.shape[0] // batch
    tk = min(tk, n)
    nk = n // tk
    nct = HYENA_WIDTH // tc
    mat = pl.BlockSpec((tk, n), lambda b, i, j: (i, 0))
    col = pl.BlockSpec((n, tc), lambda b, i, j: (b, j))
    tile_nb = pl.BlockSpec((tk, tc), lambda b, i, j: (i, j))
    tile = pl.BlockSpec((tk, tc), lambda b, i, j: (b * nk + i, j))
    spec_shape = jax.ShapeDtypeStruct((batch * n, HYENA_WIDTH), BF16)
    pr, ps = pl.pallas_call(
        _dft_fwd_kernel,
        grid=(batch, nk, nct),
        in_specs=[mat, mat, col, tile_nb, tile_nb],
        out_specs=[tile, tile],
        out_shape=[spec_shape, spec_shape],
        compiler_params=_cparams(("arbitrary", "arbitrary", "arbitrary")),
    )(cm, sf, z, spec_a, spec_b)
    return pl.pallas_call(
        _dft_inv_kernel,
        grid=(batch, nk, nct),
        in_specs=[mat, mat, col, col, tile, tile, pl.BlockSpec((1, tc), lambda b, i, j: (0, j))],
        out_specs=tile,
        out_shape=jax.ShapeDtypeStruct((batch * n, HYENA_WIDTH), BF16),
        compiler_params=_cparams(("arbitrary", "arbitrary", "arbitrary")),
    )(cm, si, pr, ps, z, x0, skip.reshape(1, -1))


def _hyena(p, first_blk, conv_w, conv_b, filt_params, skip, batch):
    n = p.shape[0] // batch
    cm, sf, si = _dft_matrices(n)
    filt = _hyena_filters(n, *filt_params)
    spec_a, spec_b = _filter_spectrum(filt, cm, sf)
    x0, z = _hyena_gate(p, first_blk, conv_w, conv_b, batch)
    return _long_conv(z, x0, spec_a, spec_b, skip, cm, sf, si, batch)


def kernel(x, c, ctx, c_ctx, ada_w, ada_b, norm_mix_g, norm_mlp_g, w_out, mlp_w1, mlp_w2, final_norm_g, ev_w_in, mla_kv_norm_g, mla_w_ukv, na_rel_bias, od_w_in, hgrn_lb_logits, hgrn_norm_g, hy_conv_w, hy_conv_b, hy_filt_w1, hy_filt_b1, hy_filt_w2, hy_filt_b2, hy_filt_w3, hy_filt_b3, hy_filt_freq, hy_filt_wout, hy_skip):
    batch, seq, d = x.shape
    lc = ctx.shape[1]
    depth = ada_w.shape[0]
    h_lat = x.reshape(batch * seq, d)
    h_ctx = ctx.reshape(batch * lc, d)

    cond = jnp.concatenate([c, c_ctx[None, :], jnp.zeros((8 - batch - 1, d), F32)], axis=0)
    mod_all = _ada_modulation(cond, ada_w, ada_b)
    cos_tab, sin_tab = _rope_tables(seq)
    hgrn_cols = 5 * HGRN_WIDTH
    od_w_in, w_out, mlp_w1, mlp_w2, mla_w_ukv = (
        t.astype(BF16) for t in (od_w_in, w_out, mlp_w1, mlp_w2, mla_w_ukv))

    for l in range(depth):
        ctx_out = l < depth - 1
        mod3 = mod_all[l].reshape(8, 1, 6 * d)
        lat_mod = lambda *chunks: (mod3, *chunks, seq, 0)
        ctx_mod = lambda *chunks: (mod3, *chunks, batch * lc, batch)
        if l % 2 == 0:
            e = l // 2
            w_in = _even_w_in(ev_w_in[e])
            p_lat = _norm_proj(h_lat, 0, d, norm_mix_g[l], w_in, BF16, lat_mod(0, 1))
            p_ctx = _norm_proj(h_ctx, 0, d, norm_mix_g[l], w_in, BF16, ctx_mod(0, 1))
            ckv_blk = EV_CKV_BLK * LANES // MLA_KV_RANK
            kv_lat = _norm_proj(p_lat, ckv_blk, MLA_KV_RANK, mla_kv_norm_g[e], mla_w_ukv, BF16, layer=e)
            kv_ctx = _norm_proj(p_ctx, ckv_blk, MLA_KV_RANK, mla_kv_norm_g[e], mla_w_ukv, BF16, layer=e)
            qpe_rot = _rope(p_lat, EV_QMLA_BLK + 1, MLA_HEADS, cos_tab, sin_tab)
            kpe_rot = _rope(p_lat, EV_KPE_BLK, 1, cos_tab, sin_tab)
            y1_lat = _mla_latent(p_lat, p_ctx, kv_lat, kv_ctx, qpe_rot, kpe_rot, batch)
            y2_lat = _na_latent(p_lat, p_ctx, _na_bias_table(na_rel_bias[e]), batch)
            if ctx_out:
                y1_ctx, y2_ctx = _ctx_attention(p_ctx, kv_ctx, batch)
        else:
            o = l // 2
            p_lat = _norm_proj(h_lat, 0, d, norm_mix_g[l], od_w_in, F32, lat_mod(0, 1), layer=o)
            p_ctx = _norm_proj(h_ctx, 0, d, norm_mix_g[l], od_w_in, F32, ctx_mod(0, 1), layer=o,
                               n=None if ctx_out else hgrn_cols)
            y1_lat, y1_ctx = _hgrn2(p_lat, p_ctx, hgrn_lb_logits, hgrn_norm_g, o, batch)
            filt_params = (hy_filt_w1[o], hy_filt_b1[o], hy_filt_w2[o], hy_filt_b2[o], hy_filt_w3[o],
                           hy_filt_b3[o], hy_filt_freq[o], hy_filt_wout[o])
            y2_lat = _hyena(p_lat, hgrn_cols // LANES, hy_conv_w[o], hy_conv_b[o], filt_params,
                            hy_skip[o], batch)
            if ctx_out:
                y2_ctx = _hyena(p_ctx, hgrn_cols // LANES, hy_conv_w[o], hy_conv_b[o], filt_params,
                                hy_skip[o], batch)
        h_lat = _out_proj(y1_lat, y2_lat, w_out, l, h_lat, lat_mod(2))
        h_lat = _mlp(h_lat, norm_mlp_g[l], mlp_w1, mlp_w2, l, lat_mod(3, 4, 5),
                     final_g=None if ctx_out else final_norm_g)
        if ctx_out:
            h_ctx = _out_proj(y1_ctx, y2_ctx, w_out, l, h_ctx, ctx_mod(2))
            h_ctx = _mlp(h_ctx, norm_mlp_g[l], mlp_w1, mlp_w2, l, ctx_mod(3, 4, 5))
    return h_lat.reshape(batch, seq, d)
```

```python
import functools
import math

import numpy as np
import jax
import jax.numpy as jnp
from jax import lax
from jax.experimental import pallas as pl
from jax.experimental.pallas import tpu as pltpu

F32 = jnp.float32
BF16 = jnp.bfloat16

D_MODEL = 2048
DEPTH = 4
GRID_W = 64
HEAD_DIM = 128
MLA_HEADS = 8
MLA_NOPE_DIM = 128
MLA_ROPE_DIM = 64
MLA_QK_DIM = MLA_NOPE_DIM + MLA_ROPE_DIM
MLA_KV_RANK = 512
NA_HEADS = 8
NA_DIM = 128
NA_KH = 8
NA_KW = 16
ROPE_THETA = 10000.0
HGRN_WIDTH = 1024
HGRN_HEADS = 8
FORGET_FLOOR = 1e-30
HYENA_WIDTH = 1024
HYENA_SHORT = 3
HYENA_EMB = 33
HYENA_BANDS = (HYENA_EMB - 1) // 2
HYENA_FILT_HIDDEN = 64
HYENA_DECAY_TARGET = 1e-2
HYENA_FAST_PCT = 0.3
HYENA_SLOW_PCT = 1.5
MLP_HIDDEN = 4 * D_MODEL
NORM_EPS = 1e-6
NEG_INF = -1e30
LOG2E = 1.4426950408889634

LANES = 128
VMEM_LIMIT_BYTES = 56 * 1024 * 1024

EV_QMLA_BLK = 0
EV_QNA_BLK = 16
EV_KNA_BLK = 24
EV_VNA_BLK = 32
EV_CKV_BLK = 40
EV_KPE_BLK = 44
EV_WIDTH = 48 * LANES
HGRN_CHUNK = 16
HGRN_UNROLL = 8


def _cparams(sem):
    return pltpu.CompilerParams(dimension_semantics=sem, vmem_limit_bytes=VMEM_LIMIT_BYTES)


def _dot(a, b):
    return jnp.dot(a, b, preferred_element_type=F32)


def _dot_t(a, b):
    return lax.dot_general(a, b, (((1,), (1,)), ((), ())), preferred_element_type=F32)


def _dot_tn(a, b):
    return lax.dot_general(a, b, (((0,), (0,)), ((), ())), preferred_element_type=F32)


def _dot_f32(a, b):
    return jnp.dot(a, b, preferred_element_type=F32, precision=lax.Precision.HIGHEST)


def _sigmoid(x):
    return 1.0 / (1.0 + jnp.exp(-x))


def _silu(x):
    return x * _sigmoid(x)


def _rms(x):
    return x * lax.rsqrt(jnp.mean(x * x, axis=-1, keepdims=True) + NORM_EPS)


def _ada_kernel(s_ref, w_ref, b_ref, o_ref):
    s = _silu(s_ref[...]).astype(BF16)
    o_ref[...] = _dot(s, w_ref[...].astype(BF16)) + b_ref[...]


def _ada_modulation(cond, ada_w, ada_b, tn=1024):
    depth, d, n = ada_w.shape
    rows = cond.shape[0]
    return pl.pallas_call(
        _ada_kernel,
        grid=(depth, n // tn),
        in_specs=[
            pl.BlockSpec((rows, d), lambda l, j: (0, 0)),
            pl.BlockSpec((None, d, tn), lambda l, j: (l, 0, j)),
            pl.BlockSpec((None, 1, tn), lambda l, j: (l, 0, j)),
        ],
        out_specs=pl.BlockSpec((None, rows, tn), lambda l, j: (l, 0, j)),
        out_shape=jax.ShapeDtypeStruct((depth, rows, n), F32),
        compiler_params=_cparams(("arbitrary", "arbitrary")),
    )(cond, ada_w, ada_b.reshape(depth, 1, n))


def _mod_spec(chunk, tiles_per_group, group0, d):
    return pl.BlockSpec((None, 1, d), lambda i, *_: (group0 + i // tiles_per_group, 0, chunk))


def _proj_kernel(*refs, modulated):
    if modulated:
        x_ref, g_ref, sh_ref, sc_ref, w_ref, o_ref, a_ref = refs
    else:
        x_ref, g_ref, w_ref, o_ref, a_ref = refs

    @pl.when(pl.program_id(1) == 0)
    def _():
        y = _rms(x_ref[...].astype(F32)) * g_ref[...]
        if modulated:
            y = y * (1.0 + sc_ref[...]) + sh_ref[...]
        a_ref[...] = y.astype(BF16)

    o_ref[...] = _dot(a_ref[...], w_ref[...]).astype(o_ref.dtype)


def _layer_spec(w, layer, block, index_map):
    if w.ndim == 2:
        return pl.BlockSpec(block, index_map)
    return pl.BlockSpec((None,) + block, lambda *idx: (layer,) + index_map(*idx))


def _norm_proj(x, x_col_blk, k, g, w, out_dtype, mod=None, layer=None, n=None, tm=1024, tn=1024):
    m = x.shape[0]
    n = w.shape[-1] if n is None else n
    tm = min(tm, m)
    tn = min(tn, n)
    assert m % tm == 0 and n % tn == 0
    in_specs = [pl.BlockSpec((tm, k), lambda i, j: (i, x_col_blk)),
                pl.BlockSpec((1, k), lambda i, j: (0, 0))]
    args = [x, g.reshape(1, k)]
    if mod is not None:
        mod3, sh_chunk, sc_chunk, rows_per_group, group0 = mod
        assert rows_per_group % tm == 0
        in_specs += [_mod_spec(sh_chunk, rows_per_group // tm, group0, k),
                     _mod_spec(sc_chunk, rows_per_group // tm, group0, k)]
        args += [mod3, mod3]
    in_specs.append(_layer_spec(w, layer, (k, tn), lambda i, j: (0, j)))
    args.append(w)
    return pl.pallas_call(
        functools.partial(_proj_kernel, modulated=mod is not None),
        grid=(m // tm, n // tn),
        in_specs=in_specs,
        out_specs=pl.BlockSpec((tm, tn), lambda i, j: (i, j)),
        out_shape=jax.ShapeDtypeStruct((m, n), out_dtype),
        scratch_shapes=[pltpu.VMEM((tm, k), BF16)],
        compiler_params=_cparams(("arbitrary", "arbitrary")),
    )(*args)


def _outproj_kernel(y1_ref, y2_ref, w_ref, h_ref, gate_ref, o_ref):
    k1 = y1_ref.shape[1]
    acc = _dot(y1_ref[...], w_ref[:k1, :]) + _dot(y2_ref[...], w_ref[k1:, :])
    o_ref[...] = h_ref[...] + gate_ref[...] * acc


def _out_proj(y1, y2, w, layer, h, mod, tm=512):
    m, d = h.shape
    k1, k2 = y1.shape[1], y2.shape[1]
    tm = min(tm, m)
    mod3, gate_chunk, rows_per_group, group0 = mod
    return pl.pallas_call(
        _outproj_kernel,
        grid=(m // tm,),
        in_specs=[
            pl.BlockSpec((tm, k1), lambda i: (i, 0)),
            pl.BlockSpec((tm, k2), lambda i: (i, 0)),
            _layer_spec(w, layer, (k1 + k2, d), lambda i: (0, 0)),
            pl.BlockSpec((tm, d), lambda i: (i, 0)),
            _mod_spec(gate_chunk, rows_per_group // tm, group0, d),
        ],
        out_specs=pl.BlockSpec((tm, d), lambda i: (i, 0)),
        out_shape=jax.ShapeDtypeStruct((m, d), F32),
        compiler_params=_cparams(("arbitrary",)),
    )(y1, y2, w, h, mod3)


def _mlp_kernel(*refs, final_norm):
    if final_norm:
        h_ref, g_ref, sh_ref, sc_ref, gate_ref, w1_ref, w2_ref, fg_ref, o_ref, a_ref, acc_ref = refs
    else:
        h_ref, g_ref, sh_ref, sc_ref, gate_ref, w1_ref, w2_ref, o_ref, a_ref, acc_ref = refs
    k = pl.program_id(1)

    @pl.when(k == 0)
    def _():
        y = _rms(h_ref[...]) * g_ref[...]
        a_ref[...] = (y * (1.0 + sc_ref[...]) + sh_ref[...]).astype(BF16)
        acc_ref[...] = jnp.zeros_like(acc_ref)

    u = jnp.maximum(_dot(a_ref[...], w1_ref[...]), 0.0)
    acc_ref[...] += _dot((u * u).astype(BF16), w2_ref[...])

    @pl.when(k == pl.num_programs(1) - 1)
    def _():
        out = h_ref[...] + gate_ref[...] * acc_ref[...]
        if final_norm:
            out = _rms(out) * fg_ref[...]
        o_ref[...] = out


def _mlp(h, g, w1, w2, layer, mod, final_g=None, tm=512, th=1024):
    m, d = h.shape
    hid = w1.shape[-1]
    tm = min(tm, m)
    mod3, sh_chunk, sc_chunk, gate_chunk, rows_per_group, group0 = mod
    tpg = rows_per_group // tm
    in_specs = [
        pl.BlockSpec((tm, d), lambda i, k: (i, 0)),
        pl.BlockSpec((1, d), lambda i, k: (0, 0)),
        _mod_spec(sh_chunk, tpg, group0, d),
        _mod_spec(sc_chunk, tpg, group0, d),
        _mod_spec(gate_chunk, tpg, group0, d),
        _layer_spec(w1, layer, (d, th), lambda i, k: (0, k)),
        _layer_spec(w2, layer, (th, d), lambda i, k: (k, 0)),
    ]
    args = [h, g.reshape(1, d), mod3, mod3, mod3, w1, w2]
    if final_g is not None:
        in_specs.append(pl.BlockSpec((1, d), lambda i, k: (0, 0)))
        args.append(final_g.reshape(1, d))
    return pl.pallas_call(
        functools.partial(_mlp_kernel, final_norm=final_g is not None),
        grid=(m // tm, hid // th),
        in_specs=in_specs,
        out_specs=pl.BlockSpec((tm, d), lambda i, k: (i, 0)),
        out_shape=jax.ShapeDtypeStruct((m, d), F32),
        scratch_shapes=[pltpu.VMEM((tm, d), BF16), pltpu.VMEM((tm, d), F32)],
        compiler_params=_cparams(("arbitrary", "arbitrary")),
    )(*args)


def _softmax_pv(scores, values, scale=1.0):
    m = functools.reduce(jnp.maximum, [jnp.max(s, axis=-1, keepdims=True) for s in scores])
    ps = [jnp.exp2((s - m) * (scale * LOG2E)) for s in scores]
    denom = functools.reduce(jnp.add, [jnp.sum(p, axis=-1, keepdims=True) for p in ps])
    o = functools.reduce(jnp.add, [_dot(p.astype(BF16), v) for p, v in zip(ps, values)])
    return o / denom


def _rope_rotate(x, cos, sin):
    x = x.astype(F32)
    lane = lax.broadcasted_iota(jnp.int32, x.shape, 1)
    partner = jnp.where((lane % 32) < 16, pltpu.roll(x, LANES - 16, 1), pltpu.roll(x, 16, 1))
    return x * cos + partner * sin


def _mla_lat_kernel(qn_ref, qpe_ref, cosq_ref, sinq_ref, knc_ref, kpec_ref, vc_ref, knl_ref, kpel_ref,
                    cosk_ref, sink_ref, vl_ref, o_ref, k_scr, v_scr, *, scale):
    lc = knc_ref.shape[0]

    @pl.when(pl.program_id(2) == 0)
    def _():
        k_scr[:lc, :LANES] = knc_ref[...]
        k_scr[:lc, LANES:] = kpec_ref[...]
        k_scr[lc:, :LANES] = knl_ref[...]
        k_scr[lc:, LANES:] = _rope_rotate(kpel_ref[...], cosk_ref[...], sink_ref[...]).astype(BF16)
        v_scr[:lc, :] = vc_ref[...]
        v_scr[lc:, :] = vl_ref[...]

    qpe = _rope_rotate(qpe_ref[...], cosq_ref[...], sinq_ref[...]).astype(BF16)
    q = jnp.concatenate([qn_ref[...], qpe], axis=1)
    s = _dot_t(q, k_scr[...])
    o_ref[...] = _softmax_pv([s], [v_scr[...]], scale).astype(o_ref.dtype)


def _mla_latent(p_lat, p_ctx, kv_lat, kv_ctx, cos_tab, sin_tab, batch, tq=512):
    n = p_lat.shape[0] // batch
    lc = p_ctx.shape[0] // batch
    nq = n // tq
    h = MLA_HEADS
    blk = lambda rows, f: pl.BlockSpec((rows, LANES), f)
    return pl.pallas_call(
        functools.partial(_mla_lat_kernel, scale=MLA_QK_DIM ** -0.5),
        grid=(batch, h, nq),
        in_specs=[
            blk(tq, lambda b, hh, i: (b * nq + i, EV_QMLA_BLK + 2 * hh)),
            blk(tq, lambda b, hh, i: (b * nq + i, EV_QMLA_BLK + 2 * hh + 1)),
            blk(tq, lambda b, hh, i: (i, 0)),
            blk(tq, lambda b, hh, i: (i, 0)),
            blk(lc, lambda b, hh, i: (b, 2 * hh)),
            blk(lc, lambda b, hh, i: (b, EV_KPE_BLK)),
            blk(lc, lambda b, hh, i: (b, 2 * hh + 1)),
            blk(n, lambda b, hh, i: (b, 2 * hh)),
            blk(n, lambda b, hh, i: (b, EV_KPE_BLK)),
            blk(n, lambda b, hh, i: (0, 0)),
            blk(n, lambda b, hh, i: (0, 0)),
            blk(n, lambda b, hh, i: (b, 2 * hh + 1)),
        ],
        out_specs=blk(tq, lambda b, hh, i: (b * nq + i, hh)),
        out_shape=jax.ShapeDtypeStruct((batch * n, h * LANES), BF16),
        scratch_shapes=[pltpu.VMEM((lc + n, 2 * LANES), BF16), pltpu.VMEM((lc + n, LANES), BF16)],
        compiler_params=_cparams(("arbitrary", "arbitrary", "arbitrary")),
    )(p_lat, p_lat, cos_tab, sin_tab, kv_ctx, p_ctx, kv_ctx, kv_lat, p_lat, cos_tab, sin_tab, kv_lat)


def _ctx_attn_kernel(qm_ref, kn_ref, kpe_ref, vm_ref, qn_ref, kna_ref, vna_ref, om_ref, on_ref,
                     *, mla_scale, na_scale):
    k = jnp.concatenate([kn_ref[...], kpe_ref[...]], axis=1)
    s = _dot_t(qm_ref[...], k)
    om_ref[...] = _softmax_pv([s], [vm_ref[...]], mla_scale).astype(om_ref.dtype)
    s = _dot_t(qn_ref[...], kna_ref[...])
    on_ref[...] = _softmax_pv([s], [vna_ref[...]], na_scale).astype(on_ref.dtype)


def _ctx_attention(p_ctx, kv_ctx, batch):
    lc = p_ctx.shape[0] // batch
    h = MLA_HEADS
    blk = lambda f: pl.BlockSpec((lc, LANES), f)
    out = jax.ShapeDtypeStruct((batch * lc, h * LANES), BF16)
    return pl.pallas_call(
        functools.partial(_ctx_attn_kernel, mla_scale=MLA_QK_DIM ** -0.5, na_scale=NA_DIM ** -0.5),
        grid=(batch, h),
        in_specs=[
            pl.BlockSpec((lc, 2 * LANES), lambda b, hh: (b, hh)),
            blk(lambda b, hh: (b, 2 * hh)),
            blk(lambda b, hh: (b, EV_KPE_BLK)),
            blk(lambda b, hh: (b, 2 * hh + 1)),
            blk(lambda b, hh: (b, EV_QNA_BLK + hh)),
            blk(lambda b, hh: (b, EV_KNA_BLK + hh)),
            blk(lambda b, hh: (b, EV_VNA_BLK + hh)),
        ],
        out_specs=[blk(lambda b, hh: (b, hh)), blk(lambda b, hh: (b, hh))],
        out_shape=[out, out],
        compiler_params=_cparams(("arbitrary", "arbitrary")),
    )(p_ctx, kv_ctx, p_ctx, kv_ctx, p_ctx, p_ctx, p_ctx)


def _na_kernel(q_ref, k_ref, v_ref, kc_ref, vc_ref, bias_ref, o_ref, *, scale, n_rows):
    kc = kc_ref[...]
    vc = vc_ref[...]
    win = NA_KH * GRID_W
    for r in range(n_rows):
        ws = min(max(r - NA_KH // 2, 0), n_rows - NA_KH)
        qr = q_ref[r * GRID_W:(r + 1) * GRID_W, :]
        s = _dot_t(qr, k_ref[ws * GRID_W:ws * GRID_W + win, :]) * scale + bias_ref[r - ws]
        sc = _dot_t(qr, kc) * scale
        o = _softmax_pv([s, sc], [v_ref[ws * GRID_W:ws * GRID_W + win, :], vc])
        o_ref[r * GRID_W:(r + 1) * GRID_W, :] = o.astype(o_ref.dtype)


def _na_bias_kernel(rb_ref, onehot_ref, mask_ref, o_ref):
    o_ref[...] = _dot_f32(rb_ref[...], onehot_ref[...]) + mask_ref[...]


def _na_bias_table(rel_bias):
    n_heads, n_ro, n_co = rel_bias.shape
    col = np.arange(GRID_W)
    col_start = np.clip(col - NA_KW // 2, 0, GRID_W - NA_KW)
    col_mask = (col[None, :] >= col_start[:, None]) & (col[None, :] < col_start[:, None] + NA_KW)
    col_off = np.clip(col[None, :] - col[:, None], 1 - NA_KW, NA_KW - 1) + (NA_KW - 1)
    onehot = (col_off.reshape(1, -1) == np.arange(n_co)[:, None]).astype(np.float32)
    mask_add = np.where(col_mask.reshape(1, -1), 0.0, NEG_INF).astype(np.float32)
    qw = GRID_W * GRID_W
    full = lambda shape: pl.BlockSpec(shape, lambda: (0,) * len(shape))
    cols = pl.pallas_call(
        _na_bias_kernel,
        in_specs=[full((n_heads * n_ro, n_co)), full((n_co, qw)), full((1, qw))],
        out_specs=full((n_heads * n_ro, qw)),
        out_shape=jax.ShapeDtypeStruct((n_heads * n_ro, qw), F32),
    )(rel_bias.reshape(n_heads * n_ro, n_co), jnp.asarray(onehot), jnp.asarray(mask_add))
    cols = cols.reshape(n_heads, n_ro, GRID_W, GRID_W)
    t = jnp.stack([cols[:, NA_KH - 1 - e:2 * NA_KH - 1 - e] for e in range(NA_KH)], axis=1)
    return t.transpose(0, 1, 3, 2, 4).reshape(n_heads, NA_KH, GRID_W, NA_KH * GRID_W)


def _na_latent(p_lat, p_ctx, bias_tab, batch):
    n = p_lat.shape[0] // batch
    lc = p_ctx.shape[0] // batch
    h = NA_HEADS
    n_rows = n // GRID_W
    assert n_rows >= NA_KH
    blk = lambda rows, f: pl.BlockSpec((rows, LANES), f)
    return pl.pallas_call(
        functools.partial(_na_kernel, scale=NA_DIM ** -0.5, n_rows=n_rows),
        grid=(batch, h),
        in_specs=[
            blk(n, lambda b, hh: (b, EV_QNA_BLK + hh)),
            blk(n, lambda b, hh: (b, EV_KNA_BLK + hh)),
            blk(n, lambda b, hh: (b, EV_VNA_BLK + hh)),
            blk(lc, lambda b, hh: (b, EV_KNA_BLK + hh)),
            blk(lc, lambda b, hh: (b, EV_VNA_BLK + hh)),
            pl.BlockSpec((None, NA_KH, GRID_W, NA_KH * GRID_W), lambda b, hh: (hh, 0, 0, 0)),
        ],
        out_specs=blk(n, lambda b, hh: (b, hh)),
        out_shape=jax.ShapeDtypeStruct((batch * n, h * LANES), BF16),
        compiler_params=_cparams(("arbitrary", "arbitrary")),
    )(p_lat, p_lat, p_lat, p_ctx, p_ctx, bias_tab)


def _rope_tables(n):
    pos = np.arange(n)
    rows, cols = pos // GRID_W, pos % GRID_W
    half = MLA_ROPE_DIM // 2
    inv_freq = ROPE_THETA ** (-np.arange(0, half, 2, dtype=np.float64) / half)
    cos = np.zeros((n, LANES), np.float64)
    sin = np.zeros((n, LANES), np.float64)
    for base, p in ((0, rows), (half, cols)):
        ang = p[:, None].astype(np.float64) * inv_freq[None, :]
        q = half // 2
        cos[:, base:base + q] = np.cos(ang)
        cos[:, base + q:base + half] = np.cos(ang)
        sin[:, base:base + q] = -np.sin(ang)
        sin[:, base + q:base + half] = np.sin(ang)
    return jnp.asarray(cos, F32), jnp.asarray(sin, F32)


def _even_w_in(w):
    d = w.shape[0]
    z64 = jnp.zeros((d, LANES - MLA_ROPE_DIM), w.dtype)
    pieces = []
    for h in range(MLA_HEADS):
        pieces += [w[:, h * MLA_QK_DIM:h * MLA_QK_DIM + MLA_NOPE_DIM],
                   w[:, h * MLA_QK_DIM + MLA_NOPE_DIM:(h + 1) * MLA_QK_DIM], z64]
    q_end = MLA_HEADS * MLA_QK_DIM
    ckv_end = q_end + MLA_KV_RANK
    kpe_end = ckv_end + MLA_ROPE_DIM
    pieces += [w[:, kpe_end:], w[:, q_end:ckv_end], w[:, ckv_end:kpe_end], z64]
    out = jnp.concatenate(pieces, axis=1)
    pad = EV_WIDTH - out.shape[1]
    return jnp.concatenate([out, jnp.zeros((d, pad), w.dtype)], axis=1).astype(BF16)


def _hgrn_gates(q_ref, i_ref, z_refs, lbs, row0, qb_s, v_s, cum_s, cpk_s, qd_s, kd_s, dec_s):
    c = HGRN_CHUNK
    grp = LANES
    ri = lax.broadcasted_iota(jnp.int32, (grp, grp), 0)
    ci = lax.broadcasted_iota(jnp.int32, (grp, grp), 1)
    same = (ri // c) == (ci // c)
    blk = jnp.where(same, 1.0, 0.0).astype(F32)
    tris = (jnp.where(jnp.logical_and(same, ci <= ri), 1.0, 0.0).astype(F32),
            jnp.where(jnp.logical_and(same, ci >= ri), 1.0, 0.0).astype(F32))

    def body(g, carry):
        src = pl.ds(pl.multiple_of(g * grp, grp), grp)
        dst = pl.ds(pl.multiple_of(row0 + g * grp, grp), grp)
        q = _silu(q_ref[src, :].astype(F32))
        qb_s[dst, :] = q.astype(BF16)
        v_s[dst, :] = i_ref[src, :].astype(F32)
        for d in range(2):
            f = jnp.maximum(lbs[d] + (1.0 - lbs[d]) * _sigmoid(z_refs[d][src, :].astype(F32)), FORGET_FLOOR)
            lf = jnp.log(f)
            k = 1.0 - f
            cum = _dot_f32(tris[d], lf)
            tot = _dot_f32(blk, lf)
            cum_s[d, dst, :] = cum * LOG2E
            cpk_s[d, dst, :] = (cum - jnp.log(k)) * LOG2E
            qd_s[d, dst, :] = (q * jnp.exp(cum)).astype(BF16)
            kd_s[d, dst, :] = (k * jnp.exp(tot - cum)).astype(BF16)
            dec_s[d, dst, :] = jnp.exp(tot)
        return carry

    lax.fori_loop(0, q_ref.shape[0] // grp, body, 0)


def _hgrn_chunk(qb, cum2, cpk2, v, qd, st, ones_bf, reverse):
    c = HGRN_CHUNK
    hc = c // 2
    o = _dot_t(qd, st.astype(BF16))
    rows = lax.broadcasted_iota(jnp.int32, (hc, LANES), 0)
    halves = (cum2[:hc], cum2[hc:])
    zero = jnp.zeros((hc, LANES), F32)
    pieces = []
    for s in range(c):
        ref = cpk2[s:s + 1]
        hs, rs = divmod(s, hc)
        es = []
        for hh in range(2):
            if hh == hs:
                mask = (rows <= rs) if reverse else (rows >= rs)
                es.append(jnp.exp2(jnp.where(mask, halves[hh] - ref, NEG_INF)))
            elif (hh > hs) != reverse:
                es.append(jnp.exp2(halves[hh] - ref))
            else:
                es.append(zero)
        pieces.append(jnp.concatenate(es, axis=0).astype(BF16) * qb)
    lhs = jnp.concatenate([jnp.concatenate(pieces[:hc], axis=0), jnp.concatenate(pieces[hc:], axis=0)], axis=1)
    r = _dot(lhs, ones_bf)
    for s in range(c):
        hs, rs = divmod(s, hc)
        o = o + r[rs * c:(rs + 1) * c, hs * LANES:(hs + 1) * LANES] * v[s:s + 1]
    return o


def _hgrn_kernel(ql_ref, il_ref, zfl_ref, zbl_ref, gl_ref, qc_ref, ic_ref, zfc_ref, zbc_ref, gc_ref,
                 lbl_ref, ng_ref, yl_ref, yc_ref,
                 qb_s, v_s, cum_s, cpk_s, qd_s, kd_s, dec_s, o_s, *, layer):
    c = HGRN_CHUNK
    lc, n = qc_ref.shape[0], ql_ref.shape[0]
    lbs = []
    for d in range(2):
        lg = lbl_ref[d]
        ex = jnp.exp(lg - jnp.max(lg, axis=0, keepdims=True))
        p = ex / jnp.sum(ex, axis=0, keepdims=True)
        lbs.append(jnp.sum(p[:layer + 1], axis=0, keepdims=True) - p[0:1])

    scr = (qb_s, v_s, cum_s, cpk_s, qd_s, kd_s, dec_s)
    _hgrn_gates(qc_ref, ic_ref, (zfc_ref, zbc_ref), lbs, 0, *scr)
    _hgrn_gates(ql_ref, il_ref, (zfl_ref, zbl_ref), lbs, lc, *scr)

    ri = lax.broadcasted_iota(jnp.int32, (2 * LANES, 2 * LANES), 0)
    ci = lax.broadcasted_iota(jnp.int32, (2 * LANES, 2 * LANES), 1)
    ones_bf = jnp.where((ri // LANES) == (ci // LANES), 1.0, 0.0).astype(BF16)
    nc_ctx = lc // c
    n_chunks = (lc + n) // c

    def body(g, sts):
        slices = ([], [])
        for i in range(HGRN_UNROLL):
            j = g * HGRN_UNROLL + i
            r_fwd = j * c
            r_bwd = jnp.where(j < nc_ctx, lc - c - j * c, 2 * lc + n - c - j * c)
            for d, r in enumerate((r_fwd, r_bwd)):
                slices[d].append(pl.ds(pl.multiple_of(r, c), c))
        incs = [[_dot_tn(v_s[sl, :].astype(BF16), kd_s[d, sl, :]) for sl in slices[d]] for d in range(2)]
        sts = list(sts)
        for i in range(HGRN_UNROLL):
            for d in range(2):
                sl = slices[d][i]
                o_s[d, sl, :] = _hgrn_chunk(qb_s[sl, :], cum_s[d, sl, :], cpk_s[d, sl, :], v_s[sl, :],
                                            qd_s[d, sl, :], sts[d], ones_bf, reverse=d == 1)
                sts[d] = sts[d] * dec_s[d, sl, :][0:1] + incs[d][i]
        return tuple(sts)

    zero = jnp.zeros((LANES, LANES), F32)
    assert n_chunks % HGRN_UNROLL == 0 and nc_ctx % HGRN_UNROLL == 0
    lax.fori_loop(0, n_chunks // HGRN_UNROLL, body, (zero, zero))

    ng = ng_ref[...]
    yc_ref[...] = (_rms(o_s[0, :lc, :] + o_s[1, :lc, :]) * ng
                   * _silu(gc_ref[...].astype(F32))).astype(yc_ref.dtype)
    yl_ref[...] = (_rms(o_s[0, lc:, :] + o_s[1, lc:, :]) * ng
                   * _silu(gl_ref[...].astype(F32))).astype(yl_ref.dtype)


def _hgrn2(p_lat, p_ctx, lb_logits, norm_g, layer, batch):
    n = p_lat.shape[0] // batch
    lc = p_ctx.shape[0] // batch
    h = HGRN_HEADS
    n_layers = lb_logits.shape[1]
    rows = lc + n
    assert lc % LANES == 0 and n % LANES == 0
    lat = lambda part: pl.BlockSpec((n, LANES), lambda b, hh: (b, part * h + hh))
    ctx = lambda part: pl.BlockSpec((lc, LANES), lambda b, hh: (b, part * h + hh))
    return pl.pallas_call(
        functools.partial(_hgrn_kernel, layer=layer),
        grid=(batch, h),
        in_specs=[lat(0), lat(1), lat(2), lat(3), lat(4), ctx(0), ctx(1), ctx(2), ctx(3), ctx(4),
                  pl.BlockSpec((2, n_layers, LANES), lambda b, hh: (0, 0, hh)),
                  pl.BlockSpec((1, LANES), lambda b, hh: (0, hh))],
        out_specs=[pl.BlockSpec((n, LANES), lambda b, hh: (b, hh)),
                   pl.BlockSpec((lc, LANES), lambda b, hh: (b, hh))],
        out_shape=[jax.ShapeDtypeStruct((batch * n, h * LANES), BF16),
                   jax.ShapeDtypeStruct((batch * lc, h * LANES), BF16)],
        scratch_shapes=[pltpu.VMEM((rows, LANES), BF16), pltpu.VMEM((rows, LANES), F32),
                        pltpu.VMEM((2, rows, LANES), F32), pltpu.VMEM((2, rows, LANES), F32),
                        pltpu.VMEM((2, rows, LANES), BF16), pltpu.VMEM((2, rows, LANES), BF16),
                        pltpu.VMEM((2, rows, LANES), F32), pltpu.VMEM((2, rows, LANES), F32)],
        compiler_params=_cparams(("arbitrary", "arbitrary")),
    )(p_lat, p_lat, p_lat, p_lat, p_lat, p_ctx, p_ctx, p_ctx, p_ctx, p_ctx,
      lb_logits, norm_g[layer].reshape(1, -1))


def _dft_matrices(n):
    idx = (np.arange(n)[:, None] * np.arange(n)[None, :]) % (2 * n)
    ang = idx.astype(np.float64) * (math.pi / n)
    cm = np.cos(ang)
    sf = np.sin(ang)
    sf[0, :] = (-1.0) ** np.arange(n)
    return (jnp.asarray(cm, F32).astype(BF16), jnp.asarray(sf, F32).astype(BF16),
            jnp.asarray(sf.T, F32).astype(BF16))


def _filter_features(n):
    pos = np.arange(n, dtype=np.float64)
    t = pos / max(n - 1, 1)
    bands = np.linspace(1e-4, HYENA_BANDS - 1, HYENA_BANDS)
    ang = (2.0 * math.pi / n) * pos[:, None] * bands[None, :]
    z = np.concatenate([t[:, None], np.cos(ang), -np.sin(ang)], -1)
    max_decay = math.log(HYENA_DECAY_TARGET) / HYENA_FAST_PCT
    min_decay = math.log(HYENA_DECAY_TARGET) / HYENA_SLOW_PCT
    deltas = np.abs(np.linspace(min_decay, max_decay, HYENA_WIDTH))
    return jnp.asarray(z, F32), jnp.asarray(t[:, None], F32), jnp.asarray(deltas[None, :], F32)


def _filter_kernel(z_ref, t_ref, dl_ref, w1_ref, b1_ref, w2_ref, b2_ref, w3_ref, b3_ref, fr_ref, wo_ref,
                   o_ref, hdn_ref):
    j = pl.program_id(0)

    @pl.when(j == 0)
    def _():
        fr = fr_ref[...]
        hdn = jnp.sin(fr * (_dot_f32(z_ref[...], w1_ref[...]) + b1_ref[...]))
        hdn = jnp.sin(fr * (_dot_f32(hdn, w2_ref[...]) + b2_ref[...]))
        hdn_ref[...] = jnp.sin(fr * (_dot_f32(hdn, w3_ref[...]) + b3_ref[...]))

    filt = _dot_f32(hdn_ref[...], wo_ref[...]) * jnp.exp(-t_ref[...] * dl_ref[...])
    row = lax.broadcasted_iota(jnp.int32, filt.shape, 0)
    is_bwd = j >= pl.num_programs(0) // 2
    o_ref[...] = jnp.where(jnp.logical_and(is_bwd, row == 0), 0.0, filt).astype(o_ref.dtype)


def _hyena_filters(n, w1, b1, w2, b2, w3, b3, freq, w_out, tc=512):
    z, t, deltas = _filter_features(n)
    hid = HYENA_FILT_HIDDEN
    nct = HYENA_WIDTH // tc
    full = lambda shape: pl.BlockSpec(shape, lambda j: (0,) * len(shape))
    return pl.pallas_call(
        _filter_kernel,
        grid=(2 * nct,),
        in_specs=[full((n, HYENA_EMB)), full((n, 1)),
                  pl.BlockSpec((1, tc), lambda j: (0, j % nct)),
                  full((HYENA_EMB, hid)), full((1, hid)), full((hid, hid)), full((1, hid)),
                  full((hid, hid)), full((1, hid)), full((1, hid)),
                  pl.BlockSpec((hid, tc), lambda j: (0, j))],
        out_specs=pl.BlockSpec((n, tc), lambda j: (0, j)),
        out_shape=jax.ShapeDtypeStruct((n, 2 * HYENA_WIDTH), BF16),
        scratch_shapes=[pltpu.VMEM((n, hid), F32)],
        compiler_params=_cparams(("arbitrary",)),
    )(z, t, deltas, w1, b1.reshape(1, hid), w2, b2.reshape(1, hid), w3, b3.reshape(1, hid),
      freq.reshape(1, hid), w_out)


def _spectrum_kernel(cm_ref, sf_ref, hf_ref, hb_ref, a_ref, b_ref, *, inv_len):
    cm, sf, hf, hb = cm_ref[...], sf_ref[...], hf_ref[...], hb_ref[...]
    kr = _dot(cm, hf) + _dot(cm, hb)
    d1 = _dot(sf, hf)
    d2 = _dot(sf, hb)
    row = lax.broadcasted_iota(jnp.int32, kr.shape, 0) + pl.program_id(0) * kr.shape[0]
    first = row == 0
    w = jnp.where(first, inv_len, 2.0 * inv_len)
    a_ref[...] = kr * w
    b_ref[...] = jnp.where(first, d1 + d2, d1 - d2) * w


def _filter_spectrum(filt, cm, sf, tk=512, tc=512):
    n = filt.shape[0]
    tk = min(tk, n)
    nct = HYENA_WIDTH // tc
    out = jax.ShapeDtypeStruct((n, HYENA_WIDTH), F32)
    return pl.pallas_call(
        functools.partial(_spectrum_kernel, inv_len=1.0 / (2 * n)),
        grid=(n // tk, nct),
        in_specs=[pl.BlockSpec((tk, n), lambda i, j: (i, 0)),
                  pl.BlockSpec((tk, n), lambda i, j: (i, 0)),
                  pl.BlockSpec((n, tc), lambda i, j: (0, j)),
                  pl.BlockSpec((n, tc), lambda i, j: (0, nct + j))],
        out_specs=[pl.BlockSpec((tk, tc), lambda i, j: (i, j)), pl.BlockSpec((tk, tc), lambda i, j: (i, j))],
        out_shape=[out, out],
        compiler_params=_cparams(("arbitrary", "arbitrary")),
    )(cm, sf, filt, filt)


def _hyena_gate_kernel(u0_ref, u1_ref, uv_ref, w0_ref, w1_ref, wv_ref, b0_ref, b1_ref, bv_ref,
                       x0_ref, z_ref):
    n = u0_ref.shape[0]
    row = lax.broadcasted_iota(jnp.int32, u0_ref.shape, 0)

    def conv(u_ref, w_ref, b_ref):
        u = u_ref[...].astype(F32)
        prev = jnp.where(row == 0, 0.0, pltpu.roll(u, 1, 0))
        nxt = jnp.where(row == n - 1, 0.0, pltpu.roll(u, n - 1, 0))
        return b_ref[...] + prev * w_ref[0:1] + u * w_ref[1:2] + nxt * w_ref[2:3]

    x0_ref[...] = conv(u0_ref, w0_ref, b0_ref)
    z_ref[...] = conv(uv_ref, wv_ref, bv_ref) * conv(u1_ref, w1_ref, b1_ref)


def _hyena_gate(p, first_blk, conv_w, conv_b, batch, tc=256):
    n = p.shape[0] // batch
    nct = HYENA_WIDTH // tc
    c0 = first_blk * LANES // tc
    u = lambda part: pl.BlockSpec((n, tc), lambda b, j: (b, c0 + part * nct + j))
    w = lambda part: pl.BlockSpec((HYENA_SHORT, tc), lambda b, j: (0, part * nct + j))
    bb = lambda part: pl.BlockSpec((1, tc), lambda b, j: (0, part * nct + j))
    out = jax.ShapeDtypeStruct((batch * n, HYENA_WIDTH), F32)
    cb = conv_b.reshape(1, -1)
    return pl.pallas_call(
        _hyena_gate_kernel,
        grid=(batch, nct),
        in_specs=[u(0), u(1), u(2), w(0), w(1), w(2), bb(0), bb(1), bb(2)],
        out_specs=[pl.BlockSpec((n, tc), lambda b, j: (b, j)), pl.BlockSpec((n, tc), lambda b, j: (b, j))],
        out_shape=[out, out],
        compiler_params=_cparams(("arbitrary", "arbitrary")),
    )(p, p, p, conv_w, conv_w, conv_w, cb, cb, cb)


def _dft_fwd_kernel(cm_ref, sf_ref, z_ref, a_ref, b_ref, pr_ref, ps_ref):
    z = z_ref[...].astype(BF16)
    zr = _dot(cm_ref[...], z)
    zs = _dot(sf_ref[...], z)
    a, b = a_ref[...], b_ref[...]
    row = lax.broadcasted_iota(jnp.int32, zr.shape, 0) + pl.program_id(1) * zr.shape[0]
    first = row == 0
    pr_ref[...] = (zr * a - jnp.where(first, 0.0, zs * b)).astype(pr_ref.dtype)
    ps_ref[...] = (jnp.where(first, 0.0, zr * b) + zs * jnp.where(first, b, a)).astype(ps_ref.dtype)


def _dft_inv_kernel(cm_ref, si_ref, pr_ref, ps_ref, z_ref, x0_ref, skip_ref, o_ref):
    y = _dot(cm_ref[...], pr_ref[...]) + _dot(si_ref[...], ps_ref[...])
    o_ref[...] = (x0_ref[...] * (y + z_ref[...] * skip_ref[...])).astype(o_ref.dtype)


def _long_conv(z, x0, spec_a, spec_b, skip, cm, sf, si, batch, tk=512, tc=512):
    n = z.shape[0] // batch
    tk = min(tk, n)
    nk = n // tk
    nct = HYENA_WIDTH // tc
    mat = pl.BlockSpec((tk, n), lambda b, i, j: (i, 0))
    col = pl.BlockSpec((n, tc), lambda b, i, j: (b, j))
    tile_nb = pl.BlockSpec((tk, tc), lambda b, i, j: (i, j))
    tile = pl.BlockSpec((tk, tc), lambda b, i, j: (b * nk + i, j))
    spec_shape = jax.ShapeDtypeStruct((batch * n, HYENA_WIDTH), BF16)
    pr, ps = pl.pallas_call(
        _dft_fwd_kernel,
        grid=(batch, nk, nct),
        in_specs=[mat, mat, col, tile_nb, tile_nb],
        out_specs=[tile, tile],
        out_shape=[spec_shape, spec_shape],
        compiler_params=_cparams(("arbitrary", "arbitrary", "arbitrary")),
    )(cm, sf, z, spec_a, spec_b)
    return pl.pallas_call(
        _dft_inv_kernel,
        grid=(batch, nk, nct),
        in_specs=[mat, mat, col, col, tile, tile, pl.BlockSpec((1, tc), lambda b, i, j: (0, j))],
        out_specs=tile,
        out_shape=jax.ShapeDtypeStruct((batch * n, HYENA_WIDTH), BF16),
        compiler_params=_cparams(("arbitrary", "arbitrary", "arbitrary")),
    )(cm, si, pr, ps, z, x0, skip.reshape(1, -1))


def _hyena(p, first_blk, conv_w, conv_b, filt_params, skip, batch):
    n = p.shape[0] // batch
    cm, sf, si = _dft_matrices(n)
    filt = _hyena_filters(n, *filt_params)
    spec_a, spec_b = _filter_spectrum(filt, cm, sf)
    x0, z = _hyena_gate(p, first_blk, conv_w, conv_b, batch)
    return _long_conv(z, x0, spec_a, spec_b, skip, cm, sf, si, batch)


def kernel(x, c, ctx, c_ctx, ada_w, ada_b, norm_mix_g, norm_mlp_g, w_out, mlp_w1, mlp_w2, final_norm_g, ev_w_in, mla_kv_norm_g, mla_w_ukv, na_rel_bias, od_w_in, hgrn_lb_logits, hgrn_norm_g, hy_conv_w, hy_conv_b, hy_filt_w1, hy_filt_b1, hy_filt_w2, hy_filt_b2, hy_filt_w3, hy_filt_b3, hy_filt_freq, hy_filt_wout, hy_skip):
    batch, seq, d = x.shape
    lc = ctx.shape[1]
    depth = ada_w.shape[0]
    h_lat = x.reshape(batch * seq, d)
    h_ctx = ctx.reshape(batch * lc, d)

    cond = jnp.concatenate([c, c_ctx[None, :], jnp.zeros((8 - batch - 1, d), F32)], axis=0)
    mod_all = _ada_modulation(cond, ada_w, ada_b)
    cos_tab, sin_tab = _rope_tables(seq)
    hgrn_cols = 5 * HGRN_WIDTH
    od_w_in, w_out, mlp_w1, mlp_w2, mla_w_ukv = (
        t.astype(BF16) for t in (od_w_in, w_out, mlp_w1, mlp_w2, mla_w_ukv))

    for l in range(depth):
        ctx_out = l < depth - 1
        mod3 = mod_all[l].reshape(8, 1, 6 * d)
        lat_mod = lambda *chunks: (mod3, *chunks, seq, 0)
        ctx_mod = lambda *chunks: (mod3, *chunks, batch * lc, batch)
        if l % 2 == 0:
            e = l // 2
            w_in = _even_w_in(ev_w_in[e])
            p_lat = _norm_proj(h_lat, 0, d, norm_mix_g[l], w_in, BF16, lat_mod(0, 1))
            p_ctx = _norm_proj(h_ctx, 0, d, norm_mix_g[l], w_in, BF16, ctx_mod(0, 1))
            ckv_blk = EV_CKV_BLK * LANES // MLA_KV_RANK
            kv_lat = _norm_proj(p_lat, ckv_blk, MLA_KV_RANK, mla_kv_norm_g[e], mla_w_ukv, BF16, layer=e)
            kv_ctx = _norm_proj(p_ctx, ckv_blk, MLA_KV_RANK, mla_kv_norm_g[e], mla_w_ukv, BF16, layer=e)
            y1_lat = _mla_latent(p_lat, p_ctx, kv_lat, kv_ctx, cos_tab, sin_tab, batch)
            y2_lat = _na_latent(p_lat, p_ctx, _na_bias_table(na_rel_bias[e]), batch)
            if ctx_out:
                y1_ctx, y2_ctx = _ctx_attention(p_ctx, kv_ctx, batch)
        else:
            o = l // 2
            p_lat = _norm_proj(h_lat, 0, d, norm_mix_g[l], od_w_in, BF16, lat_mod(0, 1), layer=o)
            p_ctx = _norm_proj(h_ctx, 0, d, norm_mix_g[l], od_w_in, BF16, ctx_mod(0, 1), layer=o,
                               n=None if ctx_out else hgrn_cols)
            y1_lat, y1_ctx = _hgrn2(p_lat, p_ctx, hgrn_lb_logits, hgrn_norm_g, o, batch)
            filt_params = (hy_filt_w1[o], hy_filt_b1[o], hy_filt_w2[o], hy_filt_b2[o], hy_filt_w3[o],
                           hy_filt_b3[o], hy_filt_freq[o], hy_filt_wout[o])
            y2_lat = _hyena(p_lat, hgrn_cols // LANES, hy_conv_w[o], hy_conv_b[o], filt_params,
                            hy_skip[o], batch)
            if ctx_out:
                y2_ctx = _hyena(p_ctx, hgrn_cols // LANES, hy_conv_w[o], hy_conv_b[o], filt_params,
                                hy_skip[o], batch)
        h_lat = _out_proj(y1_lat, y2_lat, w_out, l, h_lat, lat_mod(2))
        h_lat = _mlp(h_lat, norm_mlp_g[l], mlp_w1, mlp_w2, l, lat_mod(3, 4, 5),
                     final_g=None if ctx_out else final_norm_g)
        if ctx_out:
            h_ctx = _out_proj(y1_ctx, y2_ctx, w_out, l, h_ctx, ctx_mod(2))
            h_ctx = _mlp(h_ctx, norm_mlp_g[l], mlp_w1, mlp_w2, l, ctx_mod(3, 4, 5))
    return h_lat.reshape(batch, seq, d)
```

```python
import functools
import math

import numpy as np
import jax
import jax.numpy as jnp
from jax import lax
from jax.experimental import pallas as pl
from jax.experimental.pallas import tpu as pltpu

F32 = jnp.float32
BF16 = jnp.bfloat16

D_MODEL = 2048
DEPTH = 4
GRID_W = 64
HEAD_DIM = 128
MLA_HEADS = 8
MLA_NOPE_DIM = 128
MLA_ROPE_DIM = 64
MLA_QK_DIM = MLA_NOPE_DIM + MLA_ROPE_DIM
MLA_KV_RANK = 512
NA_HEADS = 8
NA_DIM = 128
NA_KH = 8
NA_KW = 16
ROPE_THETA = 10000.0
HGRN_WIDTH = 1024
HGRN_HEADS = 8
FORGET_FLOOR = 1e-30
HYENA_WIDTH = 1024
HYENA_SHORT = 3
HYENA_EMB = 33
HYENA_BANDS = (HYENA_EMB - 1) // 2
HYENA_FILT_HIDDEN = 64
HYENA_DECAY_TARGET = 1e-2
HYENA_FAST_PCT = 0.3
HYENA_SLOW_PCT = 1.5
MLP_HIDDEN = 4 * D_MODEL
NORM_EPS = 1e-6
NEG_INF = -1e30
LOG2E = 1.4426950408889634

LANES = 128
VMEM_LIMIT_BYTES = 56 * 1024 * 1024

EV_QMLA_BLK = 0
EV_QNA_BLK = 16
EV_KNA_BLK = 24
EV_VNA_BLK = 32
EV_CKV_BLK = 40
EV_KPE_BLK = 44
EV_WIDTH = 48 * LANES
HGRN_CHUNK = 16
HGRN_UNROLL = 8


def _cparams(sem):
    return pltpu.CompilerParams(dimension_semantics=sem, vmem_limit_bytes=VMEM_LIMIT_BYTES)


def _dot(a, b):
    return jnp.dot(a, b, preferred_element_type=F32)


def _dot_t(a, b):
    return lax.dot_general(a, b, (((1,), (1,)), ((), ())), preferred_element_type=F32)


def _dot_tn(a, b):
    return lax.dot_general(a, b, (((0,), (0,)), ((), ())), preferred_element_type=F32)


def _dot_f32(a, b):
    return jnp.dot(a, b, preferred_element_type=F32, precision=lax.Precision.HIGHEST)


def _dot_01(m01, x):
    hi = x.astype(BF16)
    lo = (x - hi.astype(F32)).astype(BF16)
    return _dot(m01, hi) + _dot(m01, lo)


def _sigmoid(x):
    return 1.0 / (1.0 + jnp.exp(-x))


def _silu(x):
    return x * _sigmoid(x)


def _rms(x):
    return x * lax.rsqrt(jnp.mean(x * x, axis=-1, keepdims=True) + NORM_EPS)


def _ada_kernel(s_ref, w_ref, b_ref, o_ref):
    s = _silu(s_ref[...]).astype(BF16)
    o_ref[...] = _dot(s, w_ref[...].astype(BF16)) + b_ref[...]


def _ada_modulation(cond, ada_w, ada_b, tn=1024):
    depth, d, n = ada_w.shape
    rows = cond.shape[0]
    return pl.pallas_call(
        _ada_kernel,
        grid=(depth, n // tn),
        in_specs=[
            pl.BlockSpec((rows, d), lambda l, j: (0, 0)),
            pl.BlockSpec((None, d, tn), lambda l, j: (l, 0, j)),
            pl.BlockSpec((None, 1, tn), lambda l, j: (l, 0, j)),
        ],
        out_specs=pl.BlockSpec((None, rows, tn), lambda l, j: (l, 0, j)),
        out_shape=jax.ShapeDtypeStruct((depth, rows, n), F32),
        compiler_params=_cparams(("arbitrary", "arbitrary")),
    )(cond, ada_w, ada_b.reshape(depth, 1, n))


def _mod_spec(chunk, tiles_per_group, group0, d):
    return pl.BlockSpec((None, 1, d), lambda i, *_: (group0 + i // tiles_per_group, 0, chunk))


def _proj_kernel(*refs, modulated):
    if modulated:
        x_ref, g_ref, sh_ref, sc_ref, w_ref, o_ref, a_ref = refs
    else:
        x_ref, g_ref, w_ref, o_ref, a_ref = refs

    @pl.when(pl.program_id(1) == 0)
    def _():
        y = _rms(x_ref[...].astype(F32)) * g_ref[...]
        if modulated:
            y = y * (1.0 + sc_ref[...]) + sh_ref[...]
        a_ref[...] = y.astype(BF16)

    o_ref[...] = _dot(a_ref[...], w_ref[...]).astype(o_ref.dtype)


def _layer_spec(w, layer, block, index_map):
    if w.ndim == 2:
        return pl.BlockSpec(block, index_map)
    return pl.BlockSpec((None,) + block, lambda *idx: (layer,) + index_map(*idx))


def _norm_proj(x, x_col_blk, k, g, w, out_dtype, mod=None, layer=None, n=None, tm=1024, tn=1024):
    m = x.shape[0]
    n = w.shape[-1] if n is None else n
    tm = min(tm, m)
    tn = min(tn, n)
    assert m % tm == 0 and n % tn == 0
    in_specs = [pl.BlockSpec((tm, k), lambda i, j: (i, x_col_blk)),
                pl.BlockSpec((1, k), lambda i, j: (0, 0))]
    args = [x, g.reshape(1, k)]
    if mod is not None:
        mod3, sh_chunk, sc_chunk, rows_per_group, group0 = mod
        assert rows_per_group % tm == 0
        in_specs += [_mod_spec(sh_chunk, rows_per_group // tm, group0, k),
                     _mod_spec(sc_chunk, rows_per_group // tm, group0, k)]
        args += [mod3, mod3]
    in_specs.append(_layer_spec(w, layer, (k, tn), lambda i, j: (0, j)))
    args.append(w)
    return pl.pallas_call(
        functools.partial(_proj_kernel, modulated=mod is not None),
        grid=(m // tm, n // tn),
        in_specs=in_specs,
        out_specs=pl.BlockSpec((tm, tn), lambda i, j: (i, j)),
        out_shape=jax.ShapeDtypeStruct((m, n), out_dtype),
        scratch_shapes=[pltpu.VMEM((tm, k), BF16)],
        compiler_params=_cparams(("arbitrary", "arbitrary")),
    )(*args)


def _outproj_kernel(y1_ref, y2_ref, w_ref, h_ref, gate_ref, o_ref):
    k1 = y1_ref.shape[1]
    acc = _dot(y1_ref[...], w_ref[:k1, :]) + _dot(y2_ref[...], w_ref[k1:, :])
    o_ref[...] = h_ref[...] + gate_ref[...] * acc


def _out_proj(y1, y2, w, layer, h, mod, tm=512):
    m, d = h.shape
    k1, k2 = y1.shape[1], y2.shape[1]
    tm = min(tm, m)
    mod3, gate_chunk, rows_per_group, group0 = mod
    return pl.pallas_call(
        _outproj_kernel,
        grid=(m // tm,),
        in_specs=[
            pl.BlockSpec((tm, k1), lambda i: (i, 0)),
            pl.BlockSpec((tm, k2), lambda i: (i, 0)),
            _layer_spec(w, layer, (k1 + k2, d), lambda i: (0, 0)),
            pl.BlockSpec((tm, d), lambda i: (i, 0)),
            _mod_spec(gate_chunk, rows_per_group // tm, group0, d),
        ],
        out_specs=pl.BlockSpec((tm, d), lambda i: (i, 0)),
        out_shape=jax.ShapeDtypeStruct((m, d), F32),
        compiler_params=_cparams(("arbitrary",)),
    )(y1, y2, w, h, mod3)


def _mlp_kernel(*refs, final_norm):
    if final_norm:
        h_ref, g_ref, sh_ref, sc_ref, gate_ref, w1_ref, w2_ref, fg_ref, o_ref, a_ref, acc_ref = refs
    else:
        h_ref, g_ref, sh_ref, sc_ref, gate_ref, w1_ref, w2_ref, o_ref, a_ref, acc_ref = refs
    k = pl.program_id(1)

    @pl.when(k == 0)
    def _():
        y = _rms(h_ref[...]) * g_ref[...]
        a_ref[...] = (y * (1.0 + sc_ref[...]) + sh_ref[...]).astype(BF16)
        acc_ref[...] = jnp.zeros_like(acc_ref)

    u = jnp.maximum(_dot(a_ref[...], w1_ref[...]), 0.0)
    acc_ref[...] += _dot((u * u).astype(BF16), w2_ref[...])

    @pl.when(k == pl.num_programs(1) - 1)
    def _():
        out = h_ref[...] + gate_ref[...] * acc_ref[...]
        if final_norm:
            out = _rms(out) * fg_ref[...]
        o_ref[...] = out


def _mlp(h, g, w1, w2, layer, mod, final_g=None, tm=512, th=1024):
    m, d = h.shape
    hid = w1.shape[-1]
    tm = min(tm, m)
    mod3, sh_chunk, sc_chunk, gate_chunk, rows_per_group, group0 = mod
    tpg = rows_per_group // tm
    in_specs = [
        pl.BlockSpec((tm, d), lambda i, k: (i, 0)),
        pl.BlockSpec((1, d), lambda i, k: (0, 0)),
        _mod_spec(sh_chunk, tpg, group0, d),
        _mod_spec(sc_chunk, tpg, group0, d),
        _mod_spec(gate_chunk, tpg, group0, d),
        _layer_spec(w1, layer, (d, th), lambda i, k: (0, k)),
        _layer_spec(w2, layer, (th, d), lambda i, k: (k, 0)),
    ]
    args = [h, g.reshape(1, d), mod3, mod3, mod3, w1, w2]
    if final_g is not None:
        in_specs.append(pl.BlockSpec((1, d), lambda i, k: (0, 0)))
        args.append(final_g.reshape(1, d))
    return pl.pallas_call(
        functools.partial(_mlp_kernel, final_norm=final_g is not None),
        grid=(m // tm, hid // th),
        in_specs=in_specs,
        out_specs=pl.BlockSpec((tm, d), lambda i, k: (i, 0)),
        out_shape=jax.ShapeDtypeStruct((m, d), F32),
        scratch_shapes=[pltpu.VMEM((tm, d), BF16), pltpu.VMEM((tm, d), F32)],
        compiler_params=_cparams(("arbitrary", "arbitrary")),
    )(*args)


def _softmax_pv(scores, values, scale=1.0):
    m = functools.reduce(jnp.maximum, [jnp.max(s, axis=-1, keepdims=True) for s in scores])
    ps = [jnp.exp2((s - m) * (scale * LOG2E)) for s in scores]
    denom = functools.reduce(jnp.add, [jnp.sum(p, axis=-1, keepdims=True) for p in ps])
    o = functools.reduce(jnp.add, [_dot(p.astype(BF16), v) for p, v in zip(ps, values)])
    return o / denom


def _rope_rotate(x, cos, sin):
    x = x.astype(F32)
    lane = lax.broadcasted_iota(jnp.int32, x.shape, 1)
    partner = jnp.where((lane % 32) < 16, pltpu.roll(x, LANES - 16, 1), pltpu.roll(x, 16, 1))
    return x * cos + partner * sin


def _mla_lat_kernel(qn_ref, qpe_ref, cosq_ref, sinq_ref, knc_ref, kpec_ref, vc_ref, knl_ref, kpel_ref,
                    cosk_ref, sink_ref, vl_ref, o_ref, k_scr, v_scr, *, scale):
    lc = knc_ref.shape[0]

    @pl.when(pl.program_id(2) == 0)
    def _():
        k_scr[:lc, :LANES] = knc_ref[...]
        k_scr[:lc, LANES:] = kpec_ref[...]
        k_scr[lc:, :LANES] = knl_ref[...]
        k_scr[lc:, LANES:] = _rope_rotate(kpel_ref[...], cosk_ref[...], sink_ref[...]).astype(BF16)
        v_scr[:lc, :LANES] = vc_ref[...]
        v_scr[lc:, :LANES] = vl_ref[...]
        v_scr[:, LANES:] = jnp.ones((v_scr.shape[0], LANES), BF16)

    qpe = _rope_rotate(qpe_ref[...], cosq_ref[...], sinq_ref[...]).astype(BF16)
    q = jnp.concatenate([qn_ref[...], qpe], axis=1)
    s = _dot_t(q, k_scr[...])
    p = jnp.exp2((s - jnp.max(s, axis=-1, keepdims=True)) * (scale * LOG2E)).astype(BF16)
    ol = _dot(p, v_scr[...])
    o_ref[...] = (ol[:, :LANES] / ol[:, LANES:]).astype(o_ref.dtype)


def _mla_latent(p_lat, p_ctx, kv_lat, kv_ctx, cos_tab, sin_tab, batch, tq=512):
    n = p_lat.shape[0] // batch
    lc = p_ctx.shape[0] // batch
    nq = n // tq
    h = MLA_HEADS
    blk = lambda rows, f: pl.BlockSpec((rows, LANES), f)
    return pl.pallas_call(
        functools.partial(_mla_lat_kernel, scale=MLA_QK_DIM ** -0.5),
        grid=(batch, h, nq),
        in_specs=[
            blk(tq, lambda b, hh, i: (b * nq + i, EV_QMLA_BLK + 2 * hh)),
            blk(tq, lambda b, hh, i: (b * nq + i, EV_QMLA_BLK + 2 * hh + 1)),
            blk(tq, lambda b, hh, i: (i, 0)),
            blk(tq, lambda b, hh, i: (i, 0)),
            blk(lc, lambda b, hh, i: (b, 2 * hh)),
            blk(lc, lambda b, hh, i: (b, EV_KPE_BLK)),
            blk(lc, lambda b, hh, i: (b, 2 * hh + 1)),
            blk(n, lambda b, hh, i: (b, 2 * hh)),
            blk(n, lambda b, hh, i: (b, EV_KPE_BLK)),
            blk(n, lambda b, hh, i: (0, 0)),
            blk(n, lambda b, hh, i: (0, 0)),
            blk(n, lambda b, hh, i: (b, 2 * hh + 1)),
        ],
        out_specs=blk(tq, lambda b, hh, i: (b * nq + i, hh)),
        out_shape=jax.ShapeDtypeStruct((batch * n, h * LANES), BF16),
        scratch_shapes=[pltpu.VMEM((lc + n, 2 * LANES), BF16), pltpu.VMEM((lc + n, 2 * LANES), BF16)],
        compiler_params=_cparams(("arbitrary", "arbitrary", "arbitrary")),
    )(p_lat, p_lat, cos_tab, sin_tab, kv_ctx, p_ctx, kv_ctx, kv_lat, p_lat, cos_tab, sin_tab, kv_lat)


def _ctx_attn_kernel(qm_ref, kn_ref, kpe_ref, vm_ref, qn_ref, kna_ref, vna_ref, om_ref, on_ref,
                     *, mla_scale, na_scale):
    k = jnp.concatenate([kn_ref[...], kpe_ref[...]], axis=1)
    s = _dot_t(qm_ref[...], k)
    om_ref[...] = _softmax_pv([s], [vm_ref[...]], mla_scale).astype(om_ref.dtype)
    s = _dot_t(qn_ref[...], kna_ref[...])
    on_ref[...] = _softmax_pv([s], [vna_ref[...]], na_scale).astype(on_ref.dtype)


def _ctx_attention(p_ctx, kv_ctx, batch):
    lc = p_ctx.shape[0] // batch
    h = MLA_HEADS
    blk = lambda f: pl.BlockSpec((lc, LANES), f)
    out = jax.ShapeDtypeStruct((batch * lc, h * LANES), BF16)
    return pl.pallas_call(
        functools.partial(_ctx_attn_kernel, mla_scale=MLA_QK_DIM ** -0.5, na_scale=NA_DIM ** -0.5),
        grid=(batch, h),
        in_specs=[
            pl.BlockSpec((lc, 2 * LANES), lambda b, hh: (b, hh)),
            blk(lambda b, hh: (b, 2 * hh)),
            blk(lambda b, hh: (b, EV_KPE_BLK)),
            blk(lambda b, hh: (b, 2 * hh + 1)),
            blk(lambda b, hh: (b, EV_QNA_BLK + hh)),
            blk(lambda b, hh: (b, EV_KNA_BLK + hh)),
            blk(lambda b, hh: (b, EV_VNA_BLK + hh)),
        ],
        out_specs=[blk(lambda b, hh: (b, hh)), blk(lambda b, hh: (b, hh))],
        out_shape=[out, out],
        compiler_params=_cparams(("arbitrary", "arbitrary")),
    )(p_ctx, kv_ctx, p_ctx, kv_ctx, p_ctx, p_ctx, p_ctx)


def _na_kernel(q_ref, k_ref, v_ref, kc_ref, vc_ref, bias_ref, o_ref, s_scr, p_scr, l_scr, oc_scr,
               *, scale, n_rows):
    win = NA_KH * GRID_W
    slab = 256
    n = q_ref.shape[0]

    def band(r):
        ws = min(max(r - NA_KH // 2, 0), n_rows - NA_KH)
        return ws, slice(r * GRID_W, (r + 1) * GRID_W), slice(ws * GRID_W, ws * GRID_W + win)

    s_scr[:, win:] = _dot_t(q_ref[...], kc_ref[...]) * scale
    for r in range(n_rows):
        ws, rows, keys = band(r)
        s_scr[rows, :win] = _dot_t(q_ref[rows, :], k_ref[keys, :]) * scale + bias_ref[r - ws]

    def body(i, carry):
        sl = pl.ds(pl.multiple_of(i * slab, slab), slab)
        s = s_scr[sl, :]
        p = jnp.exp2((s - jnp.max(s, axis=-1, keepdims=True)) * LOG2E)
        p_scr[sl, :] = p.astype(BF16)
        l_scr[sl, :] = jnp.broadcast_to(1.0 / jnp.sum(p, axis=-1, keepdims=True), (slab, LANES))
        return carry

    lax.fori_loop(0, n // slab, body, 0)
    oc_scr[...] = _dot(p_scr[:, win:], vc_ref[...])
    for r in range(n_rows):
        ws, rows, keys = band(r)
        o = _dot(p_scr[rows, :win], v_ref[keys, :]) + oc_scr[rows, :]
        o_ref[rows, :] = (o * l_scr[rows, :]).astype(o_ref.dtype)


def _na_bias_kernel(rb_ref, onehot_ref, mask_ref, o_ref):
    o_ref[...] = _dot_f32(rb_ref[...], onehot_ref[...]) + mask_ref[...]


def _na_bias_table(rel_bias):
    n_heads, n_ro, n_co = rel_bias.shape
    col = np.arange(GRID_W)
    col_start = np.clip(col - NA_KW // 2, 0, GRID_W - NA_KW)
    col_mask = (col[None, :] >= col_start[:, None]) & (col[None, :] < col_start[:, None] + NA_KW)
    col_off = np.clip(col[None, :] - col[:, None], 1 - NA_KW, NA_KW - 1) + (NA_KW - 1)
    onehot = (col_off.reshape(1, -1) == np.arange(n_co)[:, None]).astype(np.float32)
    mask_add = np.where(col_mask.reshape(1, -1), 0.0, NEG_INF).astype(np.float32)
    qw = GRID_W * GRID_W
    full = lambda shape: pl.BlockSpec(shape, lambda: (0,) * len(shape))
    cols = pl.pallas_call(
        _na_bias_kernel,
        in_specs=[full((n_heads * n_ro, n_co)), full((n_co, qw)), full((1, qw))],
        out_specs=full((n_heads * n_ro, qw)),
        out_shape=jax.ShapeDtypeStruct((n_heads * n_ro, qw), F32),
    )(rel_bias.reshape(n_heads * n_ro, n_co), jnp.asarray(onehot), jnp.asarray(mask_add))
    cols = cols.reshape(n_heads, n_ro, GRID_W, GRID_W)
    t = jnp.stack([cols[:, NA_KH - 1 - e:2 * NA_KH - 1 - e] for e in range(NA_KH)], axis=1)
    return t.transpose(0, 1, 3, 2, 4).reshape(n_heads, NA_KH, GRID_W, NA_KH * GRID_W)


def _na_latent(p_lat, p_ctx, bias_tab, batch):
    n = p_lat.shape[0] // batch
    lc = p_ctx.shape[0] // batch
    h = NA_HEADS
    n_rows = n // GRID_W
    assert n_rows >= NA_KH
    blk = lambda rows, f: pl.BlockSpec((rows, LANES), f)
    return pl.pallas_call(
        functools.partial(_na_kernel, scale=NA_DIM ** -0.5, n_rows=n_rows),
        grid=(batch, h),
        in_specs=[
            blk(n, lambda b, hh: (b, EV_QNA_BLK + hh)),
            blk(n, lambda b, hh: (b, EV_KNA_BLK + hh)),
            blk(n, lambda b, hh: (b, EV_VNA_BLK + hh)),
            blk(lc, lambda b, hh: (b, EV_KNA_BLK + hh)),
            blk(lc, lambda b, hh: (b, EV_VNA_BLK + hh)),
            pl.BlockSpec((None, NA_KH, GRID_W, NA_KH * GRID_W), lambda b, hh: (hh, 0, 0, 0)),
        ],
        out_specs=blk(n, lambda b, hh: (b, hh)),
        out_shape=jax.ShapeDtypeStruct((batch * n, h * LANES), BF16),
        scratch_shapes=[pltpu.VMEM((n, NA_KH * GRID_W + lc), F32), pltpu.VMEM((n, NA_KH * GRID_W + lc), BF16),
                        pltpu.VMEM((n, LANES), F32), pltpu.VMEM((n, LANES), F32)],
        compiler_params=_cparams(("arbitrary", "arbitrary")),
    )(p_lat, p_lat, p_lat, p_ctx, p_ctx, bias_tab)


def _rope_tables(n):
    pos = np.arange(n)
    rows, cols = pos // GRID_W, pos % GRID_W
    half = MLA_ROPE_DIM // 2
    inv_freq = ROPE_THETA ** (-np.arange(0, half, 2, dtype=np.float64) / half)
    cos = np.zeros((n, LANES), np.float64)
    sin = np.zeros((n, LANES), np.float64)
    for base, p in ((0, rows), (half, cols)):
        ang = p[:, None].astype(np.float64) * inv_freq[None, :]
        q = half // 2
        cos[:, base:base + q] = np.cos(ang)
        cos[:, base + q:base + half] = np.cos(ang)
        sin[:, base:base + q] = -np.sin(ang)
        sin[:, base + q:base + half] = np.sin(ang)
    return jnp.asarray(cos, F32), jnp.asarray(sin, F32)


def _even_w_in(w):
    d = w.shape[0]
    z64 = jnp.zeros((d, LANES - MLA_ROPE_DIM), w.dtype)
    pieces = []
    for h in range(MLA_HEADS):
        pieces += [w[:, h * MLA_QK_DIM:h * MLA_QK_DIM + MLA_NOPE_DIM],
                   w[:, h * MLA_QK_DIM + MLA_NOPE_DIM:(h + 1) * MLA_QK_DIM], z64]
    q_end = MLA_HEADS * MLA_QK_DIM
    ckv_end = q_end + MLA_KV_RANK
    kpe_end = ckv_end + MLA_ROPE_DIM
    pieces += [w[:, kpe_end:], w[:, q_end:ckv_end], w[:, ckv_end:kpe_end], z64]
    out = jnp.concatenate(pieces, axis=1)
    pad = EV_WIDTH - out.shape[1]
    return jnp.concatenate([out, jnp.zeros((d, pad), w.dtype)], axis=1).astype(BF16)


def _hgrn_gates(q_ref, i_ref, z_refs, lbs, row0, qb_s, v_s, cum_s, cpk_s, qd_s, kd_s, dec_s):
    c = HGRN_CHUNK
    grp = LANES
    ri = lax.broadcasted_iota(jnp.int32, (grp, grp), 0)
    ci = lax.broadcasted_iota(jnp.int32, (grp, grp), 1)
    same = (ri // c) == (ci // c)
    blk = jnp.where(same, 1.0, 0.0).astype(BF16)
    tris = (jnp.where(jnp.logical_and(same, ci <= ri), 1.0, 0.0).astype(BF16),
            jnp.where(jnp.logical_and(same, ci >= ri), 1.0, 0.0).astype(BF16))

    def body(g, carry):
        src = pl.ds(pl.multiple_of(g * grp, grp), grp)
        dst = pl.ds(pl.multiple_of(row0 + g * grp, grp), grp)
        q = _silu(q_ref[src, :].astype(F32))
        qb_s[dst, :] = q.astype(BF16)
        v_s[dst, :] = i_ref[src, :].astype(F32)
        for d in range(2):
            f = jnp.maximum(lbs[d] + (1.0 - lbs[d]) * _sigmoid(z_refs[d][src, :].astype(F32)), FORGET_FLOOR)
            lf = jnp.log(f)
            k = 1.0 - f
            cum = _dot_01(tris[d], lf)
            tot = _dot_01(blk, lf)
            cum_s[d, dst, :] = cum * LOG2E
            cpk_s[d, dst, :] = (cum - jnp.log(k)) * LOG2E
            qd_s[d, dst, :] = (q * jnp.exp(cum)).astype(BF16)
            kd_s[d, dst, :] = (k * jnp.exp(tot - cum)).astype(BF16)
            dec_s[d, dst, :] = jnp.exp(tot)
        return carry

    lax.fori_loop(0, q_ref.shape[0] // grp, body, 0)


def _hgrn_chunk(qb, cum2, cpk2, v, qd, st, ones_bf, reverse):
    c = HGRN_CHUNK
    hc = c // 2
    o = _dot_t(qd, st.astype(BF16))
    rows = lax.broadcasted_iota(jnp.int32, (hc, LANES), 0)
    halves = (cum2[:hc], cum2[hc:])
    zero = jnp.zeros((hc, LANES), F32)
    pieces = []
    for s in range(c):
        ref = cpk2[s:s + 1]
        hs, rs = divmod(s, hc)
        es = []
        for hh in range(2):
            if hh == hs:
                mask = (rows <= rs) if reverse else (rows >= rs)
                es.append(jnp.exp2(jnp.where(mask, halves[hh] - ref, NEG_INF)))
            elif (hh > hs) != reverse:
                es.append(jnp.exp2(halves[hh] - ref))
            else:
                es.append(zero)
        pieces.append(jnp.concatenate(es, axis=0).astype(BF16) * qb)
    lhs = jnp.concatenate([jnp.concatenate(pieces[:hc], axis=0), jnp.concatenate(pieces[hc:], axis=0)], axis=1)
    r = _dot(lhs, ones_bf)
    for s in range(c):
        hs, rs = divmod(s, hc)
        o = o + r[rs * c:(rs + 1) * c, hs * LANES:(hs + 1) * LANES] * v[s:s + 1]
    return o


def _hgrn_kernel(ql_ref, il_ref, zfl_ref, zbl_ref, gl_ref, qc_ref, ic_ref, zfc_ref, zbc_ref, gc_ref,
                 lbl_ref, ng_ref, yl_ref, yc_ref,
                 qb_s, v_s, cum_s, cpk_s, qd_s, kd_s, dec_s, o_s, *, layer):
    c = HGRN_CHUNK
    lc, n = qc_ref.shape[0], ql_ref.shape[0]
    lbs = []
    for d in range(2):
        lg = lbl_ref[d]
        ex = jnp.exp(lg - jnp.max(lg, axis=0, keepdims=True))
        p = ex / jnp.sum(ex, axis=0, keepdims=True)
        lbs.append(jnp.sum(p[:layer + 1], axis=0, keepdims=True) - p[0:1])

    scr = (qb_s, v_s, cum_s, cpk_s, qd_s, kd_s, dec_s)
    _hgrn_gates(qc_ref, ic_ref, (zfc_ref, zbc_ref), lbs, 0, *scr)
    _hgrn_gates(ql_ref, il_ref, (zfl_ref, zbl_ref), lbs, lc, *scr)

    ri = lax.broadcasted_iota(jnp.int32, (2 * LANES, 2 * LANES), 0)
    ci = lax.broadcasted_iota(jnp.int32, (2 * LANES, 2 * LANES), 1)
    ones_bf = jnp.where((ri // LANES) == (ci // LANES), 1.0, 0.0).astype(BF16)
    nc_ctx = lc // c
    n_chunks = (lc + n) // c

    def body(g, sts):
        slices = ([], [])
        for i in range(HGRN_UNROLL):
            j = g * HGRN_UNROLL + i
            r_fwd = j * c
            r_bwd = jnp.where(j < nc_ctx, lc - c - j * c, 2 * lc + n - c - j * c)
            for d, r in enumerate((r_fwd, r_bwd)):
                slices[d].append(pl.ds(pl.multiple_of(r, c), c))
        incs = [[_dot_tn(v_s[sl, :].astype(BF16), kd_s[d, sl, :]) for sl in slices[d]] for d in range(2)]
        sts = list(sts)
        for i in range(HGRN_UNROLL):
            for d in range(2):
                sl = slices[d][i]
                o_s[d, sl, :] = _hgrn_chunk(qb_s[sl, :], cum_s[d, sl, :], cpk_s[d, sl, :], v_s[sl, :],
                                            qd_s[d, sl, :], sts[d], ones_bf, reverse=d == 1)
                sts[d] = sts[d] * dec_s[d, sl, :][0:1] + incs[d][i]
        return tuple(sts)

    zero = jnp.zeros((LANES, LANES), F32)
    assert n_chunks % HGRN_UNROLL == 0 and nc_ctx % HGRN_UNROLL == 0
    lax.fori_loop(0, n_chunks // HGRN_UNROLL, body, (zero, zero))

    ng = ng_ref[...]
    yc_ref[...] = (_rms(o_s[0, :lc, :] + o_s[1, :lc, :]) * ng
                   * _silu(gc_ref[...].astype(F32))).astype(yc_ref.dtype)
    yl_ref[...] = (_rms(o_s[0, lc:, :] + o_s[1, lc:, :]) * ng
                   * _silu(gl_ref[...].astype(F32))).astype(yl_ref.dtype)


def _hgrn2(p_lat, p_ctx, lb_logits, norm_g, layer, batch):
    n = p_lat.shape[0] // batch
    lc = p_ctx.shape[0] // batch
    h = HGRN_HEADS
    n_layers = lb_logits.shape[1]
    rows = lc + n
    assert lc % LANES == 0 and n % LANES == 0
    lat = lambda part: pl.BlockSpec((n, LANES), lambda b, hh: (b, part * h + hh))
    ctx = lambda part: pl.BlockSpec((lc, LANES), lambda b, hh: (b, part * h + hh))
    return pl.pallas_call(
        functools.partial(_hgrn_kernel, layer=layer),
        grid=(batch, h),
        in_specs=[lat(0), lat(1), lat(2), lat(3), lat(4), ctx(0), ctx(1), ctx(2), ctx(3), ctx(4),
                  pl.BlockSpec((2, n_layers, LANES), lambda b, hh: (0, 0, hh)),
                  pl.BlockSpec((1, LANES), lambda b, hh: (0, hh))],
        out_specs=[pl.BlockSpec((n, LANES), lambda b, hh: (b, hh)),
                   pl.BlockSpec((lc, LANES), lambda b, hh: (b, hh))],
        out_shape=[jax.ShapeDtypeStruct((batch * n, h * LANES), BF16),
                   jax.ShapeDtypeStruct((batch * lc, h * LANES), BF16)],
        scratch_shapes=[pltpu.VMEM((rows, LANES), BF16), pltpu.VMEM((rows, LANES), F32),
                        pltpu.VMEM((2, rows, LANES), F32), pltpu.VMEM((2, rows, LANES), F32),
                        pltpu.VMEM((2, rows, LANES), BF16), pltpu.VMEM((2, rows, LANES), BF16),
                        pltpu.VMEM((2, rows, LANES), F32), pltpu.VMEM((2, rows, LANES), F32)],
        compiler_params=_cparams(("arbitrary", "arbitrary")),
    )(p_lat, p_lat, p_lat, p_lat, p_lat, p_ctx, p_ctx, p_ctx, p_ctx, p_ctx,
      lb_logits, norm_g[layer].reshape(1, -1))


def _dft_matrices(n):
    idx = (np.arange(n)[:, None] * np.arange(n)[None, :]) % (2 * n)
    ang = idx.astype(np.float64) * (math.pi / n)
    cm = np.cos(ang)
    sf = np.sin(ang)
    sf[0, :] = (-1.0) ** np.arange(n)
    return (jnp.asarray(cm, F32).astype(BF16), jnp.asarray(sf, F32).astype(BF16),
            jnp.asarray(sf.T, F32).astype(BF16))


def _filter_features(n):
    pos = np.arange(n, dtype=np.float64)
    t = pos / max(n - 1, 1)
    bands = np.linspace(1e-4, HYENA_BANDS - 1, HYENA_BANDS)
    ang = (2.0 * math.pi / n) * pos[:, None] * bands[None, :]
    z = np.concatenate([t[:, None], np.cos(ang), -np.sin(ang)], -1)
    max_decay = math.log(HYENA_DECAY_TARGET) / HYENA_FAST_PCT
    min_decay = math.log(HYENA_DECAY_TARGET) / HYENA_SLOW_PCT
    deltas = np.abs(np.linspace(min_decay, max_decay, HYENA_WIDTH))
    return jnp.asarray(z, F32), jnp.asarray(t[:, None], F32), jnp.asarray(deltas[None, :], F32)


def _filter_kernel(z_ref, t_ref, dl_ref, w1_ref, b1_ref, w2_ref, b2_ref, w3_ref, b3_ref, fr_ref, wo_ref,
                   o_ref, hdn_ref):
    j = pl.program_id(0)

    @pl.when(j == 0)
    def _():
        fr = fr_ref[...]
        hdn = jnp.sin(fr * (_dot_f32(z_ref[...], w1_ref[...]) + b1_ref[...]))
        hdn = jnp.sin(fr * (_dot_f32(hdn, w2_ref[...]) + b2_ref[...]))
        hdn_ref[...] = jnp.sin(fr * (_dot_f32(hdn, w3_ref[...]) + b3_ref[...]))

    filt = _dot_f32(hdn_ref[...], wo_ref[...]) * jnp.exp(-t_ref[...] * dl_ref[...])
    row = lax.broadcasted_iota(jnp.int32, filt.shape, 0)
    is_bwd = j >= pl.num_programs(0) // 2
    o_ref[...] = jnp.where(jnp.logical_and(is_bwd, row == 0), 0.0, filt).astype(o_ref.dtype)


def _hyena_filters(n, w1, b1, w2, b2, w3, b3, freq, w_out, tc=512):
    z, t, deltas = _filter_features(n)
    hid = HYENA_FILT_HIDDEN
    nct = HYENA_WIDTH // tc
    full = lambda shape: pl.BlockSpec(shape, lambda j: (0,) * len(shape))
    return pl.pallas_call(
        _filter_kernel,
        grid=(2 * nct,),
        in_specs=[full((n, HYENA_EMB)), full((n, 1)),
                  pl.BlockSpec((1, tc), lambda j: (0, j % nct)),
                  full((HYENA_EMB, hid)), full((1, hid)), full((hid, hid)), full((1, hid)),
                  full((hid, hid)), full((1, hid)), full((1, hid)),
                  pl.BlockSpec((hid, tc), lambda j: (0, j))],
        out_specs=pl.BlockSpec((n, tc), lambda j: (0, j)),
        out_shape=jax.ShapeDtypeStruct((n, 2 * HYENA_WIDTH), BF16),
        scratch_shapes=[pltpu.VMEM((n, hid), F32)],
        compiler_params=_cparams(("arbitrary",)),
    )(z, t, deltas, w1, b1.reshape(1, hid), w2, b2.reshape(1, hid), w3, b3.reshape(1, hid),
      freq.reshape(1, hid), w_out)


def _spectrum_kernel(cm_ref, sf_ref, hf_ref, hb_ref, a_ref, b_ref, *, inv_len):
    cm, sf, hf, hb = cm_ref[...], sf_ref[...], hf_ref[...], hb_ref[...]
    kr = _dot(cm, hf) + _dot(cm, hb)
    d1 = _dot(sf, hf)
    d2 = _dot(sf, hb)
    row = lax.broadcasted_iota(jnp.int32, kr.shape, 0) + pl.program_id(0) * kr.shape[0]
    first = row == 0
    w = jnp.where(first, inv_len, 2.0 * inv_len)
    a_ref[...] = kr * w
    b_ref[...] = jnp.where(first, d1 + d2, d1 - d2) * w


def _filter_spectrum(filt, cm, sf, tk=512, tc=512):
    n = filt.shape[0]
    tk = min(tk, n)
    nct = HYENA_WIDTH // tc
    out = jax.ShapeDtypeStruct((n, HYENA_WIDTH), F32)
    return pl.pallas_call(
        functools.partial(_spectrum_kernel, inv_len=1.0 / (2 * n)),
        grid=(n // tk, nct),
        in_specs=[pl.BlockSpec((tk, n), lambda i, j: (i, 0)),
                  pl.BlockSpec((tk, n), lambda i, j: (i, 0)),
                  pl.BlockSpec((n, tc), lambda i, j: (0, j)),
                  pl.BlockSpec((n, tc), lambda i, j: (0, nct + j))],
        out_specs=[pl.BlockSpec((tk, tc), lambda i, j: (i, j)), pl.BlockSpec((tk, tc), lambda i, j: (i, j))],
        out_shape=[out, out],
        compiler_params=_cparams(("arbitrary", "arbitrary")),
    )(cm, sf, filt, filt)


def _hyena_gate_kernel(u0_ref, u1_ref, uv_ref, w0_ref, w1_ref, wv_ref, b0_ref, b1_ref, bv_ref,
                       x0_ref, z_ref):
    n = u0_ref.shape[0]
    row = lax.broadcasted_iota(jnp.int32, u0_ref.shape, 0)

    def conv(u_ref, w_ref, b_ref):
        u = u_ref[...].astype(F32)
        prev = jnp.where(row == 0, 0.0, pltpu.roll(u, 1, 0))
        nxt = jnp.where(row == n - 1, 0.0, pltpu.roll(u, n - 1, 0))
        return b_ref[...] + prev * w_ref[0:1] + u * w_ref[1:2] + nxt * w_ref[2:3]

    x0_ref[...] = conv(u0_ref, w0_ref, b0_ref)
    z_ref[...] = conv(uv_ref, wv_ref, bv_ref) * conv(u1_ref, w1_ref, b1_ref)


def _hyena_gate(p, first_blk, conv_w, conv_b, batch, tc=256):
    n = p.shape[0] // batch
    nct = HYENA_WIDTH // tc
    c0 = first_blk * LANES // tc
    u = lambda part: pl.BlockSpec((n, tc), lambda b, j: (b, c0 + part * nct + j))
    w = lambda part: pl.BlockSpec((HYENA_SHORT, tc), lambda b, j: (0, part * nct + j))
    bb = lambda part: pl.BlockSpec((1, tc), lambda b, j: (0, part * nct + j))
    out = jax.ShapeDtypeStruct((batch * n, HYENA_WIDTH), F32)
    cb = conv_b.reshape(1, -1)
    return pl.pallas_call(
        _hyena_gate_kernel,
        grid=(batch, nct),
        in_specs=[u(0), u(1), u(2), w(0), w(1), w(2), bb(0), bb(1), bb(2)],
        out_specs=[pl.BlockSpec((n, tc), lambda b, j: (b, j)), pl.BlockSpec((n, tc), lambda b, j: (b, j))],
        out_shape=[out, out],
        compiler_params=_cparams(("arbitrary", "arbitrary")),
    )(p, p, p, conv_w, conv_w, conv_w, cb, cb, cb)


def _dft_fwd_kernel(cm_ref, sf_ref, z_ref, a_ref, b_ref, pr_ref, ps_ref):
    z = z_ref[...].astype(BF16)
    zr = _dot(cm_ref[...], z)
    zs = _dot(sf_ref[...], z)
    a, b = a_ref[...], b_ref[...]
    row = lax.broadcasted_iota(jnp.int32, zr.shape, 0) + pl.program_id(1) * zr.shape[0]
    first = row == 0
    pr_ref[...] = (zr * a - jnp.where(first, 0.0, zs * b)).astype(pr_ref.dtype)
    ps_ref[...] = (jnp.where(first, 0.0, zr * b) + zs * jnp.where(first, b, a)).astype(ps_ref.dtype)


def _dft_inv_kernel(cm_ref, si_ref, pr_ref, ps_ref, z_ref, x0_ref, skip_ref, o_ref):
    y = _dot(cm_ref[...], pr_ref[...]) + _dot(si_ref[...], ps_ref[...])
    o_ref[...] = (x0_ref[...] * (y + z_ref[...] * skip_ref[...])).astype(o_ref.dtype)


def _long_conv(z, x0, spec_a, spec_b, skip, cm, sf, si, batch, tk=1024, tc=512):
    n = z.shape[0] // batch
    tk = min(tk, n)
    nk = n // tk
    nct = HYENA_WIDTH // tc
    mat = pl.BlockSpec((tk, n), lambda b, i, j: (i, 0))
    col = pl.BlockSpec((n, tc), lambda b, i, j: (b, j))
    tile_nb = pl.BlockSpec((tk, tc), lambda b, i, j: (i, j))
    tile = pl.BlockSpec((tk, tc), lambda b, i, j: (b * nk + i, j))
    spec_shape = jax.ShapeDtypeStruct((batch * n, HYENA_WIDTH), BF16)
    pr, ps = pl.pallas_call(
        _dft_fwd_kernel,
        grid=(batch, nk, nct),
        in_specs=[mat, mat, col, tile_nb, tile_nb],
        out_specs=[tile, tile],
        out_shape=[spec_shape, spec_shape],
        compiler_params=_cparams(("arbitrary", "arbitrary", "arbitrary")),
    )(cm, sf, z, spec_a, spec_b)
    return pl.pallas_call(
        _dft_inv_kernel,
        grid=(batch, nk, nct),
        in_specs=[mat, mat, col, col, tile, tile, pl.BlockSpec((1, tc), lambda b, i, j: (0, j))],
        out_specs=tile,
        out_shape=jax.ShapeDtypeStruct((batch * n, HYENA_WIDTH), BF16),
        compiler_params=_cparams(("arbitrary", "arbitrary", "arbitrary")),
    )(cm, si, pr, ps, z, x0, skip.reshape(1, -1))


def _hyena(p, first_blk, conv_w, conv_b, filt_params, skip, batch):
    n = p.shape[0] // batch
    cm, sf, si = _dft_matrices(n)
    filt = _hyena_filters(n, *filt_params)
    spec_a, spec_b = _filter_spectrum(filt, cm, sf)
    x0, z = _hyena_gate(p, first_blk, conv_w, conv_b, batch)
    return _long_conv(z, x0, spec_a, spec_b, skip, cm, sf, si, batch)


def kernel(x, c, ctx, c_ctx, ada_w, ada_b, norm_mix_g, norm_mlp_g, w_out, mlp_w1, mlp_w2, final_norm_g, ev_w_in, mla_kv_norm_g, mla_w_ukv, na_rel_bias, od_w_in, hgrn_lb_logits, hgrn_norm_g, hy_conv_w, hy_conv_b, hy_filt_w1, hy_filt_b1, hy_filt_w2, hy_filt_b2, hy_filt_w3, hy_filt_b3, hy_filt_freq, hy_filt_wout, hy_skip):
    batch, seq, d = x.shape
    lc = ctx.shape[1]
    depth = ada_w.shape[0]
    h_lat = x.reshape(batch * seq, d)
    h_ctx = ctx.reshape(batch * lc, d)

    cond = jnp.concatenate([c, c_ctx[None, :], jnp.zeros((8 - batch - 1, d), F32)], axis=0)
    mod_all = _ada_modulation(cond, ada_w, ada_b)
    cos_tab, sin_tab = _rope_tables(seq)
    hgrn_cols = 5 * HGRN_WIDTH
    od_w_in, w_out, mlp_w1, mlp_w2, mla_w_ukv = (
        t.astype(BF16) for t in (od_w_in, w_out, mlp_w1, mlp_w2, mla_w_ukv))

    for l in range(depth):
        ctx_out = l < depth - 1
        mod3 = mod_all[l].reshape(8, 1, 6 * d)
        lat_mod = lambda *chunks: (mod3, *chunks, seq, 0)
        ctx_mod = lambda *chunks: (mod3, *chunks, batch * lc, batch)
        if l % 2 == 0:
            e = l // 2
            w_in = _even_w_in(ev_w_in[e])
            p_lat = _norm_proj(h_lat, 0, d, norm_mix_g[l], w_in, BF16, lat_mod(0, 1))
            p_ctx = _norm_proj(h_ctx, 0, d, norm_mix_g[l], w_in, BF16, ctx_mod(0, 1))
            ckv_blk = EV_CKV_BLK * LANES // MLA_KV_RANK
            kv_lat = _norm_proj(p_lat, ckv_blk, MLA_KV_RANK, mla_kv_norm_g[e], mla_w_ukv, BF16, layer=e)
            kv_ctx = _norm_proj(p_ctx, ckv_blk, MLA_KV_RANK, mla_kv_norm_g[e], mla_w_ukv, BF16, layer=e)
            y1_lat = _mla_latent(p_lat, p_ctx, kv_lat, kv_ctx, cos_tab, sin_tab, batch)
            y2_lat = _na_latent(p_lat, p_ctx, _na_bias_table(na_rel_bias[e]), batch)
            if ctx_out:
                y1_ctx, y2_ctx = _ctx_attention(p_ctx, kv_ctx, batch)
        else:
            o = l // 2
            p_lat = _norm_proj(h_lat, 0, d, norm_mix_g[l], od_w_in, BF16, lat_mod(0, 1), layer=o)
            p_ctx = _norm_proj(h_ctx, 0, d, norm_mix_g[l], od_w_in, BF16, ctx_mod(0, 1), layer=o,
                               n=None if ctx_out else hgrn_cols)
            y1_lat, y1_ctx = _hgrn2(p_lat, p_ctx, hgrn_lb_logits, hgrn_norm_g, o, batch)
            filt_params = (hy_filt_w1[o], hy_filt_b1[o], hy_filt_w2[o], hy_filt_b2[o], hy_filt_w3[o],
                           hy_filt_b3[o], hy_filt_freq[o], hy_filt_wout[o])
            y2_lat = _hyena(p_lat, hgrn_cols // LANES, hy_conv_w[o], hy_conv_b[o], filt_params,
                            hy_skip[o], batch)
            if ctx_out:
                y2_ctx = _hyena(p_ctx, hgrn_cols // LANES, hy_conv_w[o], hy_conv_b[o], filt_params,
                                hy_skip[o], batch)
        h_lat = _out_proj(y1_lat, y2_lat, w_out, l, h_lat, lat_mod(2))
        h_lat = _mlp(h_lat, norm_mlp_g[l], mlp_w1, mlp_w2, l, lat_mod(3, 4, 5),
                     final_g=None if ctx_out else final_norm_g)
        if ctx_out:
            h_ctx = _out_proj(y1_ctx, y2_ctx, w_out, l, h_ctx, ctx_mod(2))
            h_ctx = _mlp(h_ctx, norm_mlp_g[l], mlp_w1, mlp_w2, l, ctx_mod(3, 4, 5))
    return h_lat.reshape(batch, seq, d)
```

```python
import functools
import math

import numpy as np
import jax
import jax.numpy as jnp
from jax import lax
from jax.experimental import pallas as pl
from jax.experimental.pallas import tpu as pltpu

F32 = jnp.float32
BF16 = jnp.bfloat16

D_MODEL = 2048
DEPTH = 4
GRID_W = 64
HEAD_DIM = 128
MLA_HEADS = 8
MLA_NOPE_DIM = 128
MLA_ROPE_DIM = 64
MLA_QK_DIM = MLA_NOPE_DIM + MLA_ROPE_DIM
MLA_KV_RANK = 512
NA_HEADS = 8
NA_DIM = 128
NA_KH = 8
NA_KW = 16
ROPE_THETA = 10000.0
HGRN_WIDTH = 1024
HGRN_HEADS = 8
FORGET_FLOOR = 1e-30
HYENA_WIDTH = 1024
HYENA_SHORT = 3
HYENA_EMB = 33
HYENA_BANDS = (HYENA_EMB - 1) // 2
HYENA_FILT_HIDDEN = 64
HYENA_DECAY_TARGET = 1e-2
HYENA_FAST_PCT = 0.3
HYENA_SLOW_PCT = 1.5
MLP_HIDDEN = 4 * D_MODEL
NORM_EPS = 1e-6
NEG_INF = -1e30
LOG2E = 1.4426950408889634

LANES = 128
VMEM_LIMIT_BYTES = 56 * 1024 * 1024

EV_QMLA_BLK = 0
EV_QNA_BLK = 16
EV_KNA_BLK = 24
EV_VNA_BLK = 32
EV_CKV_BLK = 40
EV_KPE_BLK = 44
EV_WIDTH = 48 * LANES
HGRN_CHUNK = 16
HGRN_UNROLL = 8
HGRN_FAST_CHUNK = 32
HGRN_FAST_UNROLL = 4
HGRN_FAST_MAX_DECAY = 80.0


def _cparams(sem):
    return pltpu.CompilerParams(dimension_semantics=sem, vmem_limit_bytes=VMEM_LIMIT_BYTES)


def _dot(a, b):
    return jnp.dot(a, b, preferred_element_type=F32)


def _dot_t(a, b):
    return lax.dot_general(a, b, (((1,), (1,)), ((), ())), preferred_element_type=F32)


def _dot_tn(a, b):
    return lax.dot_general(a, b, (((0,), (0,)), ((), ())), preferred_element_type=F32)


def _dot_f32(a, b):
    return jnp.dot(a, b, preferred_element_type=F32, precision=lax.Precision.HIGHEST)


def _dot_01(m01, x):
    hi = x.astype(BF16)
    lo = (x - hi.astype(F32)).astype(BF16)
    return _dot(m01, hi) + _dot(m01, lo)


def _sigmoid(x):
    return 1.0 / (1.0 + jnp.exp(-x))


def _silu(x):
    return x * _sigmoid(x)


def _rms(x):
    return x * lax.rsqrt(jnp.mean(x * x, axis=-1, keepdims=True) + NORM_EPS)


def _ada_kernel(s_ref, w_ref, b_ref, o_ref):
    s = _silu(s_ref[...]).astype(BF16)
    o_ref[...] = _dot(s, w_ref[...].astype(BF16)) + b_ref[...]


def _ada_modulation(cond, ada_w, ada_b, tn=1024):
    depth, d, n = ada_w.shape
    rows = cond.shape[0]
    return pl.pallas_call(
        _ada_kernel,
        grid=(depth, n // tn),
        in_specs=[
            pl.BlockSpec((rows, d), lambda l, j: (0, 0)),
            pl.BlockSpec((None, d, tn), lambda l, j: (l, 0, j)),
            pl.BlockSpec((None, 1, tn), lambda l, j: (l, 0, j)),
        ],
        out_specs=pl.BlockSpec((None, rows, tn), lambda l, j: (l, 0, j)),
        out_shape=jax.ShapeDtypeStruct((depth, rows, n), F32),
        compiler_params=_cparams(("arbitrary", "arbitrary")),
    )(cond, ada_w, ada_b.reshape(depth, 1, n))


def _mod_spec(chunk, tiles_per_group, group0, d):
    return pl.BlockSpec((None, 1, d), lambda i, *_: (group0 + i // tiles_per_group, 0, chunk))


def _proj_kernel(*refs, modulated):
    if modulated:
        x_ref, g_ref, sh_ref, sc_ref, w_ref, o_ref, a_ref = refs
    else:
        x_ref, g_ref, w_ref, o_ref, a_ref = refs

    @pl.when(pl.program_id(1) == 0)
    def _():
        y = _rms(x_ref[...].astype(F32)) * g_ref[...]
        if modulated:
            y = y * (1.0 + sc_ref[...]) + sh_ref[...]
        a_ref[...] = y.astype(BF16)

    o_ref[...] = _dot(a_ref[...], w_ref[...]).astype(o_ref.dtype)


def _layer_spec(w, layer, block, index_map):
    if w.ndim == 2:
        return pl.BlockSpec(block, index_map)
    return pl.BlockSpec((None,) + block, lambda *idx: (layer,) + index_map(*idx))


def _norm_proj(x, x_col_blk, k, g, w, out_dtype, mod=None, layer=None, n=None, tm=1024, tn=1024):
    m = x.shape[0]
    n = w.shape[-1] if n is None else n
    tm = min(tm, m)
    tn = min(tn, n)
    assert m % tm == 0 and n % tn == 0
    in_specs = [pl.BlockSpec((tm, k), lambda i, j: (i, x_col_blk)),
                pl.BlockSpec((1, k), lambda i, j: (0, 0))]
    args = [x, g.reshape(1, k)]
    if mod is not None:
        mod3, sh_chunk, sc_chunk, rows_per_group, group0 = mod
        assert rows_per_group % tm == 0
        in_specs += [_mod_spec(sh_chunk, rows_per_group // tm, group0, k),
                     _mod_spec(sc_chunk, rows_per_group // tm, group0, k)]
        args += [mod3, mod3]
    in_specs.append(_layer_spec(w, layer, (k, tn), lambda i, j: (0, j)))
    args.append(w)
    return pl.pallas_call(
        functools.partial(_proj_kernel, modulated=mod is not None),
        grid=(m // tm, n // tn),
        in_specs=in_specs,
        out_specs=pl.BlockSpec((tm, tn), lambda i, j: (i, j)),
        out_shape=jax.ShapeDtypeStruct((m, n), out_dtype),
        scratch_shapes=[pltpu.VMEM((tm, k), BF16)],
        compiler_params=_cparams(("arbitrary", "arbitrary")),
    )(*args)


def _outproj_kernel(y1_ref, y2_ref, w_ref, h_ref, gate_ref, o_ref):
    k1 = y1_ref.shape[1]
    acc = _dot(y1_ref[...], w_ref[:k1, :]) + _dot(y2_ref[...], w_ref[k1:, :])
    o_ref[...] = h_ref[...] + gate_ref[...] * acc


def _out_proj(y1, y2, w, layer, h, mod, tm=512):
    m, d = h.shape
    k1, k2 = y1.shape[1], y2.shape[1]
    tm = min(tm, m)
    mod3, gate_chunk, rows_per_group, group0 = mod
    return pl.pallas_call(
        _outproj_kernel,
        grid=(m // tm,),
        in_specs=[
            pl.BlockSpec((tm, k1), lambda i: (i, 0)),
            pl.BlockSpec((tm, k2), lambda i: (i, 0)),
            _layer_spec(w, layer, (k1 + k2, d), lambda i: (0, 0)),
            pl.BlockSpec((tm, d), lambda i: (i, 0)),
            _mod_spec(gate_chunk, rows_per_group // tm, group0, d),
        ],
        out_specs=pl.BlockSpec((tm, d), lambda i: (i, 0)),
        out_shape=jax.ShapeDtypeStruct((m, d), F32),
        compiler_params=_cparams(("arbitrary",)),
    )(y1, y2, w, h, mod3)


def _mlp_kernel(*refs, final_norm):
    if final_norm:
        h_ref, g_ref, sh_ref, sc_ref, gate_ref, w1_ref, w2_ref, fg_ref, o_ref, a_ref, acc_ref = refs
    else:
        h_ref, g_ref, sh_ref, sc_ref, gate_ref, w1_ref, w2_ref, o_ref, a_ref, acc_ref = refs
    k = pl.program_id(1)

    @pl.when(k == 0)
    def _():
        y = _rms(h_ref[...]) * g_ref[...]
        a_ref[...] = (y * (1.0 + sc_ref[...]) + sh_ref[...]).astype(BF16)
        acc_ref[...] = jnp.zeros_like(acc_ref)

    u = jnp.maximum(_dot(a_ref[...], w1_ref[...]), 0.0)
    acc_ref[...] += _dot((u * u).astype(BF16), w2_ref[...])

    @pl.when(k == pl.num_programs(1) - 1)
    def _():
        out = h_ref[...] + gate_ref[...] * acc_ref[...]
        if final_norm:
            out = _rms(out) * fg_ref[...]
        o_ref[...] = out


def _mlp(h, g, w1, w2, layer, mod, final_g=None, tm=512, th=1024):
    m, d = h.shape
    hid = w1.shape[-1]
    tm = min(tm, m)
    mod3, sh_chunk, sc_chunk, gate_chunk, rows_per_group, group0 = mod
    tpg = rows_per_group // tm
    in_specs = [
        pl.BlockSpec((tm, d), lambda i, k: (i, 0)),
        pl.BlockSpec((1, d), lambda i, k: (0, 0)),
        _mod_spec(sh_chunk, tpg, group0, d),
        _mod_spec(sc_chunk, tpg, group0, d),
        _mod_spec(gate_chunk, tpg, group0, d),
        _layer_spec(w1, layer, (d, th), lambda i, k: (0, k)),
        _layer_spec(w2, layer, (th, d), lambda i, k: (k, 0)),
    ]
    args = [h, g.reshape(1, d), mod3, mod3, mod3, w1, w2]
    if final_g is not None:
        in_specs.append(pl.BlockSpec((1, d), lambda i, k: (0, 0)))
        args.append(final_g.reshape(1, d))
    return pl.pallas_call(
        functools.partial(_mlp_kernel, final_norm=final_g is not None),
        grid=(m // tm, hid // th),
        in_specs=in_specs,
        out_specs=pl.BlockSpec((tm, d), lambda i, k: (i, 0)),
        out_shape=jax.ShapeDtypeStruct((m, d), F32),
        scratch_shapes=[pltpu.VMEM((tm, d), BF16), pltpu.VMEM((tm, d), F32)],
        compiler_params=_cparams(("arbitrary", "arbitrary")),
    )(*args)


def _softmax_pv(scores, values, scale=1.0):
    m = functools.reduce(jnp.maximum, [jnp.max(s, axis=-1, keepdims=True) for s in scores])
    ps = [jnp.exp2((s - m) * (scale * LOG2E)) for s in scores]
    denom = functools.reduce(jnp.add, [jnp.sum(p, axis=-1, keepdims=True) for p in ps])
    o = functools.reduce(jnp.add, [_dot(p.astype(BF16), v) for p, v in zip(ps, values)])
    return o / denom


def _rope_rotate(x, cos, sin):
    x = x.astype(F32)
    lane = lax.broadcasted_iota(jnp.int32, x.shape, 1)
    partner = jnp.where((lane % 32) < 16, pltpu.roll(x, LANES - 16, 1), pltpu.roll(x, 16, 1))
    return x * cos + partner * sin


def _mla_lat_kernel(qn_ref, qpe_ref, cosq_ref, sinq_ref, knc_ref, kpec_ref, vc_ref, knl_ref, kpel_ref,
                    cosk_ref, sink_ref, vl_ref, o_ref, k_scr, v_scr, *, scale):
    lc = knc_ref.shape[0]

    @pl.when(pl.program_id(2) == 0)
    def _():
        k_scr[:lc, :LANES] = knc_ref[...]
        k_scr[:lc, LANES:] = kpec_ref[...]
        k_scr[lc:, :LANES] = knl_ref[...]
        k_scr[lc:, LANES:] = _rope_rotate(kpel_ref[...], cosk_ref[...], sink_ref[...]).astype(BF16)
        v_scr[:lc, :LANES] = vc_ref[...]
        v_scr[lc:, :LANES] = vl_ref[...]
        v_scr[:, LANES:] = jnp.ones((v_scr.shape[0], LANES), BF16)

    qpe = _rope_rotate(qpe_ref[...], cosq_ref[...], sinq_ref[...]).astype(BF16)
    q = jnp.concatenate([qn_ref[...], qpe], axis=1)
    s = _dot_t(q, k_scr[...])
    p = jnp.exp2((s - jnp.max(s, axis=-1, keepdims=True)) * (scale * LOG2E)).astype(BF16)
    ol = _dot(p, v_scr[...])
    o_ref[...] = (ol[:, :LANES] / ol[:, LANES:]).astype(o_ref.dtype)


def _mla_latent(p_lat, p_ctx, kv_lat, kv_ctx, cos_tab, sin_tab, batch, tq=512):
    n = p_lat.shape[0] // batch
    lc = p_ctx.shape[0] // batch
    nq = n // tq
    h = MLA_HEADS
    blk = lambda rows, f: pl.BlockSpec((rows, LANES), f)
    return pl.pallas_call(
        functools.partial(_mla_lat_kernel, scale=MLA_QK_DIM ** -0.5),
        grid=(batch, h, nq),
        in_specs=[
            blk(tq, lambda b, hh, i: (b * nq + i, EV_QMLA_BLK + 2 * hh)),
            blk(tq, lambda b, hh, i: (b * nq + i, EV_QMLA_BLK + 2 * hh + 1)),
            blk(tq, lambda b, hh, i: (i, 0)),
            blk(tq, lambda b, hh, i: (i, 0)),
            blk(lc, lambda b, hh, i: (b, 2 * hh)),
            blk(lc, lambda b, hh, i: (b, EV_KPE_BLK)),
            blk(lc, lambda b, hh, i: (b, 2 * hh + 1)),
            blk(n, lambda b, hh, i: (b, 2 * hh)),
            blk(n, lambda b, hh, i: (b, EV_KPE_BLK)),
            blk(n, lambda b, hh, i: (0, 0)),
            blk(n, lambda b, hh, i: (0, 0)),
            blk(n, lambda b, hh, i: (b, 2 * hh + 1)),
        ],
        out_specs=blk(tq, lambda b, hh, i: (b * nq + i, hh)),
        out_shape=jax.ShapeDtypeStruct((batch * n, h * LANES), BF16),
        scratch_shapes=[pltpu.VMEM((lc + n, 2 * LANES), BF16), pltpu.VMEM((lc + n, 2 * LANES), BF16)],
        compiler_params=_cparams(("arbitrary", "arbitrary", "arbitrary")),
    )(p_lat, p_lat, cos_tab, sin_tab, kv_ctx, p_ctx, kv_ctx, kv_lat, p_lat, cos_tab, sin_tab, kv_lat)


def _ctx_attn_kernel(qm_ref, kn_ref, kpe_ref, vm_ref, qn_ref, kna_ref, vna_ref, om_ref, on_ref,
                     *, mla_scale, na_scale):
    k = jnp.concatenate([kn_ref[...], kpe_ref[...]], axis=1)
    s = _dot_t(qm_ref[...], k)
    om_ref[...] = _softmax_pv([s], [vm_ref[...]], mla_scale).astype(om_ref.dtype)
    s = _dot_t(qn_ref[...], kna_ref[...])
    on_ref[...] = _softmax_pv([s], [vna_ref[...]], na_scale).astype(on_ref.dtype)


def _ctx_attention(p_ctx, kv_ctx, batch):
    lc = p_ctx.shape[0] // batch
    h = MLA_HEADS
    blk = lambda f: pl.BlockSpec((lc, LANES), f)
    out = jax.ShapeDtypeStruct((batch * lc, h * LANES), BF16)
    return pl.pallas_call(
        functools.partial(_ctx_attn_kernel, mla_scale=MLA_QK_DIM ** -0.5, na_scale=NA_DIM ** -0.5),
        grid=(batch, h),
        in_specs=[
            pl.BlockSpec((lc, 2 * LANES), lambda b, hh: (b, hh)),
            blk(lambda b, hh: (b, 2 * hh)),
            blk(lambda b, hh: (b, EV_KPE_BLK)),
            blk(lambda b, hh: (b, 2 * hh + 1)),
            blk(lambda b, hh: (b, EV_QNA_BLK + hh)),
            blk(lambda b, hh: (b, EV_KNA_BLK + hh)),
            blk(lambda b, hh: (b, EV_VNA_BLK + hh)),
        ],
        out_specs=[blk(lambda b, hh: (b, hh)), blk(lambda b, hh: (b, hh))],
        out_shape=[out, out],
        compiler_params=_cparams(("arbitrary", "arbitrary")),
    )(p_ctx, kv_ctx, p_ctx, kv_ctx, p_ctx, p_ctx, p_ctx)


def _na_kernel(q_ref, k_ref, v_ref, kc_ref, vc_ref, bias_ref, o_ref, s_scr, p_scr, l_scr, oc_scr,
               *, scale, n_rows):
    win = NA_KH * GRID_W
    slab = 256
    n = q_ref.shape[0]

    def band(r):
        ws = min(max(r - NA_KH // 2, 0), n_rows - NA_KH)
        return ws, slice(r * GRID_W, (r + 1) * GRID_W), slice(ws * GRID_W, ws * GRID_W + win)

    s_scr[:, win:] = _dot_t(q_ref[...], kc_ref[...]) * scale
    for r in range(n_rows):
        ws, rows, keys = band(r)
        s_scr[rows, :win] = _dot_t(q_ref[rows, :], k_ref[keys, :]) * scale + bias_ref[r - ws]

    def body(i, carry):
        sl = pl.ds(pl.multiple_of(i * slab, slab), slab)
        s = s_scr[sl, :]
        p = jnp.exp2((s - jnp.max(s, axis=-1, keepdims=True)) * LOG2E)
        p_scr[sl, :] = p.astype(BF16)
        l_scr[sl, :] = jnp.broadcast_to(1.0 / jnp.sum(p, axis=-1, keepdims=True), (slab, LANES))
        return carry

    lax.fori_loop(0, n // slab, body, 0)
    oc_scr[...] = _dot(p_scr[:, win:], vc_ref[...])
    for r in range(n_rows):
        ws, rows, keys = band(r)
        o = _dot(p_scr[rows, :win], v_ref[keys, :]) + oc_scr[rows, :]
        o_ref[rows, :] = (o * l_scr[rows, :]).astype(o_ref.dtype)


def _na_bias_kernel(rb_ref, onehot_ref, mask_ref, o_ref):
    o_ref[...] = _dot_f32(rb_ref[...], onehot_ref[...]) + mask_ref[...]


def _na_bias_table(rel_bias):
    n_heads, n_ro, n_co = rel_bias.shape
    col = np.arange(GRID_W)
    col_start = np.clip(col - NA_KW // 2, 0, GRID_W - NA_KW)
    col_mask = (col[None, :] >= col_start[:, None]) & (col[None, :] < col_start[:, None] + NA_KW)
    col_off = np.clip(col[None, :] - col[:, None], 1 - NA_KW, NA_KW - 1) + (NA_KW - 1)
    onehot = (col_off.reshape(1, -1) == np.arange(n_co)[:, None]).astype(np.float32)
    mask_add = np.where(col_mask.reshape(1, -1), 0.0, NEG_INF).astype(np.float32)
    qw = GRID_W * GRID_W
    full = lambda shape: pl.BlockSpec(shape, lambda: (0,) * len(shape))
    cols = pl.pallas_call(
        _na_bias_kernel,
        in_specs=[full((n_heads * n_ro, n_co)), full((n_co, qw)), full((1, qw))],
        out_specs=full((n_heads * n_ro, qw)),
        out_shape=jax.ShapeDtypeStruct((n_heads * n_ro, qw), F32),
    )(rel_bias.reshape(n_heads * n_ro, n_co), jnp.asarray(onehot), jnp.asarray(mask_add))
    cols = cols.reshape(n_heads, n_ro, GRID_W, GRID_W)
    t = jnp.stack([cols[:, NA_KH - 1 - e:2 * NA_KH - 1 - e] for e in range(NA_KH)], axis=1)
    return t.transpose(0, 1, 3, 2, 4).reshape(n_heads, NA_KH, GRID_W, NA_KH * GRID_W)


def _na_latent(p_lat, p_ctx, bias_tab, batch):
    n = p_lat.shape[0] // batch
    lc = p_ctx.shape[0] // batch
    h = NA_HEADS
    n_rows = n // GRID_W
    assert n_rows >= NA_KH
    blk = lambda rows, f: pl.BlockSpec((rows, LANES), f)
    return pl.pallas_call(
        functools.partial(_na_kernel, scale=NA_DIM ** -0.5, n_rows=n_rows),
        grid=(batch, h),
        in_specs=[
            blk(n, lambda b, hh: (b, EV_QNA_BLK + hh)),
            blk(n, lambda b, hh: (b, EV_KNA_BLK + hh)),
            blk(n, lambda b, hh: (b, EV_VNA_BLK + hh)),
            blk(lc, lambda b, hh: (b, EV_KNA_BLK + hh)),
            blk(lc, lambda b, hh: (b, EV_VNA_BLK + hh)),
            pl.BlockSpec((None, NA_KH, GRID_W, NA_KH * GRID_W), lambda b, hh: (hh, 0, 0, 0)),
        ],
        out_specs=blk(n, lambda b, hh: (b, hh)),
        out_shape=jax.ShapeDtypeStruct((batch * n, h * LANES), BF16),
        scratch_shapes=[pltpu.VMEM((n, NA_KH * GRID_W + lc), F32), pltpu.VMEM((n, NA_KH * GRID_W + lc), BF16),
                        pltpu.VMEM((n, LANES), F32), pltpu.VMEM((n, LANES), F32)],
        compiler_params=_cparams(("arbitrary", "arbitrary")),
    )(p_lat, p_lat, p_lat, p_ctx, p_ctx, bias_tab)


def _rope_tables(n):
    pos = np.arange(n)
    rows, cols = pos // GRID_W, pos % GRID_W
    half = MLA_ROPE_DIM // 2
    inv_freq = ROPE_THETA ** (-np.arange(0, half, 2, dtype=np.float64) / half)
    cos = np.zeros((n, LANES), np.float64)
    sin = np.zeros((n, LANES), np.float64)
    for base, p in ((0, rows), (half, cols)):
        ang = p[:, None].astype(np.float64) * inv_freq[None, :]
        q = half // 2
        cos[:, base:base + q] = np.cos(ang)
        cos[:, base + q:base + half] = np.cos(ang)
        sin[:, base:base + q] = -np.sin(ang)
        sin[:, base + q:base + half] = np.sin(ang)
    return jnp.asarray(cos, F32), jnp.asarray(sin, F32)


def _even_w_in(w):
    d = w.shape[0]
    z64 = jnp.zeros((d, LANES - MLA_ROPE_DIM), w.dtype)
    pieces = []
    for h in range(MLA_HEADS):
        pieces += [w[:, h * MLA_QK_DIM:h * MLA_QK_DIM + MLA_NOPE_DIM],
                   w[:, h * MLA_QK_DIM + MLA_NOPE_DIM:(h + 1) * MLA_QK_DIM], z64]
    q_end = MLA_HEADS * MLA_QK_DIM
    ckv_end = q_end + MLA_KV_RANK
    kpe_end = ckv_end + MLA_ROPE_DIM
    pieces += [w[:, kpe_end:], w[:, q_end:ckv_end], w[:, ckv_end:kpe_end], z64]
    out = jnp.concatenate(pieces, axis=1)
    pad = EV_WIDTH - out.shape[1]
    return jnp.concatenate([out, jnp.zeros((d, pad), w.dtype)], axis=1).astype(BF16)


def _hgrn_gates(q_ref, z_refs, lbs, row0, qb_s, cum_s, cpk_s, qd_s, kd_s, dec_s):
    c = HGRN_CHUNK
    grp = LANES
    ri = lax.broadcasted_iota(jnp.int32, (grp, grp), 0)
    ci = lax.broadcasted_iota(jnp.int32, (grp, grp), 1)
    same = (ri // c) == (ci // c)
    blk = jnp.where(same, 1.0, 0.0).astype(BF16)
    tris = (jnp.where(jnp.logical_and(same, ci <= ri), 1.0, 0.0).astype(BF16),
            jnp.where(jnp.logical_and(same, ci >= ri), 1.0, 0.0).astype(BF16))

    def body(g, carry):
        src = pl.ds(pl.multiple_of(g * grp, grp), grp)
        dst = pl.ds(pl.multiple_of(row0 + g * grp, grp), grp)
        q = _silu(q_ref[src, :].astype(F32))
        qb_s[dst, :] = q.astype(BF16)
        for d in range(2):
            f = jnp.maximum(lbs[d] + (1.0 - lbs[d]) * _sigmoid(z_refs[d][src, :].astype(F32)), FORGET_FLOOR)
            lf = jnp.log(f)
            k = 1.0 - f
            cum = _dot_01(tris[d], lf)
            tot = _dot_01(blk, lf)
            cum_s[d, dst, :] = cum * LOG2E
            cpk_s[d, dst, :] = (cum - jnp.log(k)) * LOG2E
            qd_s[d, dst, :] = (q * jnp.exp(cum)).astype(BF16)
            kd_s[d, dst, :] = (k * jnp.exp(tot - cum)).astype(BF16)
            dec_s[d, dst, :] = jnp.exp(tot)
        return carry

    lax.fori_loop(0, q_ref.shape[0] // grp, body, 0)


def _hgrn_chunk(qb, cum2, cpk2, v, qd, st, ones_bf, reverse):
    c = HGRN_CHUNK
    hc = c // 2
    o = _dot_t(qd, st.astype(BF16))
    rows = lax.broadcasted_iota(jnp.int32, (hc, LANES), 0)
    halves = (cum2[:hc], cum2[hc:])
    zero = jnp.zeros((hc, LANES), F32)
    pieces = []
    for s in range(c):
        ref = cpk2[s:s + 1]
        hs, rs = divmod(s, hc)
        es = []
        for hh in range(2):
            if hh == hs:
                mask = (rows <= rs) if reverse else (rows >= rs)
                es.append(jnp.exp2(jnp.where(mask, halves[hh] - ref, NEG_INF)))
            elif (hh > hs) != reverse:
                es.append(jnp.exp2(halves[hh] - ref))
            else:
                es.append(zero)
        pieces.append(jnp.concatenate(es, axis=0).astype(BF16) * qb)
    lhs = jnp.concatenate([jnp.concatenate(pieces[:hc], axis=0), jnp.concatenate(pieces[hc:], axis=0)], axis=1)
    r = _dot(lhs, ones_bf)
    for s in range(c):
        hs, rs = divmod(s, hc)
        o = o + r[rs * c:(rs + 1) * c, hs * LANES:(hs + 1) * LANES] * v[s:s + 1]
    return o


def _hgrn_gates_fast(q_ref, i_ref, z_refs, lbs, row0, worst, v_s, qf_s, kf_s, qdf_s, kdf_s, decf_s):
    c = HGRN_FAST_CHUNK
    grp = LANES
    ri = lax.broadcasted_iota(jnp.int32, (grp, grp), 0)
    ci = lax.broadcasted_iota(jnp.int32, (grp, grp), 1)
    same = (ri // c) == (ci // c)
    blk = jnp.where(same, 1.0, 0.0).astype(BF16)
    tris = (jnp.where(jnp.logical_and(same, ci <= ri), 1.0, 0.0).astype(BF16),
            jnp.where(jnp.logical_and(same, ci >= ri), 1.0, 0.0).astype(BF16))
    mid = jnp.where(jnp.logical_and(same, (ci % c) == c // 2), 1.0, 0.0).astype(BF16)

    def body(g, worst):
        src = pl.ds(pl.multiple_of(g * grp, grp), grp)
        dst = pl.ds(pl.multiple_of(row0 + g * grp, grp), grp)
        q = _silu(q_ref[src, :].astype(F32))
        v_s[dst, :] = i_ref[src, :].astype(F32)
        for d in range(2):
            f = jnp.maximum(lbs[d] + (1.0 - lbs[d]) * _sigmoid(z_refs[d][src, :].astype(F32)), FORGET_FLOOR)
            lf = jnp.log(f)
            k = 1.0 - f
            cum = _dot_01(tris[d], lf)
            tot = _dot_01(blk, lf)
            ref = _dot_01(mid, cum)
            qf_s[d, dst, :] = (q * jnp.exp(cum - ref)).astype(BF16)
            kf_s[d, dst, :] = (k * jnp.exp(ref - cum)).astype(BF16)
            qdf_s[d, dst, :] = (q * jnp.exp(cum)).astype(BF16)
            kdf_s[d, dst, :] = (k * jnp.exp(tot - cum)).astype(BF16)
            decf_s[d, dst, :] = jnp.exp(tot)
            worst = jnp.maximum(worst, -tot)
        return worst

    return lax.fori_loop(0, q_ref.shape[0] // grp, body, worst)


def _hgrn_kernel(ql_ref, il_ref, zfl_ref, zbl_ref, gl_ref, qc_ref, ic_ref, zfc_ref, zbc_ref, gc_ref,
                 lbl_ref, ng_ref, yl_ref, yc_ref,
                 qb_s, v_s, cum_s, cpk_s, qd_s, kd_s, dec_s, o_s, qf_s, kf_s, qdf_s, kdf_s, decf_s, *, layer):
    lc, n = qc_ref.shape[0], ql_ref.shape[0]
    lbs = []
    for d in range(2):
        lg = lbl_ref[d]
        ex = jnp.exp(lg - jnp.max(lg, axis=0, keepdims=True))
        p = ex / jnp.sum(ex, axis=0, keepdims=True)
        lbs.append(jnp.sum(p[:layer + 1], axis=0, keepdims=True) - p[0:1])

    zero = jnp.zeros((LANES, LANES), F32)

    def chunk_rows(j, c):
        r_fwd = j * c
        r_bwd = jnp.where(j < lc // c, lc - c - j * c, 2 * lc + n - c - j * c)
        return [pl.ds(pl.multiple_of(r, c), c) for r in (r_fwd, r_bwd)]

    fast_scr = (v_s, qf_s, kf_s, qdf_s, kdf_s, decf_s)
    worst = _hgrn_gates_fast(qc_ref, ic_ref, (zfc_ref, zbc_ref), lbs, 0, zero, *fast_scr)
    worst = _hgrn_gates_fast(ql_ref, il_ref, (zfl_ref, zbl_ref), lbs, lc, worst, *fast_scr)
    fast_ok = jnp.max(worst) <= HGRN_FAST_MAX_DECAY

    @pl.when(fast_ok)
    def _():
        c = HGRN_FAST_CHUNK
        ri = lax.broadcasted_iota(jnp.int32, (c, c), 0)
        ci = lax.broadcasted_iota(jnp.int32, (c, c), 1)
        causal = (ci <= ri, ci >= ri)

        def body(g, sts):
            slices = [chunk_rows(g * HGRN_FAST_UNROLL + i, c) for i in range(HGRN_FAST_UNROLL)]
            scores = [[jnp.where(causal[d], _dot_t(qf_s[d, sl[d], :], kf_s[d, sl[d], :]), 0.0).astype(BF16)
                       for d in range(2)] for sl in slices]
            incs = [[_dot_tn(v_s[sl[d], :].astype(BF16), kdf_s[d, sl[d], :]) for d in range(2)] for sl in slices]
            local = [[_dot(scores[i][d], v_s[sl[d], :].astype(BF16)) for d in range(2)]
                     for i, sl in enumerate(slices)]
            sts = list(sts)
            for i, sl2 in enumerate(slices):
                for d in range(2):
                    sl = sl2[d]
                    o_s[d, sl, :] = _dot_t(qdf_s[d, sl, :], sts[d].astype(BF16)) + local[i][d]
                    sts[d] = sts[d] * decf_s[d, sl, :][0:1] + incs[i][d]
            return tuple(sts)

        n_chunks = (lc + n) // c
        assert n_chunks % HGRN_FAST_UNROLL == 0
        lax.fori_loop(0, n_chunks // HGRN_FAST_UNROLL, body, (zero, zero))

    @pl.when(jnp.logical_not(fast_ok))
    def _():
        c = HGRN_CHUNK
        scr = (qb_s, cum_s, cpk_s, qd_s, kd_s, dec_s)
        _hgrn_gates(qc_ref, (zfc_ref, zbc_ref), lbs, 0, *scr)
        _hgrn_gates(ql_ref, (zfl_ref, zbl_ref), lbs, lc, *scr)
        ri = lax.broadcasted_iota(jnp.int32, (2 * LANES, 2 * LANES), 0)
        ci = lax.broadcasted_iota(jnp.int32, (2 * LANES, 2 * LANES), 1)
        ones_bf = jnp.where((ri // LANES) == (ci // LANES), 1.0, 0.0).astype(BF16)

        def body(g, sts):
            slices = [chunk_rows(g * HGRN_UNROLL + i, c) for i in range(HGRN_UNROLL)]
            incs = [[_dot_tn(v_s[sl[d], :].astype(BF16), kd_s[d, sl[d], :]) for d in range(2)] for sl in slices]
            sts = list(sts)
            for i, sl2 in enumerate(slices):
                for d in range(2):
                    sl = sl2[d]
                    o_s[d, sl, :] = _hgrn_chunk(qb_s[sl, :], cum_s[d, sl, :], cpk_s[d, sl, :], v_s[sl, :],
                                                qd_s[d, sl, :], sts[d], ones_bf, reverse=d == 1)
                    sts[d] = sts[d] * dec_s[d, sl, :][0:1] + incs[i][d]
            return tuple(sts)

        n_chunks = (lc + n) // c
        assert n_chunks % HGRN_UNROLL == 0
        lax.fori_loop(0, n_chunks // HGRN_UNROLL, body, (zero, zero))

    ng = ng_ref[...]
    yc_ref[...] = (_rms(o_s[0, :lc, :] + o_s[1, :lc, :]) * ng
                   * _silu(gc_ref[...].astype(F32))).astype(yc_ref.dtype)
    yl_ref[...] = (_rms(o_s[0, lc:, :] + o_s[1, lc:, :]) * ng
                   * _silu(gl_ref[...].astype(F32))).astype(yl_ref.dtype)


def _hgrn2(p_lat, p_ctx, lb_logits, norm_g, layer, batch):
    n = p_lat.shape[0] // batch
    lc = p_ctx.shape[0] // batch
    h = HGRN_HEADS
    n_layers = lb_logits.shape[1]
    rows = lc + n
    assert lc % LANES == 0 and n % LANES == 0
    lat = lambda part: pl.BlockSpec((n, LANES), lambda b, hh: (b, part * h + hh))
    ctx = lambda part: pl.BlockSpec((lc, LANES), lambda b, hh: (b, part * h + hh))
    return pl.pallas_call(
        functools.partial(_hgrn_kernel, layer=layer),
        grid=(batch, h),
        in_specs=[lat(0), lat(1), lat(2), lat(3), lat(4), ctx(0), ctx(1), ctx(2), ctx(3), ctx(4),
                  pl.BlockSpec((2, n_layers, LANES), lambda b, hh: (0, 0, hh)),
                  pl.BlockSpec((1, LANES), lambda b, hh: (0, hh))],
        out_specs=[pl.BlockSpec((n, LANES), lambda b, hh: (b, hh)),
                   pl.BlockSpec((lc, LANES), lambda b, hh: (b, hh))],
        out_shape=[jax.ShapeDtypeStruct((batch * n, h * LANES), BF16),
                   jax.ShapeDtypeStruct((batch * lc, h * LANES), BF16)],
        scratch_shapes=[pltpu.VMEM((rows, LANES), BF16), pltpu.VMEM((rows, LANES), F32),
                        pltpu.VMEM((2, rows, LANES), F32), pltpu.VMEM((2, rows, LANES), F32),
                        pltpu.VMEM((2, rows, LANES), BF16), pltpu.VMEM((2, rows, LANES), BF16),
                        pltpu.VMEM((2, rows, LANES), F32), pltpu.VMEM((2, rows, LANES), F32),
                        pltpu.VMEM((2, rows, LANES), BF16), pltpu.VMEM((2, rows, LANES), BF16),
                        pltpu.VMEM((2, rows, LANES), BF16), pltpu.VMEM((2, rows, LANES), BF16),
                        pltpu.VMEM((2, rows, LANES), F32)],
        compiler_params=_cparams(("arbitrary", "arbitrary")),
    )(p_lat, p_lat, p_lat, p_lat, p_lat, p_ctx, p_ctx, p_ctx, p_ctx, p_ctx,
      lb_logits, norm_g[layer].reshape(1, -1))


def _dft_matrices(n):
    idx = (np.arange(n)[:, None] * np.arange(n)[None, :]) % (2 * n)
    ang = idx.astype(np.float64) * (math.pi / n)
    cm = np.cos(ang)
    sf = np.sin(ang)
    sf[0, :] = (-1.0) ** np.arange(n)
    return (jnp.asarray(cm, F32).astype(BF16), jnp.asarray(sf, F32).astype(BF16),
            jnp.asarray(sf.T, F32).astype(BF16))


def _filter_features(n):
    pos = np.arange(n, dtype=np.float64)
    t = pos / max(n - 1, 1)
    bands = np.linspace(1e-4, HYENA_BANDS - 1, HYENA_BANDS)
    ang = (2.0 * math.pi / n) * pos[:, None] * bands[None, :]
    z = np.concatenate([t[:, None], np.cos(ang), -np.sin(ang)], -1)
    max_decay = math.log(HYENA_DECAY_TARGET) / HYENA_FAST_PCT
    min_decay = math.log(HYENA_DECAY_TARGET) / HYENA_SLOW_PCT
    deltas = np.abs(np.linspace(min_decay, max_decay, HYENA_WIDTH))
    return jnp.asarray(z, F32), jnp.asarray(t[:, None], F32), jnp.asarray(deltas[None, :], F32)


def _filter_kernel(z_ref, t_ref, dl_ref, w1_ref, b1_ref, w2_ref, b2_ref, w3_ref, b3_ref, fr_ref, wo_ref,
                   o_ref, hdn_ref):
    j = pl.program_id(0)

    @pl.when(j == 0)
    def _():
        fr = fr_ref[...]
        hdn = jnp.sin(fr * (_dot_f32(z_ref[...], w1_ref[...]) + b1_ref[...]))
        hdn = jnp.sin(fr * (_dot_f32(hdn, w2_ref[...]) + b2_ref[...]))
        hdn_ref[...] = jnp.sin(fr * (_dot_f32(hdn, w3_ref[...]) + b3_ref[...]))

    filt = _dot_f32(hdn_ref[...], wo_ref[...]) * jnp.exp(-t_ref[...] * dl_ref[...])
    row = lax.broadcasted_iota(jnp.int32, filt.shape, 0)
    is_bwd = j >= pl.num_programs(0) // 2
    o_ref[...] = jnp.where(jnp.logical_and(is_bwd, row == 0), 0.0, filt).astype(o_ref.dtype)


def _hyena_filters(n, w1, b1, w2, b2, w3, b3, freq, w_out, tc=512):
    z, t, deltas = _filter_features(n)
    hid = HYENA_FILT_HIDDEN
    nct = HYENA_WIDTH // tc
    full = lambda shape: pl.BlockSpec(shape, lambda j: (0,) * len(shape))
    return pl.pallas_call(
        _filter_kernel,
        grid=(2 * nct,),
        in_specs=[full((n, HYENA_EMB)), full((n, 1)),
                  pl.BlockSpec((1, tc), lambda j: (0, j % nct)),
                  full((HYENA_EMB, hid)), full((1, hid)), full((hid, hid)), full((1, hid)),
                  full((hid, hid)), full((1, hid)), full((1, hid)),
                  pl.BlockSpec((hid, tc), lambda j: (0, j))],
        out_specs=pl.BlockSpec((n, tc), lambda j: (0, j)),
        out_shape=jax.ShapeDtypeStruct((n, 2 * HYENA_WIDTH), BF16),
        scratch_shapes=[pltpu.VMEM((n, hid), F32)],
        compiler_params=_cparams(("arbitrary",)),
    )(z, t, deltas, w1, b1.reshape(1, hid), w2, b2.reshape(1, hid), w3, b3.reshape(1, hid),
      freq.reshape(1, hid), w_out)


def _spectrum_kernel(cm_ref, sf_ref, hf_ref, hb_ref, a_ref, b_ref, *, inv_len):
    cm, sf, hf, hb = cm_ref[...], sf_ref[...], hf_ref[...], hb_ref[...]
    kr = _dot(cm, hf) + _dot(cm, hb)
    d1 = _dot(sf, hf)
    d2 = _dot(sf, hb)
    row = lax.broadcasted_iota(jnp.int32, kr.shape, 0) + pl.program_id(0) * kr.shape[0]
    first = row == 0
    w = jnp.where(first, inv_len, 2.0 * inv_len)
    a_ref[...] = kr * w
    b_ref[...] = jnp.where(first, d1 + d2, d1 - d2) * w


def _filter_spectrum(filt, cm, sf, tk=512, tc=512):
    n = filt.shape[0]
    tk = min(tk, n)
    nct = HYENA_WIDTH // tc
    out = jax.ShapeDtypeStruct((n, HYENA_WIDTH), F32)
    return pl.pallas_call(
        functools.partial(_spectrum_kernel, inv_len=1.0 / (2 * n)),
        grid=(n // tk, nct),
        in_specs=[pl.BlockSpec((tk, n), lambda i, j: (i, 0)),
                  pl.BlockSpec((tk, n), lambda i, j: (i, 0)),
                  pl.BlockSpec((n, tc), lambda i, j: (0, j)),
                  pl.BlockSpec((n, tc), lambda i, j: (0, nct + j))],
        out_specs=[pl.BlockSpec((tk, tc), lambda i, j: (i, j)), pl.BlockSpec((tk, tc), lambda i, j: (i, j))],
        out_shape=[out, out],
        compiler_params=_cparams(("arbitrary", "arbitrary")),
    )(cm, sf, filt, filt)


def _hyena_gate_kernel(u0_ref, u1_ref, uv_ref, w0_ref, w1_ref, wv_ref, b0_ref, b1_ref, bv_ref,
                       x0_ref, z_ref):
    n = u0_ref.shape[0]
    row = lax.broadcasted_iota(jnp.int32, u0_ref.shape, 0)

    def conv(u_ref, w_ref, b_ref):
        u = u_ref[...].astype(F32)
        prev = jnp.where(row == 0, 0.0, pltpu.roll(u, 1, 0))
        nxt = jnp.where(row == n - 1, 0.0, pltpu.roll(u, n - 1, 0))
        return b_ref[...] + prev * w_ref[0:1] + u * w_ref[1:2] + nxt * w_ref[2:3]

    x0_ref[...] = conv(u0_ref, w0_ref, b0_ref)
    z_ref[...] = conv(uv_ref, wv_ref, bv_ref) * conv(u1_ref, w1_ref, b1_ref)


def _hyena_gate(p, first_blk, conv_w, conv_b, batch, tc=256):
    n = p.shape[0] // batch
    nct = HYENA_WIDTH // tc
    c0 = first_blk * LANES // tc
    u = lambda part: pl.BlockSpec((n, tc), lambda b, j: (b, c0 + part * nct + j))
    w = lambda part: pl.BlockSpec((HYENA_SHORT, tc), lambda b, j: (0, part * nct + j))
    bb = lambda part: pl.BlockSpec((1, tc), lambda b, j: (0, part * nct + j))
    out = jax.ShapeDtypeStruct((batch * n, HYENA_WIDTH), F32)
    cb = conv_b.reshape(1, -1)
    return pl.pallas_call(
        _hyena_gate_kernel,
        grid=(batch, nct),
        in_specs=[u(0), u(1), u(2), w(0), w(1), w(2), bb(0), bb(1), bb(2)],
        out_specs=[pl.BlockSpec((n, tc), lambda b, j: (b, j)), pl.BlockSpec((n, tc), lambda b, j: (b, j))],
        out_shape=[out, out],
        compiler_params=_cparams(("arbitrary", "arbitrary")),
    )(p, p, p, conv_w, conv_w, conv_w, cb, cb, cb)


def _dft_fwd_kernel(cm_ref, sf_ref, z_ref, a_ref, b_ref, pr_ref, ps_ref):
    z = z_ref[...].astype(BF16)
    zr = _dot(cm_ref[...], z)
    zs = _dot(sf_ref[...], z)
    a, b = a_ref[...], b_ref[...]
    row = lax.broadcasted_iota(jnp.int32, zr.shape, 0) + pl.program_id(1) * zr.shape[0]
    first = row == 0
    pr_ref[...] = (zr * a - jnp.where(first, 0.0, zs * b)).astype(pr_ref.dtype)
    ps_ref[...] = (jnp.where(first, 0.0, zr * b) + zs * jnp.where(first, b, a)).astype(ps_ref.dtype)


def _dft_inv_kernel(cm_ref, si_ref, pr_ref, ps_ref, z_ref, x0_ref, skip_ref, o_ref):
    y = _dot(cm_ref[...], pr_ref[...]) + _dot(si_ref[...], ps_ref[...])
    o_ref[...] = (x0_ref[...] * (y + z_ref[...] * skip_ref[...])).astype(o_ref.dtype)


def _long_conv(z, x0, spec_a, spec_b, skip, cm, sf, si, batch, tk=1024, tc=512):
    n = z.shape[0] // batch
    tk = min(tk, n)
    nk = n // tk
    nct = HYENA_WIDTH // tc
    mat = pl.BlockSpec((tk, n), lambda b, i, j: (i, 0))
    col = pl.BlockSpec((n, tc), lambda b, i, j: (b, j))
    tile_nb = pl.BlockSpec((tk, tc), lambda b, i, j: (i, j))
    tile = pl.BlockSpec((tk, tc), lambda b, i, j: (b * nk + i, j))
    spec_shape = jax.ShapeDtypeStruct((batch * n, HYENA_WIDTH), BF16)
    pr, ps = pl.pallas_call(
        _dft_fwd_kernel,
        grid=(batch, nk, nct),
        in_specs=[mat, mat, col, tile_nb, tile_nb],
        out_specs=[tile, tile],
        out_shape=[spec_shape, spec_shape],
        compiler_params=_cparams(("arbitrary", "arbitrary", "arbitrary")),
    )(cm, sf, z, spec_a, spec_b)
    return pl.pallas_call(
        _dft_inv_kernel,
        grid=(batch, nk, nct),
        in_specs=[mat, mat, col, col, tile, tile, pl.BlockSpec((1, tc), lambda b, i, j: (0, j))],
        out_specs=tile,
        out_shape=jax.ShapeDtypeStruct((batch * n, HYENA_WIDTH), BF16),
        compiler_params=_cparams(("arbitrary", "arbitrary", "arbitrary")),
    )(cm, si, pr, ps, z, x0, skip.reshape(1, -1))


def _hyena(p, first_blk, conv_w, conv_b, filt_params, skip, batch):
    n = p.shape[0] // batch
    cm, sf, si = _dft_matrices(n)
    filt = _hyena_filters(n, *filt_params)
    spec_a, spec_b = _filter_spectrum(filt, cm, sf)
    x0, z = _hyena_gate(p, first_blk, conv_w, conv_b, batch)
    return _long_conv(z, x0, spec_a, spec_b, skip, cm, sf, si, batch)


def kernel(x, c, ctx, c_ctx, ada_w, ada_b, norm_mix_g, norm_mlp_g, w_out, mlp_w1, mlp_w2, final_norm_g, ev_w_in, mla_kv_norm_g, mla_w_ukv, na_rel_bias, od_w_in, hgrn_lb_logits, hgrn_norm_g, hy_conv_w, hy_conv_b, hy_filt_w1, hy_filt_b1, hy_filt_w2, hy_filt_b2, hy_filt_w3, hy_filt_b3, hy_filt_freq, hy_filt_wout, hy_skip):
    batch, seq, d = x.shape
    lc = ctx.shape[1]
    depth = ada_w.shape[0]
    h_lat = x.reshape(batch * seq, d)
    h_ctx = ctx.reshape(batch * lc, d)

    cond = jnp.concatenate([c, c_ctx[None, :], jnp.zeros((8 - batch - 1, d), F32)], axis=0)
    mod_all = _ada_modulation(cond, ada_w, ada_b)
    cos_tab, sin_tab = _rope_tables(seq)
    hgrn_cols = 5 * HGRN_WIDTH
    od_w_in, w_out, mlp_w1, mlp_w2, mla_w_ukv = (
        t.astype(BF16) for t in (od_w_in, w_out, mlp_w1, mlp_w2, mla_w_ukv))

    for l in range(depth):
        ctx_out = l < depth - 1
        mod3 = mod_all[l].reshape(8, 1, 6 * d)
        lat_mod = lambda *chunks: (mod3, *chunks, seq, 0)
        ctx_mod = lambda *chunks: (mod3, *chunks, batch * lc, batch)
        if l % 2 == 0:
            e = l // 2
            w_in = _even_w_in(ev_w_in[e])
            p_lat = _norm_proj(h_lat, 0, d, norm_mix_g[l], w_in, BF16, lat_mod(0, 1))
            p_ctx = _norm_proj(h_ctx, 0, d, norm_mix_g[l], w_in, BF16, ctx_mod(0, 1))
            ckv_blk = EV_CKV_BLK * LANES // MLA_KV_RANK
            kv_lat = _norm_proj(p_lat, ckv_blk, MLA_KV_RANK, mla_kv_norm_g[e], mla_w_ukv, BF16, layer=e)
            kv_ctx = _norm_proj(p_ctx, ckv_blk, MLA_KV_RANK, mla_kv_norm_g[e], mla_w_ukv, BF16, layer=e)
            y1_lat = _mla_latent(p_lat, p_ctx, kv_lat, kv_ctx, cos_tab, sin_tab, batch)
            y2_lat = _na_latent(p_lat, p_ctx, _na_bias_table(na_rel_bias[e]), batch)
            if ctx_out:
                y1_ctx, y2_ctx = _ctx_attention(p_ctx, kv_ctx, batch)
        else:
            o = l // 2
            p_lat = _norm_proj(h_lat, 0, d, norm_mix_g[l], od_w_in, BF16, lat_mod(0, 1), layer=o)
            p_ctx = _norm_proj(h_ctx, 0, d, norm_mix_g[l], od_w_in, BF16, ctx_mod(0, 1), layer=o,
                               n=None if ctx_out else hgrn_cols)
            y1_lat, y1_ctx = _hgrn2(p_lat, p_ctx, hgrn_lb_logits, hgrn_norm_g, o, batch)
            filt_params = (hy_filt_w1[o], hy_filt_b1[o], hy_filt_w2[o], hy_filt_b2[o], hy_filt_w3[o],
                           hy_filt_b3[o], hy_filt_freq[o], hy_filt_wout[o])
            y2_lat = _hyena(p_lat, hgrn_cols // LANES, hy_conv_w[o], hy_conv_b[o], filt_params,
                            hy_skip[o], batch)
            if ctx_out:
                y2_ctx = _hyena(p_ctx, hgrn_cols // LANES, hy_conv_w[o], hy_conv_b[o], filt_params,
                                hy_skip[o], batch)
        h_lat = _out_proj(y1_lat, y2_lat, w_out, l, h_lat, lat_mod(2))
        h_lat = _mlp(h_lat, norm_mlp_g[l], mlp_w1, mlp_w2, l, lat_mod(3, 4, 5),
                     final_g=None if ctx_out else final_norm_g)
        if ctx_out:
            h_ctx = _out_proj(y1_ctx, y2_ctx, w_out, l, h_ctx, ctx_mod(2))
            h_ctx = _mlp(h_ctx, norm_mlp_g[l], mlp_w1, mlp_w2, l, ctx_mod(3, 4, 5))
    return h_lat.reshape(batch, seq, d)
```

```python
import functools
import math

import numpy as np
import jax
import jax.numpy as jnp
from jax import lax
from jax.experimental import pallas as pl
from jax.experimental.pallas import tpu as pltpu

F32 = jnp.float32
BF16 = jnp.bfloat16

D_MODEL = 2048
DEPTH = 4
GRID_W = 64
HEAD_DIM = 128
MLA_HEADS = 8
MLA_NOPE_DIM = 128
MLA_ROPE_DIM = 64
MLA_QK_DIM = MLA_NOPE_DIM + MLA_ROPE_DIM
MLA_KV_RANK = 512
NA_HEADS = 8
NA_DIM = 128
NA_KH = 8
NA_KW = 16
ROPE_THETA = 10000.0
HGRN_WIDTH = 1024
HGRN_HEADS = 8
FORGET_FLOOR = 1e-30
HYENA_WIDTH = 1024
HYENA_SHORT = 3
HYENA_EMB = 33
HYENA_BANDS = (HYENA_EMB - 1) // 2
HYENA_FILT_HIDDEN = 64
HYENA_DECAY_TARGET = 1e-2
HYENA_FAST_PCT = 0.3
HYENA_SLOW_PCT = 1.5
MLP_HIDDEN = 4 * D_MODEL
NORM_EPS = 1e-6
NEG_INF = -1e30
LOG2E = 1.4426950408889634

LANES = 128
VMEM_LIMIT_BYTES = 56 * 1024 * 1024

EV_QMLA_BLK = 0
EV_QNA_BLK = 16
EV_KNA_BLK = 24
EV_VNA_BLK = 32
EV_CKV_BLK = 40
EV_KPE_BLK = 44
EV_WIDTH = 48 * LANES
MLA_SUB_ROWS = 512
HGRN_CHUNK = 16
HGRN_UNROLL = 8
HGRN_FAST_CHUNK = 32
HGRN_FAST_UNROLL = 8
HGRN_FAST_MAX_DECAY = 80.0


def _cparams(sem):
    return pltpu.CompilerParams(dimension_semantics=sem, vmem_limit_bytes=VMEM_LIMIT_BYTES)


def _dot(a, b):
    return jnp.dot(a, b, preferred_element_type=F32)


def _dot_t(a, b):
    return lax.dot_general(a, b, (((1,), (1,)), ((), ())), preferred_element_type=F32)


def _dot_tn(a, b):
    return lax.dot_general(a, b, (((0,), (0,)), ((), ())), preferred_element_type=F32)


def _dot_f32(a, b):
    return jnp.dot(a, b, preferred_element_type=F32, precision=lax.Precision.HIGHEST)


def _dot_01(m01, x):
    hi = x.astype(BF16)
    lo = (x - hi.astype(F32)).astype(BF16)
    return _dot(m01, hi) + _dot(m01, lo)


def _sigmoid(x):
    return 1.0 / (1.0 + jnp.exp(-x))


def _silu(x):
    return x * _sigmoid(x)


def _rms(x):
    return x * lax.rsqrt(jnp.mean(x * x, axis=-1, keepdims=True) + NORM_EPS)


def _ada_kernel(s_ref, w_ref, b_ref, o_ref):
    s = _silu(s_ref[...]).astype(BF16)
    o_ref[...] = _dot(s, w_ref[...].astype(BF16)) + b_ref[...]


def _ada_modulation(cond, ada_w, ada_b, tn=1024):
    depth, d, n = ada_w.shape
    rows = cond.shape[0]
    return pl.pallas_call(
        _ada_kernel,
        grid=(depth, n // tn),
        in_specs=[
            pl.BlockSpec((rows, d), lambda l, j: (0, 0)),
            pl.BlockSpec((None, d, tn), lambda l, j: (l, 0, j)),
            pl.BlockSpec((None, 1, tn), lambda l, j: (l, 0, j)),
        ],
        out_specs=pl.BlockSpec((None, rows, tn), lambda l, j: (l, 0, j)),
        out_shape=jax.ShapeDtypeStruct((depth, rows, n), F32),
        compiler_params=_cparams(("arbitrary", "arbitrary")),
    )(cond, ada_w, ada_b.reshape(depth, 1, n))


def _mod_spec(chunk, tiles_per_group, group0, d):
    return pl.BlockSpec((None, 1, d), lambda i, *_: (group0 + i // tiles_per_group, 0, chunk))


def _proj_kernel(*refs, modulated):
    if modulated:
        x_ref, g_ref, sh_ref, sc_ref, w_ref, o_ref, a_ref = refs
    else:
        x_ref, g_ref, w_ref, o_ref, a_ref = refs

    @pl.when(pl.program_id(1) == 0)
    def _():
        y = _rms(x_ref[...].astype(F32)) * g_ref[...]
        if modulated:
            y = y * (1.0 + sc_ref[...]) + sh_ref[...]
        a_ref[...] = y.astype(BF16)

    o_ref[...] = _dot(a_ref[...], w_ref[...]).astype(o_ref.dtype)


def _layer_spec(w, layer, block, index_map):
    if w.ndim == 2:
        return pl.BlockSpec(block, index_map)
    return pl.BlockSpec((None,) + block, lambda *idx: (layer,) + index_map(*idx))


def _norm_proj(x, x_col_blk, k, g, w, out_dtype, mod=None, layer=None, n=None, tm=1024, tn=1024):
    m = x.shape[0]
    n = w.shape[-1] if n is None else n
    tm = min(tm, m)
    tn = min(tn, n)
    assert m % tm == 0 and n % tn == 0
    in_specs = [pl.BlockSpec((tm, k), lambda i, j: (i, x_col_blk)),
                pl.BlockSpec((1, k), lambda i, j: (0, 0))]
    args = [x, g.reshape(1, k)]
    if mod is not None:
        mod3, sh_chunk, sc_chunk, rows_per_group, group0 = mod
        assert rows_per_group % tm == 0
        in_specs += [_mod_spec(sh_chunk, rows_per_group // tm, group0, k),
                     _mod_spec(sc_chunk, rows_per_group // tm, group0, k)]
        args += [mod3, mod3]
    in_specs.append(_layer_spec(w, layer, (k, tn), lambda i, j: (0, j)))
    args.append(w)
    return pl.pallas_call(
        functools.partial(_proj_kernel, modulated=mod is not None),
        grid=(m // tm, n // tn),
        in_specs=in_specs,
        out_specs=pl.BlockSpec((tm, tn), lambda i, j: (i, j)),
        out_shape=jax.ShapeDtypeStruct((m, n), out_dtype),
        scratch_shapes=[pltpu.VMEM((tm, k), BF16)],
        compiler_params=_cparams(("arbitrary", "arbitrary")),
    )(*args)


def _outproj_kernel(y1_ref, y2_ref, w_ref, h_ref, gate_ref, o_ref):
    k1 = y1_ref.shape[1]
    acc = _dot(y1_ref[...], w_ref[:k1, :]) + _dot(y2_ref[...], w_ref[k1:, :])
    o_ref[...] = h_ref[...] + gate_ref[...] * acc


def _out_proj(y1, y2, w, layer, h, mod, tm=512):
    m, d = h.shape
    k1, k2 = y1.shape[1], y2.shape[1]
    tm = min(tm, m)
    mod3, gate_chunk, rows_per_group, group0 = mod
    return pl.pallas_call(
        _outproj_kernel,
        grid=(m // tm,),
        in_specs=[
            pl.BlockSpec((tm, k1), lambda i: (i, 0)),
            pl.BlockSpec((tm, k2), lambda i: (i, 0)),
            _layer_spec(w, layer, (k1 + k2, d), lambda i: (0, 0)),
            pl.BlockSpec((tm, d), lambda i: (i, 0)),
            _mod_spec(gate_chunk, rows_per_group // tm, group0, d),
        ],
        out_specs=pl.BlockSpec((tm, d), lambda i: (i, 0)),
        out_shape=jax.ShapeDtypeStruct((m, d), F32),
        compiler_params=_cparams(("arbitrary",)),
    )(y1, y2, w, h, mod3)


def _mlp_kernel(*refs, final_norm):
    if final_norm:
        h_ref, g_ref, sh_ref, sc_ref, gate_ref, w1_ref, w2_ref, fg_ref, o_ref, a_ref, acc_ref = refs
    else:
        h_ref, g_ref, sh_ref, sc_ref, gate_ref, w1_ref, w2_ref, o_ref, a_ref, acc_ref = refs
    k = pl.program_id(1)

    @pl.when(k == 0)
    def _():
        y = _rms(h_ref[...]) * g_ref[...]
        a_ref[...] = (y * (1.0 + sc_ref[...]) + sh_ref[...]).astype(BF16)
        acc_ref[...] = jnp.zeros_like(acc_ref)

    u = jnp.maximum(_dot(a_ref[...], w1_ref[...]), 0.0)
    acc_ref[...] += _dot((u * u).astype(BF16), w2_ref[...])

    @pl.when(k == pl.num_programs(1) - 1)
    def _():
        out = h_ref[...] + gate_ref[...] * acc_ref[...]
        if final_norm:
            out = _rms(out) * fg_ref[...]
        o_ref[...] = out


def _mlp(h, g, w1, w2, layer, mod, final_g=None, tm=512, th=1024):
    m, d = h.shape
    hid = w1.shape[-1]
    tm = min(tm, m)
    mod3, sh_chunk, sc_chunk, gate_chunk, rows_per_group, group0 = mod
    tpg = rows_per_group // tm
    in_specs = [
        pl.BlockSpec((tm, d), lambda i, k: (i, 0)),
        pl.BlockSpec((1, d), lambda i, k: (0, 0)),
        _mod_spec(sh_chunk, tpg, group0, d),
        _mod_spec(sc_chunk, tpg, group0, d),
        _mod_spec(gate_chunk, tpg, group0, d),
        _layer_spec(w1, layer, (d, th), lambda i, k: (0, k)),
        _layer_spec(w2, layer, (th, d), lambda i, k: (k, 0)),
    ]
    args = [h, g.reshape(1, d), mod3, mod3, mod3, w1, w2]
    if final_g is not None:
        in_specs.append(pl.BlockSpec((1, d), lambda i, k: (0, 0)))
        args.append(final_g.reshape(1, d))
    return pl.pallas_call(
        functools.partial(_mlp_kernel, final_norm=final_g is not None),
        grid=(m // tm, hid // th),
        in_specs=in_specs,
        out_specs=pl.BlockSpec((tm, d), lambda i, k: (i, 0)),
        out_shape=jax.ShapeDtypeStruct((m, d), F32),
        scratch_shapes=[pltpu.VMEM((tm, d), BF16), pltpu.VMEM((tm, d), F32)],
        compiler_params=_cparams(("arbitrary", "arbitrary")),
    )(*args)


def _softmax_pv(scores, values, scale=1.0):
    m = functools.reduce(jnp.maximum, [jnp.max(s, axis=-1, keepdims=True) for s in scores])
    ps = [jnp.exp2((s - m) * (scale * LOG2E)) for s in scores]
    denom = functools.reduce(jnp.add, [jnp.sum(p, axis=-1, keepdims=True) for p in ps])
    o = functools.reduce(jnp.add, [_dot(p.astype(BF16), v) for p, v in zip(ps, values)])
    return o / denom


def _rope_rotate(x, cos, sin):
    x = x.astype(F32)
    lane = lax.broadcasted_iota(jnp.int32, x.shape, 1)
    partner = jnp.where((lane % 32) < 16, pltpu.roll(x, LANES - 16, 1), pltpu.roll(x, 16, 1))
    return x * cos + partner * sin


def _mla_lat_kernel(qn_ref, qpe_ref, cosq_ref, sinq_ref, knc_ref, kpec_ref, vc_ref, knl_ref, kpel_ref,
                    cosk_ref, sink_ref, vl_ref, o_ref, k_scr, v_scr, *, scale):
    lc = knc_ref.shape[0]

    @pl.when(pl.program_id(2) == 0)
    def _():
        k_scr[:lc, :LANES] = knc_ref[...]
        k_scr[:lc, LANES:] = kpec_ref[...]
        k_scr[lc:, :LANES] = knl_ref[...]
        k_scr[lc:, LANES:] = _rope_rotate(kpel_ref[...], cosk_ref[...], sink_ref[...]).astype(BF16)
        v_scr[:lc, :LANES] = vc_ref[...]
        v_scr[lc:, :LANES] = vl_ref[...]
        v_scr[:, LANES:] = jnp.ones((v_scr.shape[0], LANES), BF16)

    qpe = _rope_rotate(qpe_ref[...], cosq_ref[...], sinq_ref[...]).astype(BF16)
    q = jnp.concatenate([qn_ref[...], qpe], axis=1)
    sub = min(MLA_SUB_ROWS, q.shape[0])
    n_sub = q.shape[0] // sub
    k = k_scr[...]
    v = v_scr[...]

    def scores(i):
        return _dot_t(q[i * sub:(i + 1) * sub], k)

    def finish(i, s):
        p = jnp.exp2((s - jnp.max(s, axis=-1, keepdims=True)) * (scale * LOG2E)).astype(BF16)
        ol = _dot(p, v)
        o_ref[i * sub:(i + 1) * sub, :] = (ol[:, :LANES] / ol[:, LANES:]).astype(o_ref.dtype)

    s_cur = scores(0)
    for i in range(n_sub):
        s_next = scores(i + 1) if i + 1 < n_sub else None
        finish(i, s_cur)
        s_cur = s_next


def _mla_latent(p_lat, p_ctx, kv_lat, kv_ctx, cos_tab, sin_tab, batch, tq=2048):
    n = p_lat.shape[0] // batch
    lc = p_ctx.shape[0] // batch
    tq = min(tq, n)
    assert n % tq == 0 and tq % min(MLA_SUB_ROWS, tq) == 0
    nq = n // tq
    h = MLA_HEADS
    blk = lambda rows, f: pl.BlockSpec((rows, LANES), f)
    return pl.pallas_call(
        functools.partial(_mla_lat_kernel, scale=MLA_QK_DIM ** -0.5),
        grid=(batch, h, nq),
        in_specs=[
            blk(tq, lambda b, hh, i: (b * nq + i, EV_QMLA_BLK + 2 * hh)),
            blk(tq, lambda b, hh, i: (b * nq + i, EV_QMLA_BLK + 2 * hh + 1)),
            blk(tq, lambda b, hh, i: (i, 0)),
            blk(tq, lambda b, hh, i: (i, 0)),
            blk(lc, lambda b, hh, i: (b, 2 * hh)),
            blk(lc, lambda b, hh, i: (b, EV_KPE_BLK)),
            blk(lc, lambda b, hh, i: (b, 2 * hh + 1)),
            blk(n, lambda b, hh, i: (b, 2 * hh)),
            blk(n, lambda b, hh, i: (b, EV_KPE_BLK)),
            blk(n, lambda b, hh, i: (0, 0)),
            blk(n, lambda b, hh, i: (0, 0)),
            blk(n, lambda b, hh, i: (b, 2 * hh + 1)),
        ],
        out_specs=blk(tq, lambda b, hh, i: (b * nq + i, hh)),
        out_shape=jax.ShapeDtypeStruct((batch * n, h * LANES), BF16),
        scratch_shapes=[pltpu.VMEM((lc + n, 2 * LANES), BF16), pltpu.VMEM((lc + n, 2 * LANES), BF16)],
        compiler_params=_cparams(("arbitrary", "arbitrary", "arbitrary")),
    )(p_lat, p_lat, cos_tab, sin_tab, kv_ctx, p_ctx, kv_ctx, kv_lat, p_lat, cos_tab, sin_tab, kv_lat)


def _ctx_attn_kernel(qm_ref, kn_ref, kpe_ref, vm_ref, qn_ref, kna_ref, vna_ref, om_ref, on_ref,
                     *, mla_scale, na_scale):
    k = jnp.concatenate([kn_ref[...], kpe_ref[...]], axis=1)
    s = _dot_t(qm_ref[...], k)
    om_ref[...] = _softmax_pv([s], [vm_ref[...]], mla_scale).astype(om_ref.dtype)
    s = _dot_t(qn_ref[...], kna_ref[...])
    on_ref[...] = _softmax_pv([s], [vna_ref[...]], na_scale).astype(on_ref.dtype)


def _ctx_attention(p_ctx, kv_ctx, batch):
    lc = p_ctx.shape[0] // batch
    h = MLA_HEADS
    blk = lambda f: pl.BlockSpec((lc, LANES), f)
    out = jax.ShapeDtypeStruct((batch * lc, h * LANES), BF16)
    return pl.pallas_call(
        functools.partial(_ctx_attn_kernel, mla_scale=MLA_QK_DIM ** -0.5, na_scale=NA_DIM ** -0.5),
        grid=(batch, h),
        in_specs=[
            pl.BlockSpec((lc, 2 * LANES), lambda b, hh: (b, hh)),
            blk(lambda b, hh: (b, 2 * hh)),
            blk(lambda b, hh: (b, EV_KPE_BLK)),
            blk(lambda b, hh: (b, 2 * hh + 1)),
            blk(lambda b, hh: (b, EV_QNA_BLK + hh)),
            blk(lambda b, hh: (b, EV_KNA_BLK + hh)),
            blk(lambda b, hh: (b, EV_VNA_BLK + hh)),
        ],
        out_specs=[blk(lambda b, hh: (b, hh)), blk(lambda b, hh: (b, hh))],
        out_shape=[out, out],
        compiler_params=_cparams(("arbitrary", "arbitrary")),
    )(p_ctx, kv_ctx, p_ctx, kv_ctx, p_ctx, p_ctx, p_ctx)


def _na_kernel(q_ref, k_ref, v_ref, kc_ref, vc_ref, bias_ref, o_ref, s_scr, p_scr, l_scr, oc_scr,
               *, scale, n_rows):
    win = NA_KH * GRID_W
    slab = 256
    n = q_ref.shape[0]

    def band(r):
        ws = min(max(r - NA_KH // 2, 0), n_rows - NA_KH)
        return ws, slice(r * GRID_W, (r + 1) * GRID_W), slice(ws * GRID_W, ws * GRID_W + win)

    s_scr[:, win:] = _dot_t(q_ref[...], kc_ref[...]) * scale
    for r in range(n_rows):
        ws, rows, keys = band(r)
        s_scr[rows, :win] = _dot_t(q_ref[rows, :], k_ref[keys, :]) * scale + bias_ref[r - ws]

    def body(i, carry):
        sl = pl.ds(pl.multiple_of(i * slab, slab), slab)
        s = s_scr[sl, :]
        p = jnp.exp2((s - jnp.max(s, axis=-1, keepdims=True)) * LOG2E)
        p_scr[sl, :] = p.astype(BF16)
        l_scr[sl, :] = jnp.broadcast_to(1.0 / jnp.sum(p, axis=-1, keepdims=True), (slab, LANES))
        return carry

    lax.fori_loop(0, n // slab, body, 0)
    oc_scr[...] = _dot(p_scr[:, win:], vc_ref[...])
    for r in range(n_rows):
        ws, rows, keys = band(r)
        o = _dot(p_scr[rows, :win], v_ref[keys, :]) + oc_scr[rows, :]
        o_ref[rows, :] = (o * l_scr[rows, :]).astype(o_ref.dtype)


def _na_bias_kernel(rb_ref, onehot_ref, mask_ref, o_ref):
    o_ref[...] = _dot_f32(rb_ref[...], onehot_ref[...]) + mask_ref[...]


def _na_bias_table(rel_bias):
    n_heads, n_ro, n_co = rel_bias.shape
    col = np.arange(GRID_W)
    col_start = np.clip(col - NA_KW // 2, 0, GRID_W - NA_KW)
    col_mask = (col[None, :] >= col_start[:, None]) & (col[None, :] < col_start[:, None] + NA_KW)
    col_off = np.clip(col[None, :] - col[:, None], 1 - NA_KW, NA_KW - 1) + (NA_KW - 1)
    onehot = (col_off.reshape(1, -1) == np.arange(n_co)[:, None]).astype(np.float32)
    mask_add = np.where(col_mask.reshape(1, -1), 0.0, NEG_INF).astype(np.float32)
    qw = GRID_W * GRID_W
    full = lambda shape: pl.BlockSpec(shape, lambda: (0,) * len(shape))
    cols = pl.pallas_call(
        _na_bias_kernel,
        in_specs=[full((n_heads * n_ro, n_co)), full((n_co, qw)), full((1, qw))],
        out_specs=full((n_heads * n_ro, qw)),
        out_shape=jax.ShapeDtypeStruct((n_heads * n_ro, qw), F32),
    )(rel_bias.reshape(n_heads * n_ro, n_co), jnp.asarray(onehot), jnp.asarray(mask_add))
    cols = cols.reshape(n_heads, n_ro, GRID_W, GRID_W)
    t = jnp.stack([cols[:, NA_KH - 1 - e:2 * NA_KH - 1 - e] for e in range(NA_KH)], axis=1)
    return t.transpose(0, 1, 3, 2, 4).reshape(n_heads, NA_KH, GRID_W, NA_KH * GRID_W)


def _na_latent(p_lat, p_ctx, bias_tab, batch):
    n = p_lat.shape[0] // batch
    lc = p_ctx.shape[0] // batch
    h = NA_HEADS
    n_rows = n // GRID_W
    assert n_rows >= NA_KH
    blk = lambda rows, f: pl.BlockSpec((rows, LANES), f)
    return pl.pallas_call(
        functools.partial(_na_kernel, scale=NA_DIM ** -0.5, n_rows=n_rows),
        grid=(batch, h),
        in_specs=[
            blk(n, lambda b, hh: (b, EV_QNA_BLK + hh)),
            blk(n, lambda b, hh: (b, EV_KNA_BLK + hh)),
            blk(n, lambda b, hh: (b, EV_VNA_BLK + hh)),
            blk(lc, lambda b, hh: (b, EV_KNA_BLK + hh)),
            blk(lc, lambda b, hh: (b, EV_VNA_BLK + hh)),
            pl.BlockSpec((None, NA_KH, GRID_W, NA_KH * GRID_W), lambda b, hh: (hh, 0, 0, 0)),
        ],
        out_specs=blk(n, lambda b, hh: (b, hh)),
        out_shape=jax.ShapeDtypeStruct((batch * n, h * LANES), BF16),
        scratch_shapes=[pltpu.VMEM((n, NA_KH * GRID_W + lc), F32), pltpu.VMEM((n, NA_KH * GRID_W + lc), BF16),
                        pltpu.VMEM((n, LANES), F32), pltpu.VMEM((n, LANES), F32)],
        compiler_params=_cparams(("arbitrary", "arbitrary")),
    )(p_lat, p_lat, p_lat, p_ctx, p_ctx, bias_tab)


def _rope_tables(n):
    pos = np.arange(n)
    rows, cols = pos // GRID_W, pos % GRID_W
    half = MLA_ROPE_DIM // 2
    inv_freq = ROPE_THETA ** (-np.arange(0, half, 2, dtype=np.float64) / half)
    cos = np.zeros((n, LANES), np.float64)
    sin = np.zeros((n, LANES), np.float64)
    for base, p in ((0, rows), (half, cols)):
        ang = p[:, None].astype(np.float64) * inv_freq[None, :]
        q = half // 2
        cos[:, base:base + q] = np.cos(ang)
        cos[:, base + q:base + half] = np.cos(ang)
        sin[:, base:base + q] = -np.sin(ang)
        sin[:, base + q:base + half] = np.sin(ang)
    return jnp.asarray(cos, F32), jnp.asarray(sin, F32)


def _even_w_in(w):
    d = w.shape[0]
    z64 = jnp.zeros((d, LANES - MLA_ROPE_DIM), w.dtype)
    pieces = []
    for h in range(MLA_HEADS):
        pieces += [w[:, h * MLA_QK_DIM:h * MLA_QK_DIM + MLA_NOPE_DIM],
                   w[:, h * MLA_QK_DIM + MLA_NOPE_DIM:(h + 1) * MLA_QK_DIM], z64]
    q_end = MLA_HEADS * MLA_QK_DIM
    ckv_end = q_end + MLA_KV_RANK
    kpe_end = ckv_end + MLA_ROPE_DIM
    pieces += [w[:, kpe_end:], w[:, q_end:ckv_end], w[:, ckv_end:kpe_end], z64]
    out = jnp.concatenate(pieces, axis=1)
    pad = EV_WIDTH - out.shape[1]
    return jnp.concatenate([out, jnp.zeros((d, pad), w.dtype)], axis=1).astype(BF16)


def _hgrn_gates(q_ref, z_refs, lbs, row0, qb_s, cum_s, cpk_s, qd_s, kd_s, dec_s):
    c = HGRN_CHUNK
    grp = LANES
    ri = lax.broadcasted_iota(jnp.int32, (grp, grp), 0)
    ci = lax.broadcasted_iota(jnp.int32, (grp, grp), 1)
    same = (ri // c) == (ci // c)
    blk = jnp.where(same, 1.0, 0.0).astype(BF16)
    tris = (jnp.where(jnp.logical_and(same, ci <= ri), 1.0, 0.0).astype(BF16),
            jnp.where(jnp.logical_and(same, ci >= ri), 1.0, 0.0).astype(BF16))

    def body(g, carry):
        src = pl.ds(pl.multiple_of(g * grp, grp), grp)
        dst = pl.ds(pl.multiple_of(row0 + g * grp, grp), grp)
        q = _silu(q_ref[src, :].astype(F32))
        qb_s[dst, :] = q.astype(BF16)
        for d in range(2):
            f = jnp.maximum(lbs[d] + (1.0 - lbs[d]) * _sigmoid(z_refs[d][src, :].astype(F32)), FORGET_FLOOR)
            lf = jnp.log(f)
            k = 1.0 - f
            cum = _dot_01(tris[d], lf)
            tot = _dot_01(blk, lf)
            cum_s[d, dst, :] = cum * LOG2E
            cpk_s[d, dst, :] = (cum - jnp.log(k)) * LOG2E
            qd_s[d, dst, :] = (q * jnp.exp(cum)).astype(BF16)
            kd_s[d, dst, :] = (k * jnp.exp(tot - cum)).astype(BF16)
            dec_s[d, dst, :] = jnp.exp(tot)
        return carry

    lax.fori_loop(0, q_ref.shape[0] // grp, body, 0)


def _hgrn_chunk(qb, cum2, cpk2, v, qd, st, ones_bf, reverse):
    c = HGRN_CHUNK
    hc = c // 2
    o = _dot_t(qd, st.astype(BF16))
    rows = lax.broadcasted_iota(jnp.int32, (hc, LANES), 0)
    halves = (cum2[:hc], cum2[hc:])
    zero = jnp.zeros((hc, LANES), F32)
    pieces = []
    for s in range(c):
        ref = cpk2[s:s + 1]
        hs, rs = divmod(s, hc)
        es = []
        for hh in range(2):
            if hh == hs:
                mask = (rows <= rs) if reverse else (rows >= rs)
                es.append(jnp.exp2(jnp.where(mask, halves[hh] - ref, NEG_INF)))
            elif (hh > hs) != reverse:
                es.append(jnp.exp2(halves[hh] - ref))
            else:
                es.append(zero)
        pieces.append(jnp.concatenate(es, axis=0).astype(BF16) * qb)
    lhs = jnp.concatenate([jnp.concatenate(pieces[:hc], axis=0), jnp.concatenate(pieces[hc:], axis=0)], axis=1)
    r = _dot(lhs, ones_bf)
    for s in range(c):
        hs, rs = divmod(s, hc)
        o = o + r[rs * c:(rs + 1) * c, hs * LANES:(hs + 1) * LANES] * v[s:s + 1]
    return o


def _hgrn_gates_fast(q_ref, i_ref, z_refs, lbs, row0, worst, v_s, qf_s, kf_s, qdf_s, kdf_s, decf_s):
    c = HGRN_FAST_CHUNK
    grp = LANES
    ri = lax.broadcasted_iota(jnp.int32, (grp, grp), 0)
    ci = lax.broadcasted_iota(jnp.int32, (grp, grp), 1)
    same = (ri // c) == (ci // c)
    half = c // 2
    sums = []
    for fwd in (True, False):
        tri = (ci <= ri) if fwd else (ci >= ri)
        upto_ref = ((ci % c) <= half) if fwd else ((ci % c) >= half)
        rows = [jnp.logical_and(same, tri), same, jnp.logical_and(same, upto_ref)]
        sums.append(jnp.concatenate([jnp.where(m, 1.0, 0.0) for m in rows], axis=0).astype(BF16))

    def body(g, worst):
        src = pl.ds(pl.multiple_of(g * grp, grp), grp)
        dst = pl.ds(pl.multiple_of(row0 + g * grp, grp), grp)
        q = _silu(q_ref[src, :].astype(F32))
        v_s[dst, :] = i_ref[src, :].astype(F32)
        fs = [jnp.maximum(lbs[d] + (1.0 - lbs[d]) * _sigmoid(z_refs[d][src, :].astype(F32)), FORGET_FLOOR)
              for d in range(2)]
        res = [_dot_01(sums[d], jnp.log(fs[d])) for d in range(2)]
        for d in range(2):
            k = 1.0 - fs[d]
            cum, tot, ref = res[d][:grp], res[d][grp:2 * grp], res[d][2 * grp:]
            qf_s[d, dst, :] = (q * jnp.exp(cum - ref)).astype(BF16)
            kf_s[d, dst, :] = (k * jnp.exp(ref - cum)).astype(BF16)
            qdf_s[d, dst, :] = (q * jnp.exp(cum)).astype(BF16)
            kdf_s[d, dst, :] = (k * jnp.exp(tot - cum)).astype(BF16)
            decf_s[d, dst, :] = jnp.exp(tot)
            worst = jnp.maximum(worst, -tot)
        return worst

    return lax.fori_loop(0, q_ref.shape[0] // grp, body, worst, unroll=2)


def _hgrn_kernel(ql_ref, il_ref, zfl_ref, zbl_ref, gl_ref, qc_ref, ic_ref, zfc_ref, zbc_ref, gc_ref,
                 lbl_ref, ng_ref, yl_ref, yc_ref,
                 qb_s, v_s, cum_s, cpk_s, qd_s, kd_s, dec_s, o_s, qf_s, kf_s, qdf_s, kdf_s, decf_s, *, layer):
    lc, n = qc_ref.shape[0], ql_ref.shape[0]
    lbs = []
    for d in range(2):
        lg = lbl_ref[d]
        ex = jnp.exp(lg - jnp.max(lg, axis=0, keepdims=True))
        p = ex / jnp.sum(ex, axis=0, keepdims=True)
        lbs.append(jnp.sum(p[:layer + 1], axis=0, keepdims=True) - p[0:1])

    zero = jnp.zeros((LANES, LANES), F32)

    def chunk_rows(j, c):
        r_fwd = j * c
        r_bwd = jnp.where(j < lc // c, lc - c - j * c, 2 * lc + n - c - j * c)
        return [pl.ds(pl.multiple_of(r, c), c) for r in (r_fwd, r_bwd)]

    fast_scr = (v_s, qf_s, kf_s, qdf_s, kdf_s, decf_s)
    worst = _hgrn_gates_fast(qc_ref, ic_ref, (zfc_ref, zbc_ref), lbs, 0, zero, *fast_scr)
    worst = _hgrn_gates_fast(ql_ref, il_ref, (zfl_ref, zbl_ref), lbs, lc, worst, *fast_scr)
    fast_ok = jnp.max(worst) <= HGRN_FAST_MAX_DECAY

    @pl.when(fast_ok)
    def _():
        c = HGRN_FAST_CHUNK
        ri = lax.broadcasted_iota(jnp.int32, (c, c), 0)
        ci = lax.broadcasted_iota(jnp.int32, (c, c), 1)
        causal = (ci <= ri, ci >= ri)

        def body(g, sts):
            slices = [chunk_rows(g * HGRN_FAST_UNROLL + i, c) for i in range(HGRN_FAST_UNROLL)]
            scores = [[jnp.where(causal[d], _dot_t(qf_s[d, sl[d], :], kf_s[d, sl[d], :]), 0.0).astype(BF16)
                       for d in range(2)] for sl in slices]
            incs = [[_dot_tn(v_s[sl[d], :].astype(BF16), kdf_s[d, sl[d], :]) for d in range(2)] for sl in slices]
            local = [[_dot(scores[i][d], v_s[sl[d], :].astype(BF16)) for d in range(2)]
                     for i, sl in enumerate(slices)]
            sts = list(sts)
            for i, sl2 in enumerate(slices):
                for d in range(2):
                    sl = sl2[d]
                    o_s[d, sl, :] = _dot_t(qdf_s[d, sl, :], sts[d].astype(BF16)) + local[i][d]
                    sts[d] = sts[d] * decf_s[d, sl, :][0:1] + incs[i][d]
            return tuple(sts)

        n_chunks = (lc + n) // c
        assert n_chunks % HGRN_FAST_UNROLL == 0
        lax.fori_loop(0, n_chunks // HGRN_FAST_UNROLL, body, (zero, zero))

    @pl.when(jnp.logical_not(fast_ok))
    def _():
        c = HGRN_CHUNK
        scr = (qb_s, cum_s, cpk_s, qd_s, kd_s, dec_s)
        _hgrn_gates(qc_ref, (zfc_ref, zbc_ref), lbs, 0, *scr)
        _hgrn_gates(ql_ref, (zfl_ref, zbl_ref), lbs, lc, *scr)
        ri = lax.broadcasted_iota(jnp.int32, (2 * LANES, 2 * LANES), 0)
        ci = lax.broadcasted_iota(jnp.int32, (2 * LANES, 2 * LANES), 1)
        ones_bf = jnp.where((ri // LANES) == (ci // LANES), 1.0, 0.0).astype(BF16)

        def body(g, sts):
            slices = [chunk_rows(g * HGRN_UNROLL + i, c) for i in range(HGRN_UNROLL)]
            incs = [[_dot_tn(v_s[sl[d], :].astype(BF16), kd_s[d, sl[d], :]) for d in range(2)] for sl in slices]
            sts = list(sts)
            for i, sl2 in enumerate(slices):
                for d in range(2):
                    sl = sl2[d]
                    o_s[d, sl, :] = _hgrn_chunk(qb_s[sl, :], cum_s[d, sl, :], cpk_s[d, sl, :], v_s[sl, :],
                                                qd_s[d, sl, :], sts[d], ones_bf, reverse=d == 1)
                    sts[d] = sts[d] * dec_s[d, sl, :][0:1] + incs[i][d]
            return tuple(sts)

        n_chunks = (lc + n) // c
        assert n_chunks % HGRN_UNROLL == 0
        lax.fori_loop(0, n_chunks // HGRN_UNROLL, body, (zero, zero))

    ng = ng_ref[...]
    yc_ref[...] = (_rms(o_s[0, :lc, :] + o_s[1, :lc, :]) * ng
                   * _silu(gc_ref[...].astype(F32))).astype(yc_ref.dtype)
    yl_ref[...] = (_rms(o_s[0, lc:, :] + o_s[1, lc:, :]) * ng
                   * _silu(gl_ref[...].astype(F32))).astype(yl_ref.dtype)


def _hgrn2(p_lat, p_ctx, lb_logits, norm_g, layer, batch):
    n = p_lat.shape[0] // batch
    lc = p_ctx.shape[0] // batch
    h = HGRN_HEADS
    n_layers = lb_logits.shape[1]
    rows = lc + n
    assert lc % LANES == 0 and n % LANES == 0
    lat = lambda part: pl.BlockSpec((n, LANES), lambda b, hh: (b, part * h + hh))
    ctx = lambda part: pl.BlockSpec((lc, LANES), lambda b, hh: (b, part * h + hh))
    return pl.pallas_call(
        functools.partial(_hgrn_kernel, layer=layer),
        grid=(batch, h),
        in_specs=[lat(0), lat(1), lat(2), lat(3), lat(4), ctx(0), ctx(1), ctx(2), ctx(3), ctx(4),
                  pl.BlockSpec((2, n_layers, LANES), lambda b, hh: (0, 0, hh)),
                  pl.BlockSpec((1, LANES), lambda b, hh: (0, hh))],
        out_specs=[pl.BlockSpec((n, LANES), lambda b, hh: (b, hh)),
                   pl.BlockSpec((lc, LANES), lambda b, hh: (b, hh))],
        out_shape=[jax.ShapeDtypeStruct((batch * n, h * LANES), BF16),
                   jax.ShapeDtypeStruct((batch * lc, h * LANES), BF16)],
        scratch_shapes=[pltpu.VMEM((rows, LANES), BF16), pltpu.VMEM((rows, LANES), F32),
                        pltpu.VMEM((2, rows, LANES), F32), pltpu.VMEM((2, rows, LANES), F32),
                        pltpu.VMEM((2, rows, LANES), BF16), pltpu.VMEM((2, rows, LANES), BF16),
                        pltpu.VMEM((2, rows, LANES), F32), pltpu.VMEM((2, rows, LANES), F32),
                        pltpu.VMEM((2, rows, LANES), BF16), pltpu.VMEM((2, rows, LANES), BF16),
                        pltpu.VMEM((2, rows, LANES), BF16), pltpu.VMEM((2, rows, LANES), BF16),
                        pltpu.VMEM((2, rows, LANES), F32)],
        compiler_params=_cparams(("arbitrary", "arbitrary")),
    )(p_lat, p_lat, p_lat, p_lat, p_lat, p_ctx, p_ctx, p_ctx, p_ctx, p_ctx,
      lb_logits, norm_g[layer].reshape(1, -1))


def _dft_matrices(n):
    idx = (np.arange(n)[:, None] * np.arange(n)[None, :]) % (2 * n)
    ang = idx.astype(np.float64) * (math.pi / n)
    cm = np.cos(ang)
    sf = np.sin(ang)
    sf[0, :] = (-1.0) ** np.arange(n)
    return (jnp.asarray(cm, F32).astype(BF16), jnp.asarray(sf, F32).astype(BF16),
            jnp.asarray(sf.T, F32).astype(BF16))


def _filter_features(n):
    pos = np.arange(n, dtype=np.float64)
    t = pos / max(n - 1, 1)
    bands = np.linspace(1e-4, HYENA_BANDS - 1, HYENA_BANDS)
    ang = (2.0 * math.pi / n) * pos[:, None] * bands[None, :]
    z = np.concatenate([t[:, None], np.cos(ang), -np.sin(ang)], -1)
    max_decay = math.log(HYENA_DECAY_TARGET) / HYENA_FAST_PCT
    min_decay = math.log(HYENA_DECAY_TARGET) / HYENA_SLOW_PCT
    deltas = np.abs(np.linspace(min_decay, max_decay, HYENA_WIDTH))
    return jnp.asarray(z, F32), jnp.asarray(t[:, None], F32), jnp.asarray(deltas[None, :], F32)


def _filter_kernel(z_ref, t_ref, dl_ref, w1_ref, b1_ref, w2_ref, b2_ref, w3_ref, b3_ref, fr_ref, wo_ref,
                   o_ref, hdn_ref):
    j = pl.program_id(0)

    @pl.when(j == 0)
    def _():
        fr = fr_ref[...]
        hdn = jnp.sin(fr * (_dot_f32(z_ref[...], w1_ref[...]) + b1_ref[...]))
        hdn = jnp.sin(fr * (_dot_f32(hdn, w2_ref[...]) + b2_ref[...]))
        hdn_ref[...] = jnp.sin(fr * (_dot_f32(hdn, w3_ref[...]) + b3_ref[...]))

    filt = _dot_f32(hdn_ref[...], wo_ref[...]) * jnp.exp(-t_ref[...] * dl_ref[...])
    row = lax.broadcasted_iota(jnp.int32, filt.shape, 0)
    is_bwd = j >= pl.num_programs(0) // 2
    o_ref[...] = jnp.where(jnp.logical_and(is_bwd, row == 0), 0.0, filt).astype(o_ref.dtype)


def _hyena_filters(n, w1, b1, w2, b2, w3, b3, freq, w_out, tc=512):
    z, t, deltas = _filter_features(n)
    hid = HYENA_FILT_HIDDEN
    nct = HYENA_WIDTH // tc
    full = lambda shape: pl.BlockSpec(shape, lambda j: (0,) * len(shape))
    return pl.pallas_call(
        _filter_kernel,
        grid=(2 * nct,),
        in_specs=[full((n, HYENA_EMB)), full((n, 1)),
                  pl.BlockSpec((1, tc), lambda j: (0, j % nct)),
                  full((HYENA_EMB, hid)), full((1, hid)), full((hid, hid)), full((1, hid)),
                  full((hid, hid)), full((1, hid)), full((1, hid)),
                  pl.BlockSpec((hid, tc), lambda j: (0, j))],
        out_specs=pl.BlockSpec((n, tc), lambda j: (0, j)),
        out_shape=jax.ShapeDtypeStruct((n, 2 * HYENA_WIDTH), BF16),
        scratch_shapes=[pltpu.VMEM((n, hid), F32)],
        compiler_params=_cparams(("arbitrary",)),
    )(z, t, deltas, w1, b1.reshape(1, hid), w2, b2.reshape(1, hid), w3, b3.reshape(1, hid),
      freq.reshape(1, hid), w_out)


def _spectrum_kernel(cm_ref, sf_ref, hf_ref, hb_ref, a_ref, b_ref, *, inv_len):
    cm, sf, hf, hb = cm_ref[...], sf_ref[...], hf_ref[...], hb_ref[...]
    kr = _dot(cm, hf) + _dot(cm, hb)
    d1 = _dot(sf, hf)
    d2 = _dot(sf, hb)
    row = lax.broadcasted_iota(jnp.int32, kr.shape, 0) + pl.program_id(0) * kr.shape[0]
    first = row == 0
    w = jnp.where(first, inv_len, 2.0 * inv_len)
    a_ref[...] = kr * w
    b_ref[...] = jnp.where(first, d1 + d2, d1 - d2) * w


def _filter_spectrum(filt, cm, sf, tk=512, tc=512):
    n = filt.shape[0]
    tk = min(tk, n)
    nct = HYENA_WIDTH // tc
    out = jax.ShapeDtypeStruct((n, HYENA_WIDTH), F32)
    return pl.pallas_call(
        functools.partial(_spectrum_kernel, inv_len=1.0 / (2 * n)),
        grid=(n // tk, nct),
        in_specs=[pl.BlockSpec((tk, n), lambda i, j: (i, 0)),
                  pl.BlockSpec((tk, n), lambda i, j: (i, 0)),
                  pl.BlockSpec((n, tc), lambda i, j: (0, j)),
                  pl.BlockSpec((n, tc), lambda i, j: (0, nct + j))],
        out_specs=[pl.BlockSpec((tk, tc), lambda i, j: (i, j)), pl.BlockSpec((tk, tc), lambda i, j: (i, j))],
        out_shape=[out, out],
        compiler_params=_cparams(("arbitrary", "arbitrary")),
    )(cm, sf, filt, filt)


def _hyena_gate_kernel(u0_ref, u1_ref, uv_ref, w0_ref, w1_ref, wv_ref, b0_ref, b1_ref, bv_ref,
                       x0_ref, z_ref):
    n = u0_ref.shape[0]
    row = lax.broadcasted_iota(jnp.int32, u0_ref.shape, 0)

    def conv(u_ref, w_ref, b_ref):
        u = u_ref[...].astype(F32)
        prev = jnp.where(row == 0, 0.0, pltpu.roll(u, 1, 0))
        nxt = jnp.where(row == n - 1, 0.0, pltpu.roll(u, n - 1, 0))
        return b_ref[...] + prev * w_ref[0:1] + u * w_ref[1:2] + nxt * w_ref[2:3]

    x0_ref[...] = conv(u0_ref, w0_ref, b0_ref)
    z_ref[...] = conv(uv_ref, wv_ref, bv_ref) * conv(u1_ref, w1_ref, b1_ref)


def _hyena_gate(p, first_blk, conv_w, conv_b, batch, tc=256):
    n = p.shape[0] // batch
    nct = HYENA_WIDTH // tc
    c0 = first_blk * LANES // tc
    u = lambda part: pl.BlockSpec((n, tc), lambda b, j: (b, c0 + part * nct + j))
    w = lambda part: pl.BlockSpec((HYENA_SHORT, tc), lambda b, j: (0, part * nct + j))
    bb = lambda part: pl.BlockSpec((1, tc), lambda b, j: (0, part * nct + j))
    out = jax.ShapeDtypeStruct((batch * n, HYENA_WIDTH), F32)
    cb = conv_b.reshape(1, -1)
    return pl.pallas_call(
        _hyena_gate_kernel,
        grid=(batch, nct),
        in_specs=[u(0), u(1), u(2), w(0), w(1), w(2), bb(0), bb(1), bb(2)],
        out_specs=[pl.BlockSpec((n, tc), lambda b, j: (b, j)), pl.BlockSpec((n, tc), lambda b, j: (b, j))],
        out_shape=[out, out],
        compiler_params=_cparams(("arbitrary", "arbitrary")),
    )(p, p, p, conv_w, conv_w, conv_w, cb, cb, cb)


def _dft_fwd_kernel(cm_ref, sf_ref, z_ref, a_ref, b_ref, pr_ref, ps_ref):
    z = z_ref[...].astype(BF16)
    zr = _dot(cm_ref[...], z)
    zs = _dot(sf_ref[...], z)
    a, b = a_ref[...], b_ref[...]
    row = lax.broadcasted_iota(jnp.int32, zr.shape, 0) + pl.program_id(1) * zr.shape[0]
    first = row == 0
    pr_ref[...] = (zr * a - jnp.where(first, 0.0, zs * b)).astype(pr_ref.dtype)
    ps_ref[...] = (jnp.where(first, 0.0, zr * b) + zs * jnp.where(first, b, a)).astype(ps_ref.dtype)


def _dft_inv_kernel(cm_ref, si_ref, pr_ref, ps_ref, z_ref, x0_ref, skip_ref, o_ref):
    y = _dot(cm_ref[...], pr_ref[...]) + _dot(si_ref[...], ps_ref[...])
    o_ref[...] = (x0_ref[...] * (y + z_ref[...] * skip_ref[...])).astype(o_ref.dtype)


def _long_conv(z, x0, spec_a, spec_b, skip, cm, sf, si, batch, tk=1024, tc=512):
    n = z.shape[0] // batch
    tk = min(tk, n)
    nk = n // tk
    nct = HYENA_WIDTH // tc
    mat = pl.BlockSpec((tk, n), lambda b, i, j: (i, 0))
    col = pl.BlockSpec((n, tc), lambda b, i, j: (b, j))
    tile_nb = pl.BlockSpec((tk, tc), lambda b, i, j: (i, j))
    tile = pl.BlockSpec((tk, tc), lambda b, i, j: (b * nk + i, j))
    spec_shape = jax.ShapeDtypeStruct((batch * n, HYENA_WIDTH), BF16)
    pr, ps = pl.pallas_call(
        _dft_fwd_kernel,
        grid=(batch, nk, nct),
        in_specs=[mat, mat, col, tile_nb, tile_nb],
        out_specs=[tile, tile],
        out_shape=[spec_shape, spec_shape],
        compiler_params=_cparams(("arbitrary", "arbitrary", "arbitrary")),
    )(cm, sf, z, spec_a, spec_b)
    return pl.pallas_call(
        _dft_inv_kernel,
        grid=(batch, nk, nct),
        in_specs=[mat, mat, col, col, tile, tile, pl.BlockSpec((1, tc), lambda b, i, j: (0, j))],
        out_specs=tile,
        out_shape=jax.ShapeDtypeStruct((batch * n, HYENA_WIDTH), BF16),
        compiler_params=_cparams(("arbitrary", "arbitrary", "arbitrary")),
    )(cm, si, pr, ps, z, x0, skip.reshape(1, -1))


def _hyena(p, first_blk, conv_w, conv_b, filt_params, skip, batch):
    n = p.shape[0] // batch
    cm, sf, si = _dft_matrices(n)
    filt = _hyena_filters(n, *filt_params)
    spec_a, spec_b = _filter_spectrum(filt, cm, sf)
    x0, z = _hyena_gate(p, first_blk, conv_w, conv_b, batch)
    return _long_conv(z, x0, spec_a, spec_b, skip, cm, sf, si, batch)


def kernel(x, c, ctx, c_ctx, ada_w, ada_b, norm_mix_g, norm_mlp_g, w_out, mlp_w1, mlp_w2, final_norm_g, ev_w_in, mla_kv_norm_g, mla_w_ukv, na_rel_bias, od_w_in, hgrn_lb_logits, hgrn_norm_g, hy_conv_w, hy_conv_b, hy_filt_w1, hy_filt_b1, hy_filt_w2, hy_filt_b2, hy_filt_w3, hy_filt_b3, hy_filt_freq, hy_filt_wout, hy_skip):
    batch, seq, d = x.shape
    lc = ctx.shape[1]
    depth = ada_w.shape[0]
    h_lat = x.reshape(batch * seq, d)
    h_ctx = ctx.reshape(batch * lc, d)

    cond = jnp.concatenate([c, c_ctx[None, :], jnp.zeros((8 - batch - 1, d), F32)], axis=0)
    mod_all = _ada_modulation(cond, ada_w, ada_b)
    cos_tab, sin_tab = _rope_tables(seq)
    hgrn_cols = 5 * HGRN_WIDTH
    od_w_in, w_out, mlp_w1, mlp_w2, mla_w_ukv = (
        t.astype(BF16) for t in (od_w_in, w_out, mlp_w1, mlp_w2, mla_w_ukv))

    for l in range(depth):
        ctx_out = l < depth - 1
        mod3 = mod_all[l].reshape(8, 1, 6 * d)
        lat_mod = lambda *chunks: (mod3, *chunks, seq, 0)
        ctx_mod = lambda *chunks: (mod3, *chunks, batch * lc, batch)
        if l % 2 == 0:
            e = l // 2
            w_in = _even_w_in(ev_w_in[e])
            p_lat = _norm_proj(h_lat, 0, d, norm_mix_g[l], w_in, BF16, lat_mod(0, 1))
            p_ctx = _norm_proj(h_ctx, 0, d, norm_mix_g[l], w_in, BF16, ctx_mod(0, 1))
            ckv_blk = EV_CKV_BLK * LANES // MLA_KV_RANK
            kv_lat = _norm_proj(p_lat, ckv_blk, MLA_KV_RANK, mla_kv_norm_g[e], mla_w_ukv, BF16, layer=e)
            kv_ctx = _norm_proj(p_ctx, ckv_blk, MLA_KV_RANK, mla_kv_norm_g[e], mla_w_ukv, BF16, layer=e)
            y1_lat = _mla_latent(p_lat, p_ctx, kv_lat, kv_ctx, cos_tab, sin_tab, batch)
            y2_lat = _na_latent(p_lat, p_ctx, _na_bias_table(na_rel_bias[e]), batch)
            if ctx_out:
                y1_ctx, y2_ctx = _ctx_attention(p_ctx, kv_ctx, batch)
        else:
            o = l // 2
            p_lat = _norm_proj(h_lat, 0, d, norm_mix_g[l], od_w_in, BF16, lat_mod(0, 1), layer=o)
            p_ctx = _norm_proj(h_ctx, 0, d, norm_mix_g[l], od_w_in, BF16, ctx_mod(0, 1), layer=o,
                               n=None if ctx_out else hgrn_cols)
            y1_lat, y1_ctx = _hgrn2(p_lat, p_ctx, hgrn_lb_logits, hgrn_norm_g, o, batch)
            filt_params = (hy_filt_w1[o], hy_filt_b1[o], hy_filt_w2[o], hy_filt_b2[o], hy_filt_w3[o],
                           hy_filt_b3[o], hy_filt_freq[o], hy_filt_wout[o])
            y2_lat = _hyena(p_lat, hgrn_cols // LANES, hy_conv_w[o], hy_conv_b[o], filt_params,
                            hy_skip[o], batch)
            if ctx_out:
                y2_ctx = _hyena(p_ctx, hgrn_cols // LANES, hy_conv_w[o], hy_conv_b[o], filt_params,
                                hy_skip[o], batch)
        h_lat = _out_proj(y1_lat, y2_lat, w_out, l, h_lat, lat_mod(2))
        h_lat = _mlp(h_lat, norm_mlp_g[l], mlp_w1, mlp_w2, l, lat_mod(3, 4, 5),
                     final_g=None if ctx_out else final_norm_g)
        if ctx_out:
            h_ctx = _out_proj(y1_ctx, y2_ctx, w_out, l, h_ctx, ctx_mod(2))
            h_ctx = _mlp(h_ctx, norm_mlp_g[l], mlp_w1, mlp_w2, l, ctx_mod(3, 4, 5))
    return h_lat.reshape(batch, seq, d)
```

```python
import functools
import math

import numpy as np
import jax
import jax.numpy as jnp
from jax import lax
from jax.experimental import pallas as pl
from jax.experimental.pallas import tpu as pltpu

F32 = jnp.float32
BF16 = jnp.bfloat16

D_MODEL = 2048
DEPTH = 4
GRID_W = 64
HEAD_DIM = 128
MLA_HEADS = 8
MLA_NOPE_DIM = 128
MLA_ROPE_DIM = 64
MLA_QK_DIM = MLA_NOPE_DIM + MLA_ROPE_DIM
MLA_KV_RANK = 512
NA_HEADS = 8
NA_DIM = 128
NA_KH = 8
NA_KW = 16
ROPE_THETA = 10000.0
HGRN_WIDTH = 1024
HGRN_HEADS = 8
FORGET_FLOOR = 1e-30
HYENA_WIDTH = 1024
HYENA_SHORT = 3
HYENA_EMB = 33
HYENA_BANDS = (HYENA_EMB - 1) // 2
HYENA_FILT_HIDDEN = 64
HYENA_DECAY_TARGET = 1e-2
HYENA_FAST_PCT = 0.3
HYENA_SLOW_PCT = 1.5
MLP_HIDDEN = 4 * D_MODEL
NORM_EPS = 1e-6
NEG_INF = -1e30
LOG2E = 1.4426950408889634

LANES = 128
VMEM_LIMIT_BYTES = 56 * 1024 * 1024

EV_QMLA_BLK = 0
EV_QNA_BLK = 16
EV_KNA_BLK = 24
EV_VNA_BLK = 32
EV_CKV_BLK = 40
EV_KPE_BLK = 44
EV_WIDTH = 48 * LANES
MLA_SUB_ROWS = 512
HGRN_CHUNK = 16
HGRN_UNROLL = 8
HGRN_FAST_CHUNK = 32
HGRN_FAST_UNROLL = 8
HGRN_FAST_MAX_DECAY = 80.0


def _cparams(sem):
    return pltpu.CompilerParams(dimension_semantics=sem, vmem_limit_bytes=VMEM_LIMIT_BYTES)


def _dot(a, b):
    return jnp.dot(a, b, preferred_element_type=F32)


def _dot_t(a, b):
    return lax.dot_general(a, b, (((1,), (1,)), ((), ())), preferred_element_type=F32)


def _dot_tn(a, b):
    return lax.dot_general(a, b, (((0,), (0,)), ((), ())), preferred_element_type=F32)


def _dot_f32(a, b):
    return jnp.dot(a, b, preferred_element_type=F32, precision=lax.Precision.HIGHEST)


def _dot_01(m01, x):
    hi = x.astype(BF16)
    lo = (x - hi.astype(F32)).astype(BF16)
    return _dot(m01, hi) + _dot(m01, lo)


def _sigmoid(x):
    return 1.0 / (1.0 + jnp.exp(-x))


def _silu(x):
    return x * _sigmoid(x)


def _rms(x):
    return x * lax.rsqrt(jnp.mean(x * x, axis=-1, keepdims=True) + NORM_EPS)


def _ada_kernel(s_ref, w_ref, b_ref, o_ref):
    s = _silu(s_ref[...]).astype(BF16)
    o_ref[...] = _dot(s, w_ref[...].astype(BF16)) + b_ref[...]


def _ada_modulation(cond, ada_w, ada_b, tn=1024):
    depth, d, n = ada_w.shape
    rows = cond.shape[0]
    return pl.pallas_call(
        _ada_kernel,
        grid=(depth, n // tn),
        in_specs=[
            pl.BlockSpec((rows, d), lambda l, j: (0, 0)),
            pl.BlockSpec((None, d, tn), lambda l, j: (l, 0, j)),
            pl.BlockSpec((None, 1, tn), lambda l, j: (l, 0, j)),
        ],
        out_specs=pl.BlockSpec((None, rows, tn), lambda l, j: (l, 0, j)),
        out_shape=jax.ShapeDtypeStruct((depth, rows, n), F32),
        compiler_params=_cparams(("arbitrary", "arbitrary")),
    )(cond, ada_w, ada_b.reshape(depth, 1, n))


def _mod_spec(chunk, tiles_per_group, group0, d):
    return pl.BlockSpec((None, 1, d), lambda i, *_: (group0 + i // tiles_per_group, 0, chunk))


def _proj_kernel(*refs, modulated):
    if modulated:
        x_ref, g_ref, sh_ref, sc_ref, w_ref, o_ref, a_ref = refs
    else:
        x_ref, g_ref, w_ref, o_ref, a_ref = refs

    @pl.when(pl.program_id(1) == 0)
    def _():
        y = _rms(x_ref[...].astype(F32)) * g_ref[...]
        if modulated:
            y = y * (1.0 + sc_ref[...]) + sh_ref[...]
        a_ref[...] = y.astype(BF16)

    o_ref[...] = _dot(a_ref[...], w_ref[...]).astype(o_ref.dtype)


def _layer_spec(w, layer, block, index_map):
    if w.ndim == 2:
        return pl.BlockSpec(block, index_map)
    return pl.BlockSpec((None,) + block, lambda *idx: (layer,) + index_map(*idx))


def _norm_proj(x, x_col_blk, k, g, w, out_dtype, mod=None, layer=None, n=None, tm=1024, tn=1024):
    m = x.shape[0]
    n = w.shape[-1] if n is None else n
    tm = min(tm, m)
    tn = min(tn, n)
    assert m % tm == 0 and n % tn == 0
    in_specs = [pl.BlockSpec((tm, k), lambda i, j: (i, x_col_blk)),
                pl.BlockSpec((1, k), lambda i, j: (0, 0))]
    args = [x, g.reshape(1, k)]
    if mod is not None:
        mod3, sh_chunk, sc_chunk, rows_per_group, group0 = mod
        assert rows_per_group % tm == 0
        in_specs += [_mod_spec(sh_chunk, rows_per_group // tm, group0, k),
                     _mod_spec(sc_chunk, rows_per_group // tm, group0, k)]
        args += [mod3, mod3]
    in_specs.append(_layer_spec(w, layer, (k, tn), lambda i, j: (0, j)))
    args.append(w)
    return pl.pallas_call(
        functools.partial(_proj_kernel, modulated=mod is not None),
        grid=(m // tm, n // tn),
        in_specs=in_specs,
        out_specs=pl.BlockSpec((tm, tn), lambda i, j: (i, j)),
        out_shape=jax.ShapeDtypeStruct((m, n), out_dtype),
        scratch_shapes=[pltpu.VMEM((tm, k), BF16)],
        compiler_params=_cparams(("arbitrary", "arbitrary")),
    )(*args)


def _outproj_kernel(y1_ref, y2_ref, w_ref, h_ref, gate_ref, o_ref):
    k1 = y1_ref.shape[1]
    acc = _dot(y1_ref[...], w_ref[:k1, :]) + _dot(y2_ref[...], w_ref[k1:, :])
    o_ref[...] = h_ref[...] + gate_ref[...] * acc


def _out_proj(y1, y2, w, layer, h, mod, tm=512):
    m, d = h.shape
    k1, k2 = y1.shape[1], y2.shape[1]
    tm = min(tm, m)
    mod3, gate_chunk, rows_per_group, group0 = mod
    return pl.pallas_call(
        _outproj_kernel,
        grid=(m // tm,),
        in_specs=[
            pl.BlockSpec((tm, k1), lambda i: (i, 0)),
            pl.BlockSpec((tm, k2), lambda i: (i, 0)),
            _layer_spec(w, layer, (k1 + k2, d), lambda i: (0, 0)),
            pl.BlockSpec((tm, d), lambda i: (i, 0)),
            _mod_spec(gate_chunk, rows_per_group // tm, group0, d),
        ],
        out_specs=pl.BlockSpec((tm, d), lambda i: (i, 0)),
        out_shape=jax.ShapeDtypeStruct((m, d), F32),
        compiler_params=_cparams(("arbitrary",)),
    )(y1, y2, w, h, mod3)


def _mlp_kernel(*refs, final_norm):
    if final_norm:
        h_ref, g_ref, sh_ref, sc_ref, gate_ref, w1_ref, w2_ref, fg_ref, o_ref, a_ref, acc_ref = refs
    else:
        h_ref, g_ref, sh_ref, sc_ref, gate_ref, w1_ref, w2_ref, o_ref, a_ref, acc_ref = refs
    k = pl.program_id(1)

    @pl.when(k == 0)
    def _():
        y = _rms(h_ref[...]) * g_ref[...]
        a_ref[...] = (y * (1.0 + sc_ref[...]) + sh_ref[...]).astype(BF16)
        acc_ref[...] = jnp.zeros_like(acc_ref)

    u = jnp.maximum(_dot(a_ref[...], w1_ref[...]), 0.0)
    acc_ref[...] += _dot((u * u).astype(BF16), w2_ref[...])

    @pl.when(k == pl.num_programs(1) - 1)
    def _():
        out = h_ref[...] + gate_ref[...] * acc_ref[...]
        if final_norm:
            out = _rms(out) * fg_ref[...]
        o_ref[...] = out


def _mlp(h, g, w1, w2, layer, mod, final_g=None, tm=512, th=1024):
    m, d = h.shape
    hid = w1.shape[-1]
    tm = min(tm, m)
    mod3, sh_chunk, sc_chunk, gate_chunk, rows_per_group, group0 = mod
    tpg = rows_per_group // tm
    in_specs = [
        pl.BlockSpec((tm, d), lambda i, k: (i, 0)),
        pl.BlockSpec((1, d), lambda i, k: (0, 0)),
        _mod_spec(sh_chunk, tpg, group0, d),
        _mod_spec(sc_chunk, tpg, group0, d),
        _mod_spec(gate_chunk, tpg, group0, d),
        _layer_spec(w1, layer, (d, th), lambda i, k: (0, k)),
        _layer_spec(w2, layer, (th, d), lambda i, k: (k, 0)),
    ]
    args = [h, g.reshape(1, d), mod3, mod3, mod3, w1, w2]
    if final_g is not None:
        in_specs.append(pl.BlockSpec((1, d), lambda i, k: (0, 0)))
        args.append(final_g.reshape(1, d))
    return pl.pallas_call(
        functools.partial(_mlp_kernel, final_norm=final_g is not None),
        grid=(m // tm, hid // th),
        in_specs=in_specs,
        out_specs=pl.BlockSpec((tm, d), lambda i, k: (i, 0)),
        out_shape=jax.ShapeDtypeStruct((m, d), F32),
        scratch_shapes=[pltpu.VMEM((tm, d), BF16), pltpu.VMEM((tm, d), F32)],
        compiler_params=_cparams(("arbitrary", "arbitrary")),
    )(*args)


def _softmax_pv(scores, values, scale=1.0):
    m = functools.reduce(jnp.maximum, [jnp.max(s, axis=-1, keepdims=True) for s in scores])
    ps = [jnp.exp2((s - m) * (scale * LOG2E)) for s in scores]
    denom = functools.reduce(jnp.add, [jnp.sum(p, axis=-1, keepdims=True) for p in ps])
    o = functools.reduce(jnp.add, [_dot(p.astype(BF16), v) for p, v in zip(ps, values)])
    return o / denom


def _rope_rotate(x, cos, sin):
    x = x.astype(F32)
    lane = lax.broadcasted_iota(jnp.int32, x.shape, 1)
    partner = jnp.where((lane % 32) < 16, pltpu.roll(x, LANES - 16, 1), pltpu.roll(x, 16, 1))
    return x * cos + partner * sin


def _mla_lat_kernel(qn_ref, qpe_ref, cosq_ref, sinq_ref, knc_ref, kpec_ref, vc_ref, knl_ref, kpel_ref,
                    cosk_ref, sink_ref, vl_ref, o_ref, k_scr, v_scr, *, scale):
    lc = knc_ref.shape[0]

    @pl.when(pl.program_id(2) == 0)
    def _():
        k_scr[:lc, :LANES] = knc_ref[...]
        k_scr[:lc, LANES:] = kpec_ref[...]
        k_scr[lc:, :LANES] = knl_ref[...]
        k_scr[lc:, LANES:] = _rope_rotate(kpel_ref[...], cosk_ref[...], sink_ref[...]).astype(BF16)
        v_scr[:lc, :LANES] = vc_ref[...]
        v_scr[lc:, :LANES] = vl_ref[...]
        v_scr[:, LANES:] = jnp.ones((v_scr.shape[0], LANES), BF16)

    qpe = _rope_rotate(qpe_ref[...], cosq_ref[...], sinq_ref[...]).astype(BF16)
    q = jnp.concatenate([qn_ref[...], qpe], axis=1)
    sub = min(MLA_SUB_ROWS, q.shape[0])
    n_sub = q.shape[0] // sub
    k = k_scr[...]
    v = v_scr[...]

    def scores(i):
        return _dot_t(q[i * sub:(i + 1) * sub], k)

    def finish(i, s):
        p = jnp.exp2((s - jnp.max(s, axis=-1, keepdims=True)) * (scale * LOG2E)).astype(BF16)
        ol = _dot(p, v)
        o_ref[i * sub:(i + 1) * sub, :] = (ol[:, :LANES] / ol[:, LANES:]).astype(o_ref.dtype)

    s_cur = scores(0)
    for i in range(n_sub):
        s_next = scores(i + 1) if i + 1 < n_sub else None
        finish(i, s_cur)
        s_cur = s_next


def _mla_latent(p_lat, p_ctx, kv_lat, kv_ctx, cos_tab, sin_tab, batch, tq=2048):
    n = p_lat.shape[0] // batch
    lc = p_ctx.shape[0] // batch
    tq = min(tq, n)
    assert n % tq == 0 and tq % min(MLA_SUB_ROWS, tq) == 0
    nq = n // tq
    h = MLA_HEADS
    blk = lambda rows, f: pl.BlockSpec((rows, LANES), f)
    return pl.pallas_call(
        functools.partial(_mla_lat_kernel, scale=MLA_QK_DIM ** -0.5),
        grid=(batch, h, nq),
        in_specs=[
            blk(tq, lambda b, hh, i: (b * nq + i, EV_QMLA_BLK + 2 * hh)),
            blk(tq, lambda b, hh, i: (b * nq + i, EV_QMLA_BLK + 2 * hh + 1)),
            blk(tq, lambda b, hh, i: (i, 0)),
            blk(tq, lambda b, hh, i: (i, 0)),
            blk(lc, lambda b, hh, i: (b, 2 * hh)),
            blk(lc, lambda b, hh, i: (b, EV_KPE_BLK)),
            blk(lc, lambda b, hh, i: (b, 2 * hh + 1)),
            blk(n, lambda b, hh, i: (b, 2 * hh)),
            blk(n, lambda b, hh, i: (b, EV_KPE_BLK)),
            blk(n, lambda b, hh, i: (0, 0)),
            blk(n, lambda b, hh, i: (0, 0)),
            blk(n, lambda b, hh, i: (b, 2 * hh + 1)),
        ],
        out_specs=blk(tq, lambda b, hh, i: (b * nq + i, hh)),
        out_shape=jax.ShapeDtypeStruct((batch * n, h * LANES), BF16),
        scratch_shapes=[pltpu.VMEM((lc + n, 2 * LANES), BF16), pltpu.VMEM((lc + n, 2 * LANES), BF16)],
        compiler_params=_cparams(("arbitrary", "arbitrary", "arbitrary")),
    )(p_lat, p_lat, cos_tab, sin_tab, kv_ctx, p_ctx, kv_ctx, kv_lat, p_lat, cos_tab, sin_tab, kv_lat)


def _ctx_attn_kernel(qm_ref, kn_ref, kpe_ref, vm_ref, qn_ref, kna_ref, vna_ref, om_ref, on_ref,
                     *, mla_scale, na_scale):
    k = jnp.concatenate([kn_ref[...], kpe_ref[...]], axis=1)
    s = _dot_t(qm_ref[...], k)
    om_ref[...] = _softmax_pv([s], [vm_ref[...]], mla_scale).astype(om_ref.dtype)
    s = _dot_t(qn_ref[...], kna_ref[...])
    on_ref[...] = _softmax_pv([s], [vna_ref[...]], na_scale).astype(on_ref.dtype)


def _ctx_attention(p_ctx, kv_ctx, batch):
    lc = p_ctx.shape[0] // batch
    h = MLA_HEADS
    blk = lambda f: pl.BlockSpec((lc, LANES), f)
    out = jax.ShapeDtypeStruct((batch * lc, h * LANES), BF16)
    return pl.pallas_call(
        functools.partial(_ctx_attn_kernel, mla_scale=MLA_QK_DIM ** -0.5, na_scale=NA_DIM ** -0.5),
        grid=(batch, h),
        in_specs=[
            pl.BlockSpec((lc, 2 * LANES), lambda b, hh: (b, hh)),
            blk(lambda b, hh: (b, 2 * hh)),
            blk(lambda b, hh: (b, EV_KPE_BLK)),
            blk(lambda b, hh: (b, 2 * hh + 1)),
            blk(lambda b, hh: (b, EV_QNA_BLK + hh)),
            blk(lambda b, hh: (b, EV_KNA_BLK + hh)),
            blk(lambda b, hh: (b, EV_VNA_BLK + hh)),
        ],
        out_specs=[blk(lambda b, hh: (b, hh)), blk(lambda b, hh: (b, hh))],
        out_shape=[out, out],
        compiler_params=_cparams(("arbitrary", "arbitrary")),
    )(p_ctx, kv_ctx, p_ctx, kv_ctx, p_ctx, p_ctx, p_ctx)


def _na_kernel(q_ref, k_ref, v_ref, kc_ref, vc_ref, bias_ref, o_ref, s_scr, p_scr, l_scr, oc_scr,
               *, scale, n_rows):
    win = NA_KH * GRID_W
    slab = 256
    n = q_ref.shape[0]

    def band(r):
        ws = min(max(r - NA_KH // 2, 0), n_rows - NA_KH)
        return ws, slice(r * GRID_W, (r + 1) * GRID_W), slice(ws * GRID_W, ws * GRID_W + win)

    s_scr[:, win:] = _dot_t(q_ref[...], kc_ref[...]) * scale
    for r in range(n_rows):
        ws, rows, keys = band(r)
        s_scr[rows, :win] = _dot_t(q_ref[rows, :], k_ref[keys, :]) * scale + bias_ref[r - ws]

    def body(i, carry):
        sl = pl.ds(pl.multiple_of(i * slab, slab), slab)
        s = s_scr[sl, :]
        p = jnp.exp2((s - jnp.max(s, axis=-1, keepdims=True)) * LOG2E)
        p_scr[sl, :] = p.astype(BF16)
        l_scr[sl, :] = jnp.broadcast_to(1.0 / jnp.sum(p, axis=-1, keepdims=True), (slab, LANES))
        return carry

    lax.fori_loop(0, n // slab, body, 0, unroll=2)
    oc_scr[...] = _dot(p_scr[:, win:], vc_ref[...])
    for r in range(n_rows):
        ws, rows, keys = band(r)
        o = _dot(p_scr[rows, :win], v_ref[keys, :]) + oc_scr[rows, :]
        o_ref[rows, :] = (o * l_scr[rows, :]).astype(o_ref.dtype)


def _na_bias_kernel(rb_ref, onehot_ref, mask_ref, o_ref):
    o_ref[...] = _dot_f32(rb_ref[...], onehot_ref[...]) + mask_ref[...]


def _na_bias_table(rel_bias):
    n_heads, n_ro, n_co = rel_bias.shape
    col = np.arange(GRID_W)
    col_start = np.clip(col - NA_KW // 2, 0, GRID_W - NA_KW)
    col_mask = (col[None, :] >= col_start[:, None]) & (col[None, :] < col_start[:, None] + NA_KW)
    col_off = np.clip(col[None, :] - col[:, None], 1 - NA_KW, NA_KW - 1) + (NA_KW - 1)
    onehot = (col_off.reshape(1, -1) == np.arange(n_co)[:, None]).astype(np.float32)
    mask_add = np.where(col_mask.reshape(1, -1), 0.0, NEG_INF).astype(np.float32)
    qw = GRID_W * GRID_W
    full = lambda shape: pl.BlockSpec(shape, lambda: (0,) * len(shape))
    cols = pl.pallas_call(
        _na_bias_kernel,
        in_specs=[full((n_heads * n_ro, n_co)), full((n_co, qw)), full((1, qw))],
        out_specs=full((n_heads * n_ro, qw)),
        out_shape=jax.ShapeDtypeStruct((n_heads * n_ro, qw), F32),
    )(rel_bias.reshape(n_heads * n_ro, n_co), jnp.asarray(onehot), jnp.asarray(mask_add))
    cols = cols.reshape(n_heads, n_ro, GRID_W, GRID_W)
    t = jnp.stack([cols[:, NA_KH - 1 - e:2 * NA_KH - 1 - e] for e in range(NA_KH)], axis=1)
    return t.transpose(0, 1, 3, 2, 4).reshape(n_heads, NA_KH, GRID_W, NA_KH * GRID_W)


def _na_latent(p_lat, p_ctx, bias_tab, batch):
    n = p_lat.shape[0] // batch
    lc = p_ctx.shape[0] // batch
    h = NA_HEADS
    n_rows = n // GRID_W
    assert n_rows >= NA_KH
    blk = lambda rows, f: pl.BlockSpec((rows, LANES), f)
    return pl.pallas_call(
        functools.partial(_na_kernel, scale=NA_DIM ** -0.5, n_rows=n_rows),
        grid=(batch, h),
        in_specs=[
            blk(n, lambda b, hh: (b, EV_QNA_BLK + hh)),
            blk(n, lambda b, hh: (b, EV_KNA_BLK + hh)),
            blk(n, lambda b, hh: (b, EV_VNA_BLK + hh)),
            blk(lc, lambda b, hh: (b, EV_KNA_BLK + hh)),
            blk(lc, lambda b, hh: (b, EV_VNA_BLK + hh)),
            pl.BlockSpec((None, NA_KH, GRID_W, NA_KH * GRID_W), lambda b, hh: (hh, 0, 0, 0)),
        ],
        out_specs=blk(n, lambda b, hh: (b, hh)),
        out_shape=jax.ShapeDtypeStruct((batch * n, h * LANES), BF16),
        scratch_shapes=[pltpu.VMEM((n, NA_KH * GRID_W + lc), F32), pltpu.VMEM((n, NA_KH * GRID_W + lc), BF16),
                        pltpu.VMEM((n, LANES), F32), pltpu.VMEM((n, LANES), F32)],
        compiler_params=_cparams(("arbitrary", "arbitrary")),
    )(p_lat, p_lat, p_lat, p_ctx, p_ctx, bias_tab)


def _rope_tables(n):
    pos = np.arange(n)
    rows, cols = pos // GRID_W, pos % GRID_W
    half = MLA_ROPE_DIM // 2
    inv_freq = ROPE_THETA ** (-np.arange(0, half, 2, dtype=np.float64) / half)
    cos = np.zeros((n, LANES), np.float64)
    sin = np.zeros((n, LANES), np.float64)
    for base, p in ((0, rows), (half, cols)):
        ang = p[:, None].astype(np.float64) * inv_freq[None, :]
        q = half // 2
        cos[:, base:base + q] = np.cos(ang)
        cos[:, base + q:base + half] = np.cos(ang)
        sin[:, base:base + q] = -np.sin(ang)
        sin[:, base + q:base + half] = np.sin(ang)
    return jnp.asarray(cos, F32), jnp.asarray(sin, F32)


def _even_w_in(w):
    d = w.shape[0]
    z64 = jnp.zeros((d, LANES - MLA_ROPE_DIM), w.dtype)
    pieces = []
    for h in range(MLA_HEADS):
        pieces += [w[:, h * MLA_QK_DIM:h * MLA_QK_DIM + MLA_NOPE_DIM],
                   w[:, h * MLA_QK_DIM + MLA_NOPE_DIM:(h + 1) * MLA_QK_DIM], z64]
    q_end = MLA_HEADS * MLA_QK_DIM
    ckv_end = q_end + MLA_KV_RANK
    kpe_end = ckv_end + MLA_ROPE_DIM
    pieces += [w[:, kpe_end:], w[:, q_end:ckv_end], w[:, ckv_end:kpe_end], z64]
    out = jnp.concatenate(pieces, axis=1)
    pad = EV_WIDTH - out.shape[1]
    return jnp.concatenate([out, jnp.zeros((d, pad), w.dtype)], axis=1).astype(BF16)


def _hgrn_gates(q_ref, z_refs, lbs, row0, qb_s, cum_s, cpk_s, qd_s, kd_s, dec_s):
    c = HGRN_CHUNK
    grp = LANES
    ri = lax.broadcasted_iota(jnp.int32, (grp, grp), 0)
    ci = lax.broadcasted_iota(jnp.int32, (grp, grp), 1)
    same = (ri // c) == (ci // c)
    blk = jnp.where(same, 1.0, 0.0).astype(BF16)
    tris = (jnp.where(jnp.logical_and(same, ci <= ri), 1.0, 0.0).astype(BF16),
            jnp.where(jnp.logical_and(same, ci >= ri), 1.0, 0.0).astype(BF16))

    def body(g, carry):
        src = pl.ds(pl.multiple_of(g * grp, grp), grp)
        dst = pl.ds(pl.multiple_of(row0 + g * grp, grp), grp)
        q = _silu(q_ref[src, :].astype(F32))
        qb_s[dst, :] = q.astype(BF16)
        for d in range(2):
            f = jnp.maximum(lbs[d] + (1.0 - lbs[d]) * _sigmoid(z_refs[d][src, :].astype(F32)), FORGET_FLOOR)
            lf = jnp.log(f)
            k = 1.0 - f
            cum = _dot_01(tris[d], lf)
            tot = _dot_01(blk, lf)
            cum_s[d, dst, :] = cum * LOG2E
            cpk_s[d, dst, :] = (cum - jnp.log(k)) * LOG2E
            qd_s[d, dst, :] = (q * jnp.exp(cum)).astype(BF16)
            kd_s[d, dst, :] = (k * jnp.exp(tot - cum)).astype(BF16)
            dec_s[d, dst, :] = jnp.exp(tot)
        return carry

    lax.fori_loop(0, q_ref.shape[0] // grp, body, 0)


def _hgrn_chunk(qb, cum2, cpk2, v, qd, st, ones_bf, reverse):
    c = HGRN_CHUNK
    hc = c // 2
    o = _dot_t(qd, st.astype(BF16))
    rows = lax.broadcasted_iota(jnp.int32, (hc, LANES), 0)
    halves = (cum2[:hc], cum2[hc:])
    zero = jnp.zeros((hc, LANES), F32)
    pieces = []
    for s in range(c):
        ref = cpk2[s:s + 1]
        hs, rs = divmod(s, hc)
        es = []
        for hh in range(2):
            if hh == hs:
                mask = (rows <= rs) if reverse else (rows >= rs)
                es.append(jnp.exp2(jnp.where(mask, halves[hh] - ref, NEG_INF)))
            elif (hh > hs) != reverse:
                es.append(jnp.exp2(halves[hh] - ref))
            else:
                es.append(zero)
        pieces.append(jnp.concatenate(es, axis=0).astype(BF16) * qb)
    lhs = jnp.concatenate([jnp.concatenate(pieces[:hc], axis=0), jnp.concatenate(pieces[hc:], axis=0)], axis=1)
    r = _dot(lhs, ones_bf)
    for s in range(c):
        hs, rs = divmod(s, hc)
        o = o + r[rs * c:(rs + 1) * c, hs * LANES:(hs + 1) * LANES] * v[s:s + 1]
    return o


def _hgrn_gates_fast(q_ref, i_ref, z_refs, lbs, row0, worst, v_s, qf_s, kf_s, qdf_s, kdf_s, decf_s):
    c = HGRN_FAST_CHUNK
    grp = LANES
    ri = lax.broadcasted_iota(jnp.int32, (grp, grp), 0)
    ci = lax.broadcasted_iota(jnp.int32, (grp, grp), 1)
    same = (ri // c) == (ci // c)
    half = c // 2
    sums = []
    for fwd in (True, False):
        tri = (ci <= ri) if fwd else (ci >= ri)
        upto_ref = ((ci % c) <= half) if fwd else ((ci % c) >= half)
        rows = [jnp.logical_and(same, tri), same, jnp.logical_and(same, upto_ref)]
        sums.append(jnp.concatenate([jnp.where(m, 1.0, 0.0) for m in rows], axis=0).astype(BF16))

    def body(g, worst):
        src = pl.ds(pl.multiple_of(g * grp, grp), grp)
        dst = pl.ds(pl.multiple_of(row0 + g * grp, grp), grp)
        q = _silu(q_ref[src, :].astype(F32))
        v_s[dst, :] = i_ref[src, :].astype(F32)
        fs = [jnp.maximum(lbs[d] + (1.0 - lbs[d]) * _sigmoid(z_refs[d][src, :].astype(F32)), FORGET_FLOOR)
              for d in range(2)]
        res = [_dot_01(sums[d], jnp.log(fs[d])) for d in range(2)]
        for d in range(2):
            k = 1.0 - fs[d]
            cum, tot, ref = res[d][:grp], res[d][grp:2 * grp], res[d][2 * grp:]
            qf_s[d, dst, :] = (q * jnp.exp(cum - ref)).astype(BF16)
            kf_s[d, dst, :] = (k * jnp.exp(ref - cum)).astype(BF16)
            qdf_s[d, dst, :] = (q * jnp.exp(cum)).astype(BF16)
            kdf_s[d, dst, :] = (k * jnp.exp(tot - cum)).astype(BF16)
            decf_s[d, dst, :] = jnp.exp(tot)
            worst = jnp.maximum(worst, -tot)
        return worst

    return lax.fori_loop(0, q_ref.shape[0] // grp, body, worst, unroll=2)


def _hgrn_kernel(ql_ref, il_ref, zfl_ref, zbl_ref, gl_ref, qc_ref, ic_ref, zfc_ref, zbc_ref, gc_ref,
                 lbl_ref, ng_ref, yl_ref, yc_ref,
                 qb_s, v_s, cum_s, cpk_s, qd_s, kd_s, dec_s, o_s, qf_s, kf_s, qdf_s, kdf_s, decf_s, *, layer):
    lc, n = qc_ref.shape[0], ql_ref.shape[0]
    lbs = []
    for d in range(2):
        lg = lbl_ref[d]
        ex = jnp.exp(lg - jnp.max(lg, axis=0, keepdims=True))
        p = ex / jnp.sum(ex, axis=0, keepdims=True)
        lbs.append(jnp.sum(p[:layer + 1], axis=0, keepdims=True) - p[0:1])

    zero = jnp.zeros((LANES, LANES), F32)

    def chunk_rows(j, c):
        r_fwd = j * c
        r_bwd = jnp.where(j < lc // c, lc - c - j * c, 2 * lc + n - c - j * c)
        return [pl.ds(pl.multiple_of(r, c), c) for r in (r_fwd, r_bwd)]

    fast_scr = (v_s, qf_s, kf_s, qdf_s, kdf_s, decf_s)
    worst = _hgrn_gates_fast(qc_ref, ic_ref, (zfc_ref, zbc_ref), lbs, 0, zero, *fast_scr)
    worst = _hgrn_gates_fast(ql_ref, il_ref, (zfl_ref, zbl_ref), lbs, lc, worst, *fast_scr)
    fast_ok = jnp.max(worst) <= HGRN_FAST_MAX_DECAY

    @pl.when(fast_ok)
    def _():
        c = HGRN_FAST_CHUNK
        ri = lax.broadcasted_iota(jnp.int32, (c, c), 0)
        ci = lax.broadcasted_iota(jnp.int32, (c, c), 1)
        causal = (ci <= ri, ci >= ri)

        def body(g, sts):
            slices = [chunk_rows(g * HGRN_FAST_UNROLL + i, c) for i in range(HGRN_FAST_UNROLL)]
            scores = [[jnp.where(causal[d], _dot_t(qf_s[d, sl[d], :], kf_s[d, sl[d], :]), 0.0).astype(BF16)
                       for d in range(2)] for sl in slices]
            incs = [[_dot_tn(v_s[sl[d], :].astype(BF16), kdf_s[d, sl[d], :]) for d in range(2)] for sl in slices]
            local = [[_dot(scores[i][d], v_s[sl[d], :].astype(BF16)) for d in range(2)]
                     for i, sl in enumerate(slices)]
            sts = list(sts)
            for i, sl2 in enumerate(slices):
                for d in range(2):
                    sl = sl2[d]
                    o_s[d, sl, :] = _dot_t(qdf_s[d, sl, :], sts[d].astype(BF16)) + local[i][d]
                    sts[d] = sts[d] * decf_s[d, sl, :][0:1] + incs[i][d]
            return tuple(sts)

        n_chunks = (lc + n) // c
        assert n_chunks % HGRN_FAST_UNROLL == 0
        lax.fori_loop(0, n_chunks // HGRN_FAST_UNROLL, body, (zero, zero))

    @pl.when(jnp.logical_not(fast_ok))
    def _():
        c = HGRN_CHUNK
        scr = (qb_s, cum_s, cpk_s, qd_s, kd_s, dec_s)
        _hgrn_gates(qc_ref, (zfc_ref, zbc_ref), lbs, 0, *scr)
        _hgrn_gates(ql_ref, (zfl_ref, zbl_ref), lbs, lc, *scr)
        ri = lax.broadcasted_iota(jnp.int32, (2 * LANES, 2 * LANES), 0)
        ci = lax.broadcasted_iota(jnp.int32, (2 * LANES, 2 * LANES), 1)
        ones_bf = jnp.where((ri // LANES) == (ci // LANES), 1.0, 0.0).astype(BF16)

        def body(g, sts):
            slices = [chunk_rows(g * HGRN_UNROLL + i, c) for i in range(HGRN_UNROLL)]
            incs = [[_dot_tn(v_s[sl[d], :].astype(BF16), kd_s[d, sl[d], :]) for d in range(2)] for sl in slices]
            sts = list(sts)
            for i, sl2 in enumerate(slices):
                for d in range(2):
                    sl = sl2[d]
                    o_s[d, sl, :] = _hgrn_chunk(qb_s[sl, :], cum_s[d, sl, :], cpk_s[d, sl, :], v_s[sl, :],
                                                qd_s[d, sl, :], sts[d], ones_bf, reverse=d == 1)
                    sts[d] = sts[d] * dec_s[d, sl, :][0:1] + incs[i][d]
            return tuple(sts)

        n_chunks = (lc + n) // c
        assert n_chunks % HGRN_UNROLL == 0
        lax.fori_loop(0, n_chunks // HGRN_UNROLL, body, (zero, zero))

    ng = ng_ref[...]
    yc_ref[...] = (_rms(o_s[0, :lc, :] + o_s[1, :lc, :]) * ng
                   * _silu(gc_ref[...].astype(F32))).astype(yc_ref.dtype)
    yl_ref[...] = (_rms(o_s[0, lc:, :] + o_s[1, lc:, :]) * ng
                   * _silu(gl_ref[...].astype(F32))).astype(yl_ref.dtype)


def _hgrn2(p_lat, p_ctx, lb_logits, norm_g, layer, batch):
    n = p_lat.shape[0] // batch
    lc = p_ctx.shape[0] // batch
    h = HGRN_HEADS
    n_layers = lb_logits.shape[1]
    rows = lc + n
    assert lc % LANES == 0 and n % LANES == 0
    lat = lambda part: pl.BlockSpec((n, LANES), lambda b, hh: (b, part * h + hh))
    ctx = lambda part: pl.BlockSpec((lc, LANES), lambda b, hh: (b, part * h + hh))
    return pl.pallas_call(
        functools.partial(_hgrn_kernel, layer=layer),
        grid=(batch, h),
        in_specs=[lat(0), lat(1), lat(2), lat(3), lat(4), ctx(0), ctx(1), ctx(2), ctx(3), ctx(4),
                  pl.BlockSpec((2, n_layers, LANES), lambda b, hh: (0, 0, hh)),
                  pl.BlockSpec((1, LANES), lambda b, hh: (0, hh))],
        out_specs=[pl.BlockSpec((n, LANES), lambda b, hh: (b, hh)),
                   pl.BlockSpec((lc, LANES), lambda b, hh: (b, hh))],
        out_shape=[jax.ShapeDtypeStruct((batch * n, h * LANES), BF16),
                   jax.ShapeDtypeStruct((batch * lc, h * LANES), BF16)],
        scratch_shapes=[pltpu.VMEM((rows, LANES), BF16), pltpu.VMEM((rows, LANES), F32),
                        pltpu.VMEM((2, rows, LANES), F32), pltpu.VMEM((2, rows, LANES), F32),
                        pltpu.VMEM((2, rows, LANES), BF16), pltpu.VMEM((2, rows, LANES), BF16),
                        pltpu.VMEM((2, rows, LANES), F32), pltpu.VMEM((2, rows, LANES), F32),
                        pltpu.VMEM((2, rows, LANES), BF16), pltpu.VMEM((2, rows, LANES), BF16),
                        pltpu.VMEM((2, rows, LANES), BF16), pltpu.VMEM((2, rows, LANES), BF16),
                        pltpu.VMEM((2, rows, LANES), F32)],
        compiler_params=_cparams(("arbitrary", "arbitrary")),
    )(p_lat, p_lat, p_lat, p_lat, p_lat, p_ctx, p_ctx, p_ctx, p_ctx, p_ctx,
      lb_logits, norm_g[layer].reshape(1, -1))


def _dft_matrices(n):
    idx = (np.arange(n)[:, None] * np.arange(n)[None, :]) % (2 * n)
    ang = idx.astype(np.float64) * (math.pi / n)
    cm = np.cos(ang)
    sf = np.sin(ang)
    sf[0, :] = (-1.0) ** np.arange(n)
    return (jnp.asarray(cm, F32).astype(BF16), jnp.asarray(sf, F32).astype(BF16),
            jnp.asarray(sf.T, F32).astype(BF16))


def _filter_features(n):
    pos = np.arange(n, dtype=np.float64)
    t = pos / max(n - 1, 1)
    bands = np.linspace(1e-4, HYENA_BANDS - 1, HYENA_BANDS)
    ang = (2.0 * math.pi / n) * pos[:, None] * bands[None, :]
    z = np.concatenate([t[:, None], np.cos(ang), -np.sin(ang)], -1)
    max_decay = math.log(HYENA_DECAY_TARGET) / HYENA_FAST_PCT
    min_decay = math.log(HYENA_DECAY_TARGET) / HYENA_SLOW_PCT
    deltas = np.abs(np.linspace(min_decay, max_decay, HYENA_WIDTH))
    return jnp.asarray(z, F32), jnp.asarray(t[:, None], F32), jnp.asarray(deltas[None, :], F32)


def _filter_kernel(z_ref, t_ref, dl_ref, w1_ref, b1_ref, w2_ref, b2_ref, w3_ref, b3_ref, fr_ref, wo_ref,
                   o_ref, hdn_ref):
    j = pl.program_id(0)

    @pl.when(j == 0)
    def _():
        fr = fr_ref[...]
        hdn = jnp.sin(fr * (_dot_f32(z_ref[...], w1_ref[...]) + b1_ref[...]))
        hdn = jnp.sin(fr * (_dot_f32(hdn, w2_ref[...]) + b2_ref[...]))
        hdn_ref[...] = jnp.sin(fr * (_dot_f32(hdn, w3_ref[...]) + b3_ref[...]))

    filt = _dot_f32(hdn_ref[...], wo_ref[...]) * jnp.exp(-t_ref[...] * dl_ref[...])
    row = lax.broadcasted_iota(jnp.int32, filt.shape, 0)
    is_bwd = j >= pl.num_programs(0) // 2
    o_ref[...] = jnp.where(jnp.logical_and(is_bwd, row == 0), 0.0, filt).astype(o_ref.dtype)


def _hyena_filters(n, w1, b1, w2, b2, w3, b3, freq, w_out, tc=512):
    z, t, deltas = _filter_features(n)
    hid = HYENA_FILT_HIDDEN
    nct = HYENA_WIDTH // tc
    full = lambda shape: pl.BlockSpec(shape, lambda j: (0,) * len(shape))
    return pl.pallas_call(
        _filter_kernel,
        grid=(2 * nct,),
        in_specs=[full((n, HYENA_EMB)), full((n, 1)),
                  pl.BlockSpec((1, tc), lambda j: (0, j % nct)),
                  full((HYENA_EMB, hid)), full((1, hid)), full((hid, hid)), full((1, hid)),
                  full((hid, hid)), full((1, hid)), full((1, hid)),
                  pl.BlockSpec((hid, tc), lambda j: (0, j))],
        out_specs=pl.BlockSpec((n, tc), lambda j: (0, j)),
        out_shape=jax.ShapeDtypeStruct((n, 2 * HYENA_WIDTH), BF16),
        scratch_shapes=[pltpu.VMEM((n, hid), F32)],
        compiler_params=_cparams(("arbitrary",)),
    )(z, t, deltas, w1, b1.reshape(1, hid), w2, b2.reshape(1, hid), w3, b3.reshape(1, hid),
      freq.reshape(1, hid), w_out)


def _spectrum_kernel(cm_ref, sf_ref, hf_ref, hb_ref, a_ref, b_ref, *, inv_len):
    cm, sf, hf, hb = cm_ref[...], sf_ref[...], hf_ref[...], hb_ref[...]
    kr = _dot(cm, hf) + _dot(cm, hb)
    d1 = _dot(sf, hf)
    d2 = _dot(sf, hb)
    row = lax.broadcasted_iota(jnp.int32, kr.shape, 0) + pl.program_id(0) * kr.shape[0]
    first = row == 0
    w = jnp.where(first, inv_len, 2.0 * inv_len)
    a_ref[...] = kr * w
    b_ref[...] = jnp.where(first, d1 + d2, d1 - d2) * w


def _filter_spectrum(filt, cm, sf, tk=512, tc=512):
    n = filt.shape[0]
    tk = min(tk, n)
    nct = HYENA_WIDTH // tc
    out = jax.ShapeDtypeStruct((n, HYENA_WIDTH), F32)
    return pl.pallas_call(
        functools.partial(_spectrum_kernel, inv_len=1.0 / (2 * n)),
        grid=(n // tk, nct),
        in_specs=[pl.BlockSpec((tk, n), lambda i, j: (i, 0)),
                  pl.BlockSpec((tk, n), lambda i, j: (i, 0)),
                  pl.BlockSpec((n, tc), lambda i, j: (0, j)),
                  pl.BlockSpec((n, tc), lambda i, j: (0, nct + j))],
        out_specs=[pl.BlockSpec((tk, tc), lambda i, j: (i, j)), pl.BlockSpec((tk, tc), lambda i, j: (i, j))],
        out_shape=[out, out],
        compiler_params=_cparams(("arbitrary", "arbitrary")),
    )(cm, sf, filt, filt)


def _hyena_gate_kernel(u0_ref, u1_ref, uv_ref, w0_ref, w1_ref, wv_ref, b0_ref, b1_ref, bv_ref,
                       x0_ref, z_ref):
    n = u0_ref.shape[0]
    row = lax.broadcasted_iota(jnp.int32, u0_ref.shape, 0)

    def conv(u_ref, w_ref, b_ref):
        u = u_ref[...].astype(F32)
        prev = jnp.where(row == 0, 0.0, pltpu.roll(u, 1, 0))
        nxt = jnp.where(row == n - 1, 0.0, pltpu.roll(u, n - 1, 0))
        return b_ref[...] + prev * w_ref[0:1] + u * w_ref[1:2] + nxt * w_ref[2:3]

    x0_ref[...] = conv(u0_ref, w0_ref, b0_ref).astype(x0_ref.dtype)
    z_ref[...] = (conv(uv_ref, wv_ref, bv_ref) * conv(u1_ref, w1_ref, b1_ref)).astype(z_ref.dtype)


def _hyena_gate(p, first_blk, conv_w, conv_b, batch, tc=256):
    n = p.shape[0] // batch
    nct = HYENA_WIDTH // tc
    c0 = first_blk * LANES // tc
    u = lambda part: pl.BlockSpec((n, tc), lambda b, j: (b, c0 + part * nct + j))
    w = lambda part: pl.BlockSpec((HYENA_SHORT, tc), lambda b, j: (0, part * nct + j))
    bb = lambda part: pl.BlockSpec((1, tc), lambda b, j: (0, part * nct + j))
    out = jax.ShapeDtypeStruct((batch * n, HYENA_WIDTH), BF16)
    cb = conv_b.reshape(1, -1)
    return pl.pallas_call(
        _hyena_gate_kernel,
        grid=(batch, nct),
        in_specs=[u(0), u(1), u(2), w(0), w(1), w(2), bb(0), bb(1), bb(2)],
        out_specs=[pl.BlockSpec((n, tc), lambda b, j: (b, j)), pl.BlockSpec((n, tc), lambda b, j: (b, j))],
        out_shape=[out, out],
        compiler_params=_cparams(("arbitrary", "arbitrary")),
    )(p, p, p, conv_w, conv_w, conv_w, cb, cb, cb)


def _dft_fwd_kernel(cm_ref, sf_ref, z_ref, a_ref, b_ref, pr_ref, ps_ref):
    z = z_ref[...]
    zr = _dot(cm_ref[...], z)
    zs = _dot(sf_ref[...], z)
    a, b = a_ref[...], b_ref[...]
    row = lax.broadcasted_iota(jnp.int32, zr.shape, 0) + pl.program_id(1) * zr.shape[0]
    first = row == 0
    pr_ref[...] = (zr * a - jnp.where(first, 0.0, zs * b)).astype(pr_ref.dtype)
    ps_ref[...] = (jnp.where(first, 0.0, zr * b) + zs * jnp.where(first, b, a)).astype(ps_ref.dtype)


def _dft_inv_kernel(cm_ref, si_ref, pr_ref, ps_ref, z_ref, x0_ref, skip_ref, o_ref):
    y = _dot(cm_ref[...], pr_ref[...]) + _dot(si_ref[...], ps_ref[...])
    z = z_ref[...].astype(F32)
    o_ref[...] = (x0_ref[...].astype(F32) * (y + z * skip_ref[...])).astype(o_ref.dtype)


def _long_conv(z, x0, spec_a, spec_b, skip, cm, sf, si, batch, tk=1024, tc=512):
    n = z.shape[0] // batch
    tk = min(tk, n)
    nk = n // tk
    nct = HYENA_WIDTH // tc
    mat = pl.BlockSpec((tk, n), lambda b, i, j: (i, 0))
    col = pl.BlockSpec((n, tc), lambda b, i, j: (b, j))
    tile_nb = pl.BlockSpec((tk, tc), lambda b, i, j: (i, j))
    tile = pl.BlockSpec((tk, tc), lambda b, i, j: (b * nk + i, j))
    spec_shape = jax.ShapeDtypeStruct((batch * n, HYENA_WIDTH), BF16)
    pr, ps = pl.pallas_call(
        _dft_fwd_kernel,
        grid=(batch, nk, nct),
        in_specs=[mat, mat, col, tile_nb, tile_nb],
        out_specs=[tile, tile],
        out_shape=[spec_shape, spec_shape],
        compiler_params=_cparams(("arbitrary", "arbitrary", "arbitrary")),
    )(cm, sf, z, spec_a, spec_b)
    return pl.pallas_call(
        _dft_inv_kernel,
        grid=(batch, nk, nct),
        in_specs=[mat, mat, col, col, tile, tile, pl.BlockSpec((1, tc), lambda b, i, j: (0, j))],
        out_specs=tile,
        out_shape=jax.ShapeDtypeStruct((batch * n, HYENA_WIDTH), BF16),
        compiler_params=_cparams(("arbitrary", "arbitrary", "arbitrary")),
    )(cm, si, pr, ps, z, x0, skip.reshape(1, -1))


def _hyena(p, first_blk, conv_w, conv_b, filt_params, skip, batch):
    n = p.shape[0] // batch
    cm, sf, si = _dft_matrices(n)
    filt = _hyena_filters(n, *filt_params)
    spec_a, spec_b = _filter_spectrum(filt, cm, sf)
    x0, z = _hyena_gate(p, first_blk, conv_w, conv_b, batch)
    return _long_conv(z, x0, spec_a, spec_b, skip, cm, sf, si, batch)


def kernel(x, c, ctx, c_ctx, ada_w, ada_b, norm_mix_g, norm_mlp_g, w_out, mlp_w1, mlp_w2, final_norm_g, ev_w_in, mla_kv_norm_g, mla_w_ukv, na_rel_bias, od_w_in, hgrn_lb_logits, hgrn_norm_g, hy_conv_w, hy_conv_b, hy_filt_w1, hy_filt_b1, hy_filt_w2, hy_filt_b2, hy_filt_w3, hy_filt_b3, hy_filt_freq, hy_filt_wout, hy_skip):
    batch, seq, d = x.shape
    lc = ctx.shape[1]
    depth = ada_w.shape[0]
    h_lat = x.reshape(batch * seq, d)
    h_ctx = ctx.reshape(batch * lc, d)

    cond = jnp.concatenate([c, c_ctx[None, :], jnp.zeros((8 - batch - 1, d), F32)], axis=0)
    mod_all = _ada_modulation(cond, ada_w, ada_b)
    cos_tab, sin_tab = _rope_tables(seq)
    hgrn_cols = 5 * HGRN_WIDTH
    od_w_in, w_out, mlp_w1, mlp_w2, mla_w_ukv = (
        t.astype(BF16) for t in (od_w_in, w_out, mlp_w1, mlp_w2, mla_w_ukv))

    for l in range(depth):
        ctx_out = l < depth - 1
        mod3 = mod_all[l].reshape(8, 1, 6 * d)
        lat_mod = lambda *chunks: (mod3, *chunks, seq, 0)
        ctx_mod = lambda *chunks: (mod3, *chunks, batch * lc, batch)
        if l % 2 == 0:
            e = l // 2
            w_in = _even_w_in(ev_w_in[e])
            p_lat = _norm_proj(h_lat, 0, d, norm_mix_g[l], w_in, BF16, lat_mod(0, 1))
            p_ctx = _norm_proj(h_ctx, 0, d, norm_mix_g[l], w_in, BF16, ctx_mod(0, 1))
            ckv_blk = EV_CKV_BLK * LANES // MLA_KV_RANK
            kv_lat = _norm_proj(p_lat, ckv_blk, MLA_KV_RANK, mla_kv_norm_g[e], mla_w_ukv, BF16, layer=e)
            kv_ctx = _norm_proj(p_ctx, ckv_blk, MLA_KV_RANK, mla_kv_norm_g[e], mla_w_ukv, BF16, layer=e)
            y1_lat = _mla_latent(p_lat, p_ctx, kv_lat, kv_ctx, cos_tab, sin_tab, batch)
            y2_lat = _na_latent(p_lat, p_ctx, _na_bias_table(na_rel_bias[e]), batch)
            if ctx_out:
                y1_ctx, y2_ctx = _ctx_attention(p_ctx, kv_ctx, batch)
        else:
            o = l // 2
            p_lat = _norm_proj(h_lat, 0, d, norm_mix_g[l], od_w_in, BF16, lat_mod(0, 1), layer=o)
            p_ctx = _norm_proj(h_ctx, 0, d, norm_mix_g[l], od_w_in, BF16, ctx_mod(0, 1), layer=o,
                               n=None if ctx_out else hgrn_cols)
            y1_lat, y1_ctx = _hgrn2(p_lat, p_ctx, hgrn_lb_logits, hgrn_norm_g, o, batch)
            filt_params = (hy_filt_w1[o], hy_filt_b1[o], hy_filt_w2[o], hy_filt_b2[o], hy_filt_w3[o],
                           hy_filt_b3[o], hy_filt_freq[o], hy_filt_wout[o])
            y2_lat = _hyena(p_lat, hgrn_cols // LANES, hy_conv_w[o], hy_conv_b[o], filt_params,
                            hy_skip[o], batch)
            if ctx_out:
                y2_ctx = _hyena(p_ctx, hgrn_cols // LANES, hy_conv_w[o], hy_conv_b[o], filt_params,
                                hy_skip[o], batch)
        h_lat = _out_proj(y1_lat, y2_lat, w_out, l, h_lat, lat_mod(2))
        h_lat = _mlp(h_lat, norm_mlp_g[l], mlp_w1, mlp_w2, l, lat_mod(3, 4, 5),
                     final_g=None if ctx_out else final_norm_g)
        if ctx_out:
            h_ctx = _out_proj(y1_ctx, y2_ctx, w_out, l, h_ctx, ctx_mod(2))
            h_ctx = _mlp(h_ctx, norm_mlp_g[l], mlp_w1, mlp_w2, l, ctx_mod(3, 4, 5))
    return h_lat.reshape(batch, seq, d)
```

```python
import functools
import math

import numpy as np
import jax
import jax.numpy as jnp
from jax import lax
from jax.experimental import pallas as pl
from jax.experimental.pallas import tpu as pltpu

F32 = jnp.float32
BF16 = jnp.bfloat16

D_MODEL = 2048
DEPTH = 4
GRID_W = 64
HEAD_DIM = 128
MLA_HEADS = 8
MLA_NOPE_DIM = 128
MLA_ROPE_DIM = 64
MLA_QK_DIM = MLA_NOPE_DIM + MLA_ROPE_DIM
MLA_KV_RANK = 512
NA_HEADS = 8
NA_DIM = 128
NA_KH = 8
NA_KW = 16
ROPE_THETA = 10000.0
HGRN_WIDTH = 1024
HGRN_HEADS = 8
FORGET_FLOOR = 1e-30
HYENA_WIDTH = 1024
HYENA_SHORT = 3
HYENA_EMB = 33
HYENA_BANDS = (HYENA_EMB - 1) // 2
HYENA_FILT_HIDDEN = 64
HYENA_DECAY_TARGET = 1e-2
HYENA_FAST_PCT = 0.3
HYENA_SLOW_PCT = 1.5
MLP_HIDDEN = 4 * D_MODEL
NORM_EPS = 1e-6
NEG_INF = -1e30
LOG2E = 1.4426950408889634

LANES = 128
VMEM_LIMIT_BYTES = 56 * 1024 * 1024

EV_QMLA_BLK = 0
EV_QNA_BLK = 16
EV_KNA_BLK = 24
EV_VNA_BLK = 32
EV_CKV_BLK = 40
EV_KPE_BLK = 44
EV_WIDTH = 48 * LANES
MLA_SUB_ROWS = 512
HGRN_CHUNK = 16
HGRN_UNROLL = 8
HGRN_FAST_CHUNK = 32
HGRN_FAST_UNROLL = 8
HGRN_FAST_MAX_DECAY = 80.0


def _cparams(sem):
    return pltpu.CompilerParams(dimension_semantics=sem, vmem_limit_bytes=VMEM_LIMIT_BYTES)


def _dot(a, b):
    return jnp.dot(a, b, preferred_element_type=F32)


def _dot_t(a, b):
    return lax.dot_general(a, b, (((1,), (1,)), ((), ())), preferred_element_type=F32)


def _dot_tn(a, b):
    return lax.dot_general(a, b, (((0,), (0,)), ((), ())), preferred_element_type=F32)


def _dot_f32(a, b):
    return jnp.dot(a, b, preferred_element_type=F32, precision=lax.Precision.HIGHEST)


def _dot_01(m01, x):
    hi = x.astype(BF16)
    lo = (x - hi.astype(F32)).astype(BF16)
    return _dot(m01, hi) + _dot(m01, lo)


def _sigmoid(x):
    return 1.0 / (1.0 + jnp.exp(-x))


def _silu(x):
    return x * _sigmoid(x)


def _rms(x):
    return x * lax.rsqrt(jnp.mean(x * x, axis=-1, keepdims=True) + NORM_EPS)


def _ada_kernel(s_ref, w_ref, b_ref, o_ref):
    s = _silu(s_ref[...]).astype(BF16)
    o_ref[...] = _dot(s, w_ref[...].astype(BF16)) + b_ref[...]


def _ada_modulation(cond, ada_w, ada_b, tn=1024):
    depth, d, n = ada_w.shape
    rows = cond.shape[0]
    return pl.pallas_call(
        _ada_kernel,
        grid=(depth, n // tn),
        in_specs=[
            pl.BlockSpec((rows, d), lambda l, j: (0, 0)),
            pl.BlockSpec((None, d, tn), lambda l, j: (l, 0, j)),
            pl.BlockSpec((None, 1, tn), lambda l, j: (l, 0, j)),
        ],
        out_specs=pl.BlockSpec((None, rows, tn), lambda l, j: (l, 0, j)),
        out_shape=jax.ShapeDtypeStruct((depth, rows, n), F32),
        compiler_params=_cparams(("arbitrary", "arbitrary")),
    )(cond, ada_w, ada_b.reshape(depth, 1, n))


def _mod_spec(chunk, tiles_per_group, group0, d):
    return pl.BlockSpec((None, 1, d), lambda i, *_: (group0 + i // tiles_per_group, 0, chunk))


def _proj_kernel(*refs, modulated, kv_tile):
    refs = list(refs)
    x_ref, g_ref = refs[:2]
    rest = refs[2:]
    if modulated:
        sh_ref, sc_ref = rest[:2]
        rest = rest[2:]
    w_ref = rest[0]
    rest = rest[1:]
    if kv_tile is not None:
        kvg_ref, wkv_ref, o_ref, kv_ref, a_ref = rest
    else:
        o_ref, a_ref = rest

    @pl.when(pl.program_id(1) == 0)
    def _():
        y = _rms(x_ref[...].astype(F32)) * g_ref[...]
        if modulated:
            y = y * (1.0 + sc_ref[...]) + sh_ref[...]
        a_ref[...] = y.astype(BF16)

    acc = _dot(a_ref[...], w_ref[...])
    o_ref[...] = acc.astype(o_ref.dtype)

    if kv_tile is not None:
        @pl.when(pl.program_id(1) == kv_tile)
        def _():
            rank = kvg_ref.shape[1]
            ckv = _rms(acc[:, :rank].astype(o_ref.dtype).astype(F32)) * kvg_ref[...]
            kv_ref[...] = _dot(ckv.astype(BF16), wkv_ref[...]).astype(kv_ref.dtype)


def _layer_spec(w, layer, block, index_map):
    if w.ndim == 2:
        return pl.BlockSpec(block, index_map)
    return pl.BlockSpec((None,) + block, lambda *idx: (layer,) + index_map(*idx))


def _norm_proj(x, x_col_blk, k, g, w, out_dtype, mod=None, layer=None, n=None, kv=None, tm=1024, tn=1024):
    m = x.shape[0]
    n = w.shape[-1] if n is None else n
    tm = min(tm, m)
    tn = min(tn, n)
    assert m % tm == 0 and n % tn == 0
    in_specs = [pl.BlockSpec((tm, k), lambda i, j: (i, x_col_blk)),
                pl.BlockSpec((1, k), lambda i, j: (0, 0))]
    args = [x, g.reshape(1, k)]
    if mod is not None:
        mod3, sh_chunk, sc_chunk, rows_per_group, group0 = mod
        assert rows_per_group % tm == 0
        in_specs += [_mod_spec(sh_chunk, rows_per_group // tm, group0, k),
                     _mod_spec(sc_chunk, rows_per_group // tm, group0, k)]
        args += [mod3, mod3]
    in_specs.append(_layer_spec(w, layer, (k, tn), lambda i, j: (0, j)))
    args.append(w)
    out_specs = pl.BlockSpec((tm, tn), lambda i, j: (i, j))
    out_shape = jax.ShapeDtypeStruct((m, n), out_dtype)
    kv_tile = None
    if kv is not None:
        kv_col, kv_g, w_kv, kv_layer = kv
        rank, n_kv = w_kv.shape[-2:]
        assert kv_col % tn == 0 and rank <= tn
        kv_tile = kv_col // tn
        in_specs += [pl.BlockSpec((1, rank), lambda i, j: (0, 0)),
                     _layer_spec(w_kv, kv_layer, (rank, n_kv), lambda i, j: (0, 0))]
        args += [kv_g.reshape(1, rank), w_kv]
        out_specs = [out_specs, pl.BlockSpec((tm, n_kv), lambda i, j: (i, 0))]
        out_shape = [out_shape, jax.ShapeDtypeStruct((m, n_kv), out_dtype)]
    return pl.pallas_call(
        functools.partial(_proj_kernel, modulated=mod is not None, kv_tile=kv_tile),
        grid=(m // tm, n // tn),
        in_specs=in_specs,
        out_specs=out_specs,
        out_shape=out_shape,
        scratch_shapes=[pltpu.VMEM((tm, k), BF16)],
        compiler_params=_cparams(("arbitrary", "arbitrary")),
    )(*args)


def _outproj_kernel(y1_ref, y2_ref, w_ref, h_ref, gate_ref, o_ref):
    k1 = y1_ref.shape[1]
    acc = _dot(y1_ref[...], w_ref[:k1, :]) + _dot(y2_ref[...], w_ref[k1:, :])
    o_ref[...] = h_ref[...] + gate_ref[...] * acc


def _out_proj(y1, y2, w, layer, h, mod, tm=512):
    m, d = h.shape
    k1, k2 = y1.shape[1], y2.shape[1]
    tm = min(tm, m)
    mod3, gate_chunk, rows_per_group, group0 = mod
    return pl.pallas_call(
        _outproj_kernel,
        grid=(m // tm,),
        in_specs=[
            pl.BlockSpec((tm, k1), lambda i: (i, 0)),
            pl.BlockSpec((tm, k2), lambda i: (i, 0)),
            _layer_spec(w, layer, (k1 + k2, d), lambda i: (0, 0)),
            pl.BlockSpec((tm, d), lambda i: (i, 0)),
            _mod_spec(gate_chunk, rows_per_group // tm, group0, d),
        ],
        out_specs=pl.BlockSpec((tm, d), lambda i: (i, 0)),
        out_shape=jax.ShapeDtypeStruct((m, d), F32),
        compiler_params=_cparams(("arbitrary",)),
    )(y1, y2, w, h, mod3)


def _mlp_kernel(*refs, final_norm):
    if final_norm:
        h_ref, g_ref, sh_ref, sc_ref, gate_ref, w1_ref, w2_ref, fg_ref, o_ref, a_ref, acc_ref = refs
    else:
        h_ref, g_ref, sh_ref, sc_ref, gate_ref, w1_ref, w2_ref, o_ref, a_ref, acc_ref = refs
    k = pl.program_id(1)

    @pl.when(k == 0)
    def _():
        y = _rms(h_ref[...]) * g_ref[...]
        a_ref[...] = (y * (1.0 + sc_ref[...]) + sh_ref[...]).astype(BF16)
        acc_ref[...] = jnp.zeros_like(acc_ref)

    u = jnp.maximum(_dot(a_ref[...], w1_ref[...]), 0.0)
    acc_ref[...] += _dot((u * u).astype(BF16), w2_ref[...])

    @pl.when(k == pl.num_programs(1) - 1)
    def _():
        out = h_ref[...] + gate_ref[...] * acc_ref[...]
        if final_norm:
            out = _rms(out) * fg_ref[...]
        o_ref[...] = out


def _mlp(h, g, w1, w2, layer, mod, final_g=None, tm=512, th=1024):
    m, d = h.shape
    hid = w1.shape[-1]
    tm = min(tm, m)
    mod3, sh_chunk, sc_chunk, gate_chunk, rows_per_group, group0 = mod
    tpg = rows_per_group // tm
    in_specs = [
        pl.BlockSpec((tm, d), lambda i, k: (i, 0)),
        pl.BlockSpec((1, d), lambda i, k: (0, 0)),
        _mod_spec(sh_chunk, tpg, group0, d),
        _mod_spec(sc_chunk, tpg, group0, d),
        _mod_spec(gate_chunk, tpg, group0, d),
        _layer_spec(w1, layer, (d, th), lambda i, k: (0, k)),
        _layer_spec(w2, layer, (th, d), lambda i, k: (k, 0)),
    ]
    args = [h, g.reshape(1, d), mod3, mod3, mod3, w1, w2]
    if final_g is not None:
        in_specs.append(pl.BlockSpec((1, d), lambda i, k: (0, 0)))
        args.append(final_g.reshape(1, d))
    return pl.pallas_call(
        functools.partial(_mlp_kernel, final_norm=final_g is not None),
        grid=(m // tm, hid // th),
        in_specs=in_specs,
        out_specs=pl.BlockSpec((tm, d), lambda i, k: (i, 0)),
        out_shape=jax.ShapeDtypeStruct((m, d), F32),
        scratch_shapes=[pltpu.VMEM((tm, d), BF16), pltpu.VMEM((tm, d), F32)],
        compiler_params=_cparams(("arbitrary", "arbitrary")),
    )(*args)


def _softmax_pv(scores, values, scale=1.0):
    m = functools.reduce(jnp.maximum, [jnp.max(s, axis=-1, keepdims=True) for s in scores])
    ps = [jnp.exp2((s - m) * (scale * LOG2E)) for s in scores]
    denom = functools.reduce(jnp.add, [jnp.sum(p, axis=-1, keepdims=True) for p in ps])
    o = functools.reduce(jnp.add, [_dot(p.astype(BF16), v) for p, v in zip(ps, values)])
    return o / denom


def _rope_rotate(x, cos, sin):
    x = x.astype(F32)
    lane = lax.broadcasted_iota(jnp.int32, x.shape, 1)
    partner = jnp.where((lane % 32) < 16, pltpu.roll(x, LANES - 16, 1), pltpu.roll(x, 16, 1))
    return x * cos + partner * sin


def _mla_lat_kernel(qn_ref, qpe_ref, cosq_ref, sinq_ref, knc_ref, kpec_ref, vc_ref, knl_ref, kpel_ref,
                    cosk_ref, sink_ref, vl_ref, o_ref, k_scr, v_scr, *, scale):
    lc = knc_ref.shape[0]

    @pl.when(pl.program_id(2) == 0)
    def _():
        k_scr[:lc, :LANES] = knc_ref[...]
        k_scr[:lc, LANES:] = kpec_ref[...]
        k_scr[lc:, :LANES] = knl_ref[...]
        k_scr[lc:, LANES:] = _rope_rotate(kpel_ref[...], cosk_ref[...], sink_ref[...]).astype(BF16)
        v_scr[:lc, :LANES] = vc_ref[...]
        v_scr[lc:, :LANES] = vl_ref[...]
        v_scr[:, LANES:] = jnp.ones((v_scr.shape[0], LANES), BF16)

    qpe = _rope_rotate(qpe_ref[...], cosq_ref[...], sinq_ref[...]).astype(BF16)
    q = jnp.concatenate([qn_ref[...], qpe], axis=1)
    sub = min(MLA_SUB_ROWS, q.shape[0])
    n_sub = q.shape[0] // sub
    k = k_scr[...]
    v = v_scr[...]

    def scores(i):
        return _dot_t(q[i * sub:(i + 1) * sub], k)

    def finish(i, s):
        p = jnp.exp2((s - jnp.max(s, axis=-1, keepdims=True)) * (scale * LOG2E)).astype(BF16)
        ol = _dot(p, v)
        o_ref[i * sub:(i + 1) * sub, :] = (ol[:, :LANES] / ol[:, LANES:]).astype(o_ref.dtype)

    s_cur = scores(0)
    for i in range(n_sub):
        s_next = scores(i + 1) if i + 1 < n_sub else None
        finish(i, s_cur)
        s_cur = s_next


def _mla_latent(p_lat, p_ctx, kv_lat, kv_ctx, cos_tab, sin_tab, batch, tq=2048):
    n = p_lat.shape[0] // batch
    lc = p_ctx.shape[0] // batch
    tq = min(tq, n)
    assert n % tq == 0 and tq % min(MLA_SUB_ROWS, tq) == 0
    nq = n // tq
    h = MLA_HEADS
    blk = lambda rows, f: pl.BlockSpec((rows, LANES), f)
    return pl.pallas_call(
        functools.partial(_mla_lat_kernel, scale=MLA_QK_DIM ** -0.5),
        grid=(batch, h, nq),
        in_specs=[
            blk(tq, lambda b, hh, i: (b * nq + i, EV_QMLA_BLK + 2 * hh)),
            blk(tq, lambda b, hh, i: (b * nq + i, EV_QMLA_BLK + 2 * hh + 1)),
            blk(tq, lambda b, hh, i: (i, 0)),
            blk(tq, lambda b, hh, i: (i, 0)),
            blk(lc, lambda b, hh, i: (b, 2 * hh)),
            blk(lc, lambda b, hh, i: (b, EV_KPE_BLK)),
            blk(lc, lambda b, hh, i: (b, 2 * hh + 1)),
            blk(n, lambda b, hh, i: (b, 2 * hh)),
            blk(n, lambda b, hh, i: (b, EV_KPE_BLK)),
            blk(n, lambda b, hh, i: (0, 0)),
            blk(n, lambda b, hh, i: (0, 0)),
            blk(n, lambda b, hh, i: (b, 2 * hh + 1)),
        ],
        out_specs=blk(tq, lambda b, hh, i: (b * nq + i, hh)),
        out_shape=jax.ShapeDtypeStruct((batch * n, h * LANES), BF16),
        scratch_shapes=[pltpu.VMEM((lc + n, 2 * LANES), BF16), pltpu.VMEM((lc + n, 2 * LANES), BF16)],
        compiler_params=_cparams(("arbitrary", "arbitrary", "arbitrary")),
    )(p_lat, p_lat, cos_tab, sin_tab, kv_ctx, p_ctx, kv_ctx, kv_lat, p_lat, cos_tab, sin_tab, kv_lat)


def _ctx_attn_kernel(qm_ref, kn_ref, kpe_ref, vm_ref, qn_ref, kna_ref, vna_ref, om_ref, on_ref,
                     *, mla_scale, na_scale):
    k = jnp.concatenate([kn_ref[...], kpe_ref[...]], axis=1)
    s = _dot_t(qm_ref[...], k)
    om_ref[...] = _softmax_pv([s], [vm_ref[...]], mla_scale).astype(om_ref.dtype)
    s = _dot_t(qn_ref[...], kna_ref[...])
    on_ref[...] = _softmax_pv([s], [vna_ref[...]], na_scale).astype(on_ref.dtype)


def _ctx_attention(p_ctx, kv_ctx, batch):
    lc = p_ctx.shape[0] // batch
    h = MLA_HEADS
    blk = lambda f: pl.BlockSpec((lc, LANES), f)
    out = jax.ShapeDtypeStruct((batch * lc, h * LANES), BF16)
    return pl.pallas_call(
        functools.partial(_ctx_attn_kernel, mla_scale=MLA_QK_DIM ** -0.5, na_scale=NA_DIM ** -0.5),
        grid=(batch, h),
        in_specs=[
            pl.BlockSpec((lc, 2 * LANES), lambda b, hh: (b, hh)),
            blk(lambda b, hh: (b, 2 * hh)),
            blk(lambda b, hh: (b, EV_KPE_BLK)),
            blk(lambda b, hh: (b, 2 * hh + 1)),
            blk(lambda b, hh: (b, EV_QNA_BLK + hh)),
            blk(lambda b, hh: (b, EV_KNA_BLK + hh)),
            blk(lambda b, hh: (b, EV_VNA_BLK + hh)),
        ],
        out_specs=[blk(lambda b, hh: (b, hh)), blk(lambda b, hh: (b, hh))],
        out_shape=[out, out],
        compiler_params=_cparams(("arbitrary", "arbitrary")),
    )(p_ctx, kv_ctx, p_ctx, kv_ctx, p_ctx, p_ctx, p_ctx)


def _na_kernel(q_ref, k_ref, v_ref, kc_ref, vc_ref, bias_ref, o_ref, s_scr, p_scr, l_scr, oc_scr,
               *, scale, n_rows):
    win = NA_KH * GRID_W
    slab = 256
    n = q_ref.shape[0]

    def band(r):
        ws = min(max(r - NA_KH // 2, 0), n_rows - NA_KH)
        return ws, slice(r * GRID_W, (r + 1) * GRID_W), slice(ws * GRID_W, ws * GRID_W + win)

    s_scr[:, win:] = _dot_t(q_ref[...], kc_ref[...]) * scale
    for r in range(n_rows):
        ws, rows, keys = band(r)
        s_scr[rows, :win] = _dot_t(q_ref[rows, :], k_ref[keys, :]) * scale + bias_ref[r - ws]

    def body(i, carry):
        sl = pl.ds(pl.multiple_of(i * slab, slab), slab)
        s = s_scr[sl, :]
        p = jnp.exp2((s - jnp.max(s, axis=-1, keepdims=True)) * LOG2E)
        p_scr[sl, :] = p.astype(BF16)
        l_scr[sl, :] = jnp.broadcast_to(1.0 / jnp.sum(p, axis=-1, keepdims=True), (slab, LANES))
        return carry

    lax.fori_loop(0, n // slab, body, 0, unroll=2)
    oc_scr[...] = _dot(p_scr[:, win:], vc_ref[...])
    for r in range(n_rows):
        ws, rows, keys = band(r)
        o = _dot(p_scr[rows, :win], v_ref[keys, :]) + oc_scr[rows, :]
        o_ref[rows, :] = (o * l_scr[rows, :]).astype(o_ref.dtype)


def _na_bias_kernel(rb_ref, onehot_ref, mask_ref, o_ref):
    o_ref[...] = _dot_f32(rb_ref[...], onehot_ref[...]) + mask_ref[...]


def _na_bias_table(rel_bias):
    n_heads, n_ro, n_co = rel_bias.shape
    col = np.arange(GRID_W)
    col_start = np.clip(col - NA_KW // 2, 0, GRID_W - NA_KW)
    col_mask = (col[None, :] >= col_start[:, None]) & (col[None, :] < col_start[:, None] + NA_KW)
    col_off = np.clip(col[None, :] - col[:, None], 1 - NA_KW, NA_KW - 1) + (NA_KW - 1)
    onehot = (col_off.reshape(1, -1) == np.arange(n_co)[:, None]).astype(np.float32)
    mask_add = np.where(col_mask.reshape(1, -1), 0.0, NEG_INF).astype(np.float32)
    qw = GRID_W * GRID_W
    full = lambda shape: pl.BlockSpec(shape, lambda: (0,) * len(shape))
    cols = pl.pallas_call(
        _na_bias_kernel,
        in_specs=[full((n_heads * n_ro, n_co)), full((n_co, qw)), full((1, qw))],
        out_specs=full((n_heads * n_ro, qw)),
        out_shape=jax.ShapeDtypeStruct((n_heads * n_ro, qw), F32),
    )(rel_bias.reshape(n_heads * n_ro, n_co), jnp.asarray(onehot), jnp.asarray(mask_add))
    cols = cols.reshape(n_heads, n_ro, GRID_W, GRID_W)
    t = jnp.stack([cols[:, NA_KH - 1 - e:2 * NA_KH - 1 - e] for e in range(NA_KH)], axis=1)
    return t.transpose(0, 1, 3, 2, 4).reshape(n_heads, NA_KH, GRID_W, NA_KH * GRID_W)


def _na_latent(p_lat, p_ctx, bias_tab, batch):
    n = p_lat.shape[0] // batch
    lc = p_ctx.shape[0] // batch
    h = NA_HEADS
    n_rows = n // GRID_W
    assert n_rows >= NA_KH
    blk = lambda rows, f: pl.BlockSpec((rows, LANES), f)
    return pl.pallas_call(
        functools.partial(_na_kernel, scale=NA_DIM ** -0.5, n_rows=n_rows),
        grid=(batch, h),
        in_specs=[
            blk(n, lambda b, hh: (b, EV_QNA_BLK + hh)),
            blk(n, lambda b, hh: (b, EV_KNA_BLK + hh)),
            blk(n, lambda b, hh: (b, EV_VNA_BLK + hh)),
            blk(lc, lambda b, hh: (b, EV_KNA_BLK + hh)),
            blk(lc, lambda b, hh: (b, EV_VNA_BLK + hh)),
            pl.BlockSpec((None, NA_KH, GRID_W, NA_KH * GRID_W), lambda b, hh: (hh, 0, 0, 0)),
        ],
        out_specs=blk(n, lambda b, hh: (b, hh)),
        out_shape=jax.ShapeDtypeStruct((batch * n, h * LANES), BF16),
        scratch_shapes=[pltpu.VMEM((n, NA_KH * GRID_W + lc), F32), pltpu.VMEM((n, NA_KH * GRID_W + lc), BF16),
                        pltpu.VMEM((n, LANES), F32), pltpu.VMEM((n, LANES), F32)],
        compiler_params=_cparams(("arbitrary", "arbitrary")),
    )(p_lat, p_lat, p_lat, p_ctx, p_ctx, bias_tab)


def _rope_tables(n):
    pos = np.arange(n)
    rows, cols = pos // GRID_W, pos % GRID_W
    half = MLA_ROPE_DIM // 2
    inv_freq = ROPE_THETA ** (-np.arange(0, half, 2, dtype=np.float64) / half)
    cos = np.zeros((n, LANES), np.float64)
    sin = np.zeros((n, LANES), np.float64)
    for base, p in ((0, rows), (half, cols)):
        ang = p[:, None].astype(np.float64) * inv_freq[None, :]
        q = half // 2
        cos[:, base:base + q] = np.cos(ang)
        cos[:, base + q:base + half] = np.cos(ang)
        sin[:, base:base + q] = -np.sin(ang)
        sin[:, base + q:base + half] = np.sin(ang)
    return jnp.asarray(cos, F32), jnp.asarray(sin, F32)


def _even_w_in(w):
    d = w.shape[0]
    z64 = jnp.zeros((d, LANES - MLA_ROPE_DIM), w.dtype)
    pieces = []
    for h in range(MLA_HEADS):
        pieces += [w[:, h * MLA_QK_DIM:h * MLA_QK_DIM + MLA_NOPE_DIM],
                   w[:, h * MLA_QK_DIM + MLA_NOPE_DIM:(h + 1) * MLA_QK_DIM], z64]
    q_end = MLA_HEADS * MLA_QK_DIM
    ckv_end = q_end + MLA_KV_RANK
    kpe_end = ckv_end + MLA_ROPE_DIM
    pieces += [w[:, kpe_end:], w[:, q_end:ckv_end], w[:, ckv_end:kpe_end], z64]
    out = jnp.concatenate(pieces, axis=1)
    pad = EV_WIDTH - out.shape[1]
    return jnp.concatenate([out, jnp.zeros((d, pad), w.dtype)], axis=1).astype(BF16)


def _hgrn_gates(q_ref, z_refs, lbs, row0, qb_s, cum_s, cpk_s, qd_s, kd_s, dec_s):
    c = HGRN_CHUNK
    grp = LANES
    ri = lax.broadcasted_iota(jnp.int32, (grp, grp), 0)
    ci = lax.broadcasted_iota(jnp.int32, (grp, grp), 1)
    same = (ri // c) == (ci // c)
    blk = jnp.where(same, 1.0, 0.0).astype(BF16)
    tris = (jnp.where(jnp.logical_and(same, ci <= ri), 1.0, 0.0).astype(BF16),
            jnp.where(jnp.logical_and(same, ci >= ri), 1.0, 0.0).astype(BF16))

    def body(g, carry):
        src = pl.ds(pl.multiple_of(g * grp, grp), grp)
        dst = pl.ds(pl.multiple_of(row0 + g * grp, grp), grp)
        q = _silu(q_ref[src, :].astype(F32))
        qb_s[dst, :] = q.astype(BF16)
        for d in range(2):
            f = jnp.maximum(lbs[d] + (1.0 - lbs[d]) * _sigmoid(z_refs[d][src, :].astype(F32)), FORGET_FLOOR)
            lf = jnp.log(f)
            k = 1.0 - f
            cum = _dot_01(tris[d], lf)
            tot = _dot_01(blk, lf)
            cum_s[d, dst, :] = cum * LOG2E
            cpk_s[d, dst, :] = (cum - jnp.log(k)) * LOG2E
            qd_s[d, dst, :] = (q * jnp.exp(cum)).astype(BF16)
            kd_s[d, dst, :] = (k * jnp.exp(tot - cum)).astype(BF16)
            dec_s[d, dst, :] = jnp.exp(tot)
        return carry

    lax.fori_loop(0, q_ref.shape[0] // grp, body, 0)


def _hgrn_chunk(qb, cum2, cpk2, v, qd, st, ones_bf, reverse):
    c = HGRN_CHUNK
    hc = c // 2
    o = _dot_t(qd, st.astype(BF16))
    rows = lax.broadcasted_iota(jnp.int32, (hc, LANES), 0)
    halves = (cum2[:hc], cum2[hc:])
    zero = jnp.zeros((hc, LANES), F32)
    pieces = []
    for s in range(c):
        ref = cpk2[s:s + 1]
        hs, rs = divmod(s, hc)
        es = []
        for hh in range(2):
            if hh == hs:
                mask = (rows <= rs) if reverse else (rows >= rs)
                es.append(jnp.exp2(jnp.where(mask, halves[hh] - ref, NEG_INF)))
            elif (hh > hs) != reverse:
                es.append(jnp.exp2(halves[hh] - ref))
            else:
                es.append(zero)
        pieces.append(jnp.concatenate(es, axis=0).astype(BF16) * qb)
    lhs = jnp.concatenate([jnp.concatenate(pieces[:hc], axis=0), jnp.concatenate(pieces[hc:], axis=0)], axis=1)
    r = _dot(lhs, ones_bf)
    for s in range(c):
        hs, rs = divmod(s, hc)
        o = o + r[rs * c:(rs + 1) * c, hs * LANES:(hs + 1) * LANES] * v[s:s + 1]
    return o


def _hgrn_gates_fast(q_ref, i_ref, z_refs, lbs, row0, worst, v_s, qf_s, kf_s, qdf_s, kdf_s, decf_s):
    c = HGRN_FAST_CHUNK
    grp = LANES
    ri = lax.broadcasted_iota(jnp.int32, (grp, grp), 0)
    ci = lax.broadcasted_iota(jnp.int32, (grp, grp), 1)
    same = (ri // c) == (ci // c)
    half = c // 2
    sums = []
    for fwd in (True, False):
        tri = (ci <= ri) if fwd else (ci >= ri)
        upto_ref = ((ci % c) <= half) if fwd else ((ci % c) >= half)
        rows = [jnp.logical_and(same, tri), same, jnp.logical_and(same, upto_ref)]
        sums.append(jnp.concatenate([jnp.where(m, 1.0, 0.0) for m in rows], axis=0).astype(BF16))

    def body(g, worst):
        src = pl.ds(pl.multiple_of(g * grp, grp), grp)
        dst = pl.ds(pl.multiple_of(row0 + g * grp, grp), grp)
        q = _silu(q_ref[src, :].astype(F32))
        v_s[dst, :] = i_ref[src, :].astype(F32)
        fs = [jnp.maximum(lbs[d] + (1.0 - lbs[d]) * _sigmoid(z_refs[d][src, :].astype(F32)), FORGET_FLOOR)
              for d in range(2)]
        res = [_dot_01(sums[d], jnp.log(fs[d])) for d in range(2)]
        for d in range(2):
            k = 1.0 - fs[d]
            cum, tot, ref = res[d][:grp], res[d][grp:2 * grp], res[d][2 * grp:]
            qf_s[d, dst, :] = (q * jnp.exp(cum - ref)).astype(BF16)
            kf_s[d, dst, :] = (k * jnp.exp(ref - cum)).astype(BF16)
            qdf_s[d, dst, :] = (q * jnp.exp(cum)).astype(BF16)
            kdf_s[d, dst, :] = (k * jnp.exp(tot - cum)).astype(BF16)
            decf_s[d, dst, :] = jnp.exp(tot)
            worst = jnp.maximum(worst, -tot)
        return worst

    return lax.fori_loop(0, q_ref.shape[0] // grp, body, worst, unroll=2)


def _hgrn_kernel(ql_ref, il_ref, zfl_ref, zbl_ref, gl_ref, qc_ref, ic_ref, zfc_ref, zbc_ref, gc_ref,
                 lbl_ref, ng_ref, yl_ref, yc_ref,
                 qb_s, v_s, cum_s, cpk_s, qd_s, kd_s, dec_s, o_s, qf_s, kf_s, qdf_s, kdf_s, decf_s, *, layer):
    lc, n = qc_ref.shape[0], ql_ref.shape[0]
    lbs = []
    for d in range(2):
        lg = lbl_ref[d]
        ex = jnp.exp(lg - jnp.max(lg, axis=0, keepdims=True))
        p = ex / jnp.sum(ex, axis=0, keepdims=True)
        lbs.append(jnp.sum(p[:layer + 1], axis=0, keepdims=True) - p[0:1])

    zero = jnp.zeros((LANES, LANES), F32)

    def chunk_rows(j, c):
        r_fwd = j * c
        r_bwd = jnp.where(j < lc // c, lc - c - j * c, 2 * lc + n - c - j * c)
        return [pl.ds(pl.multiple_of(r, c), c) for r in (r_fwd, r_bwd)]

    fast_scr = (v_s, qf_s, kf_s, qdf_s, kdf_s, decf_s)
    worst = _hgrn_gates_fast(qc_ref, ic_ref, (zfc_ref, zbc_ref), lbs, 0, zero, *fast_scr)
    worst = _hgrn_gates_fast(ql_ref, il_ref, (zfl_ref, zbl_ref), lbs, lc, worst, *fast_scr)
    fast_ok = jnp.max(worst) <= HGRN_FAST_MAX_DECAY

    @pl.when(fast_ok)
    def _():
        c = HGRN_FAST_CHUNK
        ri = lax.broadcasted_iota(jnp.int32, (c, c), 0)
        ci = lax.broadcasted_iota(jnp.int32, (c, c), 1)
        causal = (ci <= ri, ci >= ri)

        def body(g, sts):
            slices = [chunk_rows(g * HGRN_FAST_UNROLL + i, c) for i in range(HGRN_FAST_UNROLL)]
            scores = [[jnp.where(causal[d], _dot_t(qf_s[d, sl[d], :], kf_s[d, sl[d], :]), 0.0).astype(BF16)
                       for d in range(2)] for sl in slices]
            incs = [[_dot_tn(v_s[sl[d], :].astype(BF16), kdf_s[d, sl[d], :]) for d in range(2)] for sl in slices]
            local = [[_dot(scores[i][d], v_s[sl[d], :].astype(BF16)) for d in range(2)]
                     for i, sl in enumerate(slices)]
            sts = list(sts)
            for i, sl2 in enumerate(slices):
                for d in range(2):
                    sl = sl2[d]
                    o_s[d, sl, :] = _dot_t(qdf_s[d, sl, :], sts[d].astype(BF16)) + local[i][d]
                    sts[d] = sts[d] * decf_s[d, sl, :][0:1] + incs[i][d]
            return tuple(sts)

        n_chunks = (lc + n) // c
        assert n_chunks % HGRN_FAST_UNROLL == 0
        lax.fori_loop(0, n_chunks // HGRN_FAST_UNROLL, body, (zero, zero))

    @pl.when(jnp.logical_not(fast_ok))
    def _():
        c = HGRN_CHUNK
        scr = (qb_s, cum_s, cpk_s, qd_s, kd_s, dec_s)
        _hgrn_gates(qc_ref, (zfc_ref, zbc_ref), lbs, 0, *scr)
        _hgrn_gates(ql_ref, (zfl_ref, zbl_ref), lbs, lc, *scr)
        ri = lax.broadcasted_iota(jnp.int32, (2 * LANES, 2 * LANES), 0)
        ci = lax.broadcasted_iota(jnp.int32, (2 * LANES, 2 * LANES), 1)
        ones_bf = jnp.where((ri // LANES) == (ci // LANES), 1.0, 0.0).astype(BF16)

        def body(g, sts):
            slices = [chunk_rows(g * HGRN_UNROLL + i, c) for i in range(HGRN_UNROLL)]
            incs = [[_dot_tn(v_s[sl[d], :].astype(BF16), kd_s[d, sl[d], :]) for d in range(2)] for sl in slices]
            sts = list(sts)
            for i, sl2 in enumerate(slices):
                for d in range(2):
                    sl = sl2[d]
                    o_s[d, sl, :] = _hgrn_chunk(qb_s[sl, :], cum_s[d, sl, :], cpk_s[d, sl, :], v_s[sl, :],
                                                qd_s[d, sl, :], sts[d], ones_bf, reverse=d == 1)
                    sts[d] = sts[d] * dec_s[d, sl, :][0:1] + incs[i][d]
            return tuple(sts)

        n_chunks = (lc + n) // c
        assert n_chunks % HGRN_UNROLL == 0
        lax.fori_loop(0, n_chunks // HGRN_UNROLL, body, (zero, zero))

    ng = ng_ref[...]
    yc_ref[...] = (_rms(o_s[0, :lc, :] + o_s[1, :lc, :]) * ng
                   * _silu(gc_ref[...].astype(F32))).astype(yc_ref.dtype)
    yl_ref[...] = (_rms(o_s[0, lc:, :] + o_s[1, lc:, :]) * ng
                   * _silu(gl_ref[...].astype(F32))).astype(yl_ref.dtype)


def _hgrn2(p_lat, p_ctx, lb_logits, norm_g, layer, batch):
    n = p_lat.shape[0] // batch
    lc = p_ctx.shape[0] // batch
    h = HGRN_HEADS
    n_layers = lb_logits.shape[1]
    rows = lc + n
    assert lc % LANES == 0 and n % LANES == 0
    lat = lambda part: pl.BlockSpec((n, LANES), lambda b, hh: (b, part * h + hh))
    ctx = lambda part: pl.BlockSpec((lc, LANES), lambda b, hh: (b, part * h + hh))
    return pl.pallas_call(
        functools.partial(_hgrn_kernel, layer=layer),
        grid=(batch, h),
        in_specs=[lat(0), lat(1), lat(2), lat(3), lat(4), ctx(0), ctx(1), ctx(2), ctx(3), ctx(4),
                  pl.BlockSpec((2, n_layers, LANES), lambda b, hh: (0, 0, hh)),
                  pl.BlockSpec((1, LANES), lambda b, hh: (0, hh))],
        out_specs=[pl.BlockSpec((n, LANES), lambda b, hh: (b, hh)),
                   pl.BlockSpec((lc, LANES), lambda b, hh: (b, hh))],
        out_shape=[jax.ShapeDtypeStruct((batch * n, h * LANES), BF16),
                   jax.ShapeDtypeStruct((batch * lc, h * LANES), BF16)],
        scratch_shapes=[pltpu.VMEM((rows, LANES), BF16), pltpu.VMEM((rows, LANES), F32),
                        pltpu.VMEM((2, rows, LANES), F32), pltpu.VMEM((2, rows, LANES), F32),
                        pltpu.VMEM((2, rows, LANES), BF16), pltpu.VMEM((2, rows, LANES), BF16),
                        pltpu.VMEM((2, rows, LANES), F32), pltpu.VMEM((2, rows, LANES), F32),
                        pltpu.VMEM((2, rows, LANES), BF16), pltpu.VMEM((2, rows, LANES), BF16),
                        pltpu.VMEM((2, rows, LANES), BF16), pltpu.VMEM((2, rows, LANES), BF16),
                        pltpu.VMEM((2, rows, LANES), F32)],
        compiler_params=_cparams(("arbitrary", "arbitrary")),
    )(p_lat, p_lat, p_lat, p_lat, p_lat, p_ctx, p_ctx, p_ctx, p_ctx, p_ctx,
      lb_logits, norm_g[layer].reshape(1, -1))


def _dft_matrices(n):
    idx = (np.arange(n)[:, None] * np.arange(n)[None, :]) % (2 * n)
    ang = idx.astype(np.float64) * (math.pi / n)
    cm = np.cos(ang)
    sf = np.sin(ang)
    sf[0, :] = (-1.0) ** np.arange(n)
    return (jnp.asarray(cm, F32).astype(BF16), jnp.asarray(sf, F32).astype(BF16),
            jnp.asarray(sf.T, F32).astype(BF16))


def _filter_features(n):
    pos = np.arange(n, dtype=np.float64)
    t = pos / max(n - 1, 1)
    bands = np.linspace(1e-4, HYENA_BANDS - 1, HYENA_BANDS)
    ang = (2.0 * math.pi / n) * pos[:, None] * bands[None, :]
    z = np.concatenate([t[:, None], np.cos(ang), -np.sin(ang)], -1)
    max_decay = math.log(HYENA_DECAY_TARGET) / HYENA_FAST_PCT
    min_decay = math.log(HYENA_DECAY_TARGET) / HYENA_SLOW_PCT
    deltas = np.abs(np.linspace(min_decay, max_decay, HYENA_WIDTH))
    return jnp.asarray(z, F32), jnp.asarray(t[:, None], F32), jnp.asarray(deltas[None, :], F32)


def _filter_kernel(z_ref, t_ref, dl_ref, w1_ref, b1_ref, w2_ref, b2_ref, w3_ref, b3_ref, fr_ref, wo_ref,
                   o_ref, hdn_ref):
    j = pl.program_id(0)

    @pl.when(j == 0)
    def _():
        fr = fr_ref[...]
        hdn = jnp.sin(fr * (_dot_f32(z_ref[...], w1_ref[...]) + b1_ref[...]))
        hdn = jnp.sin(fr * (_dot_f32(hdn, w2_ref[...]) + b2_ref[...]))
        hdn_ref[...] = jnp.sin(fr * (_dot_f32(hdn, w3_ref[...]) + b3_ref[...]))

    filt = _dot_f32(hdn_ref[...], wo_ref[...]) * jnp.exp(-t_ref[...] * dl_ref[...])
    row = lax.broadcasted_iota(jnp.int32, filt.shape, 0)
    is_bwd = j >= pl.num_programs(0) // 2
    o_ref[...] = jnp.where(jnp.logical_and(is_bwd, row == 0), 0.0, filt).astype(o_ref.dtype)


def _hyena_filters(n, w1, b1, w2, b2, w3, b3, freq, w_out, tc=512):
    z, t, deltas = _filter_features(n)
    hid = HYENA_FILT_HIDDEN
    nct = HYENA_WIDTH // tc
    full = lambda shape: pl.BlockSpec(shape, lambda j: (0,) * len(shape))
    return pl.pallas_call(
        _filter_kernel,
        grid=(2 * nct,),
        in_specs=[full((n, HYENA_EMB)), full((n, 1)),
                  pl.BlockSpec((1, tc), lambda j: (0, j % nct)),
                  full((HYENA_EMB, hid)), full((1, hid)), full((hid, hid)), full((1, hid)),
                  full((hid, hid)), full((1, hid)), full((1, hid)),
                  pl.BlockSpec((hid, tc), lambda j: (0, j))],
        out_specs=pl.BlockSpec((n, tc), lambda j: (0, j)),
        out_shape=jax.ShapeDtypeStruct((n, 2 * HYENA_WIDTH), BF16),
        scratch_shapes=[pltpu.VMEM((n, hid), F32)],
        compiler_params=_cparams(("arbitrary",)),
    )(z, t, deltas, w1, b1.reshape(1, hid), w2, b2.reshape(1, hid), w3, b3.reshape(1, hid),
      freq.reshape(1, hid), w_out)


def _spectrum_kernel(cm_ref, sf_ref, hf_ref, hb_ref, a_ref, b_ref, *, inv_len):
    cm, sf, hf, hb = cm_ref[...], sf_ref[...], hf_ref[...], hb_ref[...]
    kr = _dot(cm, hf) + _dot(cm, hb)
    d1 = _dot(sf, hf)
    d2 = _dot(sf, hb)
    row = lax.broadcasted_iota(jnp.int32, kr.shape, 0) + pl.program_id(0) * kr.shape[0]
    first = row == 0
    w = jnp.where(first, inv_len, 2.0 * inv_len)
    a_ref[...] = kr * w
    b_ref[...] = jnp.where(first, d1 + d2, d1 - d2) * w


def _filter_spectrum(filt, cm, sf, tk=512, tc=512):
    n = filt.shape[0]
    tk = min(tk, n)
    nct = HYENA_WIDTH // tc
    out = jax.ShapeDtypeStruct((n, HYENA_WIDTH), F32)
    return pl.pallas_call(
        functools.partial(_spectrum_kernel, inv_len=1.0 / (2 * n)),
        grid=(n // tk, nct),
        in_specs=[pl.BlockSpec((tk, n), lambda i, j: (i, 0)),
                  pl.BlockSpec((tk, n), lambda i, j: (i, 0)),
                  pl.BlockSpec((n, tc), lambda i, j: (0, j)),
                  pl.BlockSpec((n, tc), lambda i, j: (0, nct + j))],
        out_specs=[pl.BlockSpec((tk, tc), lambda i, j: (i, j)), pl.BlockSpec((tk, tc), lambda i, j: (i, j))],
        out_shape=[out, out],
        compiler_params=_cparams(("arbitrary", "arbitrary")),
    )(cm, sf, filt, filt)


def _hyena_gate_kernel(u0_ref, u1_ref, uv_ref, w0_ref, w1_ref, wv_ref, b0_ref, b1_ref, bv_ref,
                       x0_ref, z_ref):
    n = u0_ref.shape[0]
    row = lax.broadcasted_iota(jnp.int32, u0_ref.shape, 0)

    def conv(u_ref, w_ref, b_ref):
        u = u_ref[...].astype(F32)
        prev = jnp.where(row == 0, 0.0, pltpu.roll(u, 1, 0))
        nxt = jnp.where(row == n - 1, 0.0, pltpu.roll(u, n - 1, 0))
        return b_ref[...] + prev * w_ref[0:1] + u * w_ref[1:2] + nxt * w_ref[2:3]

    x0_ref[...] = conv(u0_ref, w0_ref, b0_ref).astype(x0_ref.dtype)
    z_ref[...] = (conv(uv_ref, wv_ref, bv_ref) * conv(u1_ref, w1_ref, b1_ref)).astype(z_ref.dtype)


def _hyena_gate(p, first_blk, conv_w, conv_b, batch, tc=256):
    n = p.shape[0] // batch
    nct = HYENA_WIDTH // tc
    c0 = first_blk * LANES // tc
    u = lambda part: pl.BlockSpec((n, tc), lambda b, j: (b, c0 + part * nct + j))
    w = lambda part: pl.BlockSpec((HYENA_SHORT, tc), lambda b, j: (0, part * nct + j))
    bb = lambda part: pl.BlockSpec((1, tc), lambda b, j: (0, part * nct + j))
    out = jax.ShapeDtypeStruct((batch * n, HYENA_WIDTH), BF16)
    cb = conv_b.reshape(1, -1)
    return pl.pallas_call(
        _hyena_gate_kernel,
        grid=(batch, nct),
        in_specs=[u(0), u(1), u(2), w(0), w(1), w(2), bb(0), bb(1), bb(2)],
        out_specs=[pl.BlockSpec((n, tc), lambda b, j: (b, j)), pl.BlockSpec((n, tc), lambda b, j: (b, j))],
        out_shape=[out, out],
        compiler_params=_cparams(("arbitrary", "arbitrary")),
    )(p, p, p, conv_w, conv_w, conv_w, cb, cb, cb)


def _dft_fwd_kernel(cm_ref, sf_ref, z_ref, a_ref, b_ref, pr_ref, ps_ref):
    z = z_ref[...]
    zr = _dot(cm_ref[...], z)
    zs = _dot(sf_ref[...], z)
    a, b = a_ref[...], b_ref[...]
    row = lax.broadcasted_iota(jnp.int32, zr.shape, 0) + pl.program_id(1) * zr.shape[0]
    first = row == 0
    pr_ref[...] = (zr * a - jnp.where(first, 0.0, zs * b)).astype(pr_ref.dtype)
    ps_ref[...] = (jnp.where(first, 0.0, zr * b) + zs * jnp.where(first, b, a)).astype(ps_ref.dtype)


def _dft_inv_kernel(cm_ref, si_ref, pr_ref, ps_ref, z_ref, x0_ref, skip_ref, o_ref):
    y = _dot(cm_ref[...], pr_ref[...]) + _dot(si_ref[...], ps_ref[...])
    z = z_ref[...].astype(F32)
    o_ref[...] = (x0_ref[...].astype(F32) * (y + z * skip_ref[...])).astype(o_ref.dtype)


def _long_conv(z, x0, spec_a, spec_b, skip, cm, sf, si, batch, tk=1024, tc=512):
    n = z.shape[0] // batch
    tk = min(tk, n)
    nk = n // tk
    nct = HYENA_WIDTH // tc
    mat = pl.BlockSpec((tk, n), lambda b, i, j: (i, 0))
    col = pl.BlockSpec((n, tc), lambda b, i, j: (b, j))
    tile_nb = pl.BlockSpec((tk, tc), lambda b, i, j: (i, j))
    tile = pl.BlockSpec((tk, tc), lambda b, i, j: (b * nk + i, j))
    spec_shape = jax.ShapeDtypeStruct((batch * n, HYENA_WIDTH), BF16)
    pr, ps = pl.pallas_call(
        _dft_fwd_kernel,
        grid=(batch, nk, nct),
        in_specs=[mat, mat, col, tile_nb, tile_nb],
        out_specs=[tile, tile],
        out_shape=[spec_shape, spec_shape],
        compiler_params=_cparams(("arbitrary", "arbitrary", "arbitrary")),
    )(cm, sf, z, spec_a, spec_b)
    return pl.pallas_call(
        _dft_inv_kernel,
        grid=(batch, nk, nct),
        in_specs=[mat, mat, col, col, tile, tile, pl.BlockSpec((1, tc), lambda b, i, j: (0, j))],
        out_specs=tile,
        out_shape=jax.ShapeDtypeStruct((batch * n, HYENA_WIDTH), BF16),
        compiler_params=_cparams(("arbitrary", "arbitrary", "arbitrary")),
    )(cm, si, pr, ps, z, x0, skip.reshape(1, -1))


def _hyena(p, first_blk, conv_w, conv_b, filt_params, skip, batch):
    n = p.shape[0] // batch
    cm, sf, si = _dft_matrices(n)
    filt = _hyena_filters(n, *filt_params)
    spec_a, spec_b = _filter_spectrum(filt, cm, sf)
    x0, z = _hyena_gate(p, first_blk, conv_w, conv_b, batch)
    return _long_conv(z, x0, spec_a, spec_b, skip, cm, sf, si, batch)


def kernel(x, c, ctx, c_ctx, ada_w, ada_b, norm_mix_g, norm_mlp_g, w_out, mlp_w1, mlp_w2, final_norm_g, ev_w_in, mla_kv_norm_g, mla_w_ukv, na_rel_bias, od_w_in, hgrn_lb_logits, hgrn_norm_g, hy_conv_w, hy_conv_b, hy_filt_w1, hy_filt_b1, hy_filt_w2, hy_filt_b2, hy_filt_w3, hy_filt_b3, hy_filt_freq, hy_filt_wout, hy_skip):
    batch, seq, d = x.shape
    lc = ctx.shape[1]
    depth = ada_w.shape[0]
    h_lat = x.reshape(batch * seq, d)
    h_ctx = ctx.reshape(batch * lc, d)

    cond = jnp.concatenate([c, c_ctx[None, :], jnp.zeros((8 - batch - 1, d), F32)], axis=0)
    mod_all = _ada_modulation(cond, ada_w, ada_b)
    cos_tab, sin_tab = _rope_tables(seq)
    hgrn_cols = 5 * HGRN_WIDTH
    od_w_in, w_out, mlp_w1, mlp_w2, mla_w_ukv = (
        t.astype(BF16) for t in (od_w_in, w_out, mlp_w1, mlp_w2, mla_w_ukv))

    for l in range(depth):
        ctx_out = l < depth - 1
        mod3 = mod_all[l].reshape(8, 1, 6 * d)
        lat_mod = lambda *chunks: (mod3, *chunks, seq, 0)
        ctx_mod = lambda *chunks: (mod3, *chunks, batch * lc, batch)
        if l % 2 == 0:
            e = l // 2
            w_in = _even_w_in(ev_w_in[e])
            kv = (EV_CKV_BLK * LANES, mla_kv_norm_g[e], mla_w_ukv, e)
            p_lat, kv_lat = _norm_proj(h_lat, 0, d, norm_mix_g[l], w_in, BF16, lat_mod(0, 1), kv=kv)
            p_ctx, kv_ctx = _norm_proj(h_ctx, 0, d, norm_mix_g[l], w_in, BF16, ctx_mod(0, 1), kv=kv)
            y1_lat = _mla_latent(p_lat, p_ctx, kv_lat, kv_ctx, cos_tab, sin_tab, batch)
            y2_lat = _na_latent(p_lat, p_ctx, _na_bias_table(na_rel_bias[e]), batch)
            if ctx_out:
                y1_ctx, y2_ctx = _ctx_attention(p_ctx, kv_ctx, batch)
        else:
            o = l // 2
            p_lat = _norm_proj(h_lat, 0, d, norm_mix_g[l], od_w_in, BF16, lat_mod(0, 1), layer=o)
            p_ctx = _norm_proj(h_ctx, 0, d, norm_mix_g[l], od_w_in, BF16, ctx_mod(0, 1), layer=o,
                               n=None if ctx_out else hgrn_cols)
            y1_lat, y1_ctx = _hgrn2(p_lat, p_ctx, hgrn_lb_logits, hgrn_norm_g, o, batch)
            filt_params = (hy_filt_w1[o], hy_filt_b1[o], hy_filt_w2[o], hy_filt_b2[o], hy_filt_w3[o],
                           hy_filt_b3[o], hy_filt_freq[o], hy_filt_wout[o])
            y2_lat = _hyena(p_lat, hgrn_cols // LANES, hy_conv_w[o], hy_conv_b[o], filt_params,
                            hy_skip[o], batch)
            if ctx_out:
                y2_ctx = _hyena(p_ctx, hgrn_cols // LANES, hy_conv_w[o], hy_conv_b[o], filt_params,
                                hy_skip[o], batch)
        h_lat = _out_proj(y1_lat, y2_lat, w_out, l, h_lat, lat_mod(2))
        h_lat = _mlp(h_lat, norm_mlp_g[l], mlp_w1, mlp_w2, l, lat_mod(3, 4, 5),
                     final_g=None if ctx_out else final_norm_g)
        if ctx_out:
            h_ctx = _out_proj(y1_ctx, y2_ctx, w_out, l, h_ctx, ctx_mod(2))
            h_ctx = _mlp(h_ctx, norm_mlp_g[l], mlp_w1, mlp_w2, l, ctx_mod(3, 4, 5))
    return h_lat.reshape(batch, seq, d)
```

```python
import functools
import math

import numpy as np
import jax
import jax.numpy as jnp
from jax import lax
from jax.experimental import pallas as pl
from jax.experimental.pallas import tpu as pltpu

F32 = jnp.float32
BF16 = jnp.bfloat16

D_MODEL = 2048
DEPTH = 4
GRID_W = 64
HEAD_DIM = 128
MLA_HEADS = 8
MLA_NOPE_DIM = 128
MLA_ROPE_DIM = 64
MLA_QK_DIM = MLA_NOPE_DIM + MLA_ROPE_DIM
MLA_KV_RANK = 512
NA_HEADS = 8
NA_DIM = 128
NA_KH = 8
NA_KW = 16
ROPE_THETA = 10000.0
HGRN_WIDTH = 1024
HGRN_HEADS = 8
FORGET_FLOOR = 1e-30
HYENA_WIDTH = 1024
HYENA_SHORT = 3
HYENA_EMB = 33
HYENA_BANDS = (HYENA_EMB - 1) // 2
HYENA_FILT_HIDDEN = 64
HYENA_DECAY_TARGET = 1e-2
HYENA_FAST_PCT = 0.3
HYENA_SLOW_PCT = 1.5
MLP_HIDDEN = 4 * D_MODEL
NORM_EPS = 1e-6
NEG_INF = -1e30
LOG2E = 1.4426950408889634

LANES = 128
VMEM_LIMIT_BYTES = 56 * 1024 * 1024

EV_QMLA_BLK = 0
EV_QNA_BLK = 16
EV_KNA_BLK = 24
EV_VNA_BLK = 32
EV_CKV_BLK = 40
EV_KPE_BLK = 44
EV_WIDTH = 48 * LANES
MLA_SUB_ROWS = 512
HGRN_CHUNK = 16
HGRN_UNROLL = 8
HGRN_FAST_CHUNK = 32
HGRN_FAST_UNROLL = 8
HGRN_FAST_MAX_DECAY = 80.0


def _cparams(sem):
    return pltpu.CompilerParams(dimension_semantics=sem, vmem_limit_bytes=VMEM_LIMIT_BYTES)


def _dot(a, b):
    return jnp.dot(a, b, preferred_element_type=F32)


def _dot_t(a, b):
    return lax.dot_general(a, b, (((1,), (1,)), ((), ())), preferred_element_type=F32)


def _dot_tn(a, b):
    return lax.dot_general(a, b, (((0,), (0,)), ((), ())), preferred_element_type=F32)


def _dot_f32(a, b):
    return jnp.dot(a, b, preferred_element_type=F32, precision=lax.Precision.HIGHEST)


def _dot_01(m01, x):
    hi = x.astype(BF16)
    lo = (x - hi.astype(F32)).astype(BF16)
    return _dot(m01, hi) + _dot(m01, lo)


def _sigmoid(x):
    return 1.0 / (1.0 + jnp.exp(-x))


def _silu(x):
    return x * _sigmoid(x)


def _rms(x):
    return x * lax.rsqrt(jnp.mean(x * x, axis=-1, keepdims=True) + NORM_EPS)


def _ada_kernel(s_ref, w_ref, b_ref, o_ref):
    s = _silu(s_ref[...]).astype(BF16)
    o_ref[...] = _dot(s, w_ref[...].astype(BF16)) + b_ref[...]


def _ada_modulation(cond, ada_w, ada_b, tn=1024):
    depth, d, n = ada_w.shape
    rows = cond.shape[0]
    return pl.pallas_call(
        _ada_kernel,
        grid=(depth, n // tn),
        in_specs=[
            pl.BlockSpec((rows, d), lambda l, j: (0, 0)),
            pl.BlockSpec((None, d, tn), lambda l, j: (l, 0, j)),
            pl.BlockSpec((None, 1, tn), lambda l, j: (l, 0, j)),
        ],
        out_specs=pl.BlockSpec((None, rows, tn), lambda l, j: (l, 0, j)),
        out_shape=jax.ShapeDtypeStruct((depth, rows, n), F32),
        compiler_params=_cparams(("arbitrary", "arbitrary")),
    )(cond, ada_w, ada_b.reshape(depth, 1, n))


def _mod_spec(chunk, tiles_per_group, group0, d):
    return pl.BlockSpec((None, 1, d), lambda i, *_: (group0 + i // tiles_per_group, 0, chunk))


def _proj_kernel(*refs, modulated, kv_tile):
    refs = list(refs)
    x_ref, g_ref = refs[:2]
    rest = refs[2:]
    if modulated:
        sh_ref, sc_ref = rest[:2]
        rest = rest[2:]
    w_ref = rest[0]
    rest = rest[1:]
    if kv_tile is not None:
        kvg_ref, wkv_ref, o_ref, kv_ref, a_ref = rest
    else:
        o_ref, a_ref = rest

    @pl.when(pl.program_id(1) == 0)
    def _():
        y = _rms(x_ref[...].astype(F32)) * g_ref[...]
        if modulated:
            y = y * (1.0 + sc_ref[...]) + sh_ref[...]
        a_ref[...] = y.astype(BF16)

    acc = _dot(a_ref[...], w_ref[...])
    o_ref[...] = acc.astype(o_ref.dtype)

    if kv_tile is not None:
        @pl.when(pl.program_id(1) == kv_tile)
        def _():
            rank = kvg_ref.shape[1]
            ckv = _rms(acc[:, :rank].astype(o_ref.dtype).astype(F32)) * kvg_ref[...]
            kv_ref[...] = _dot(ckv.astype(BF16), wkv_ref[...]).astype(kv_ref.dtype)


def _layer_spec(w, layer, block, index_map):
    if w.ndim == 2:
        return pl.BlockSpec(block, index_map)
    return pl.BlockSpec((None,) + block, lambda *idx: (layer,) + index_map(*idx))


def _norm_proj(x, x_col_blk, k, g, w, out_dtype, mod=None, layer=None, n=None, kv=None, tm=1024, tn=1024):
    m = x.shape[0]
    n = w.shape[-1] if n is None else n
    tm = min(tm, m)
    tn = min(tn, n)
    assert m % tm == 0 and n % tn == 0
    in_specs = [pl.BlockSpec((tm, k), lambda i, j: (i, x_col_blk)),
                pl.BlockSpec((1, k), lambda i, j: (0, 0))]
    args = [x, g.reshape(1, k)]
    if mod is not None:
        mod3, sh_chunk, sc_chunk, rows_per_group, group0 = mod
        assert rows_per_group % tm == 0
        in_specs += [_mod_spec(sh_chunk, rows_per_group // tm, group0, k),
                     _mod_spec(sc_chunk, rows_per_group // tm, group0, k)]
        args += [mod3, mod3]
    in_specs.append(_layer_spec(w, layer, (k, tn), lambda i, j: (0, j)))
    args.append(w)
    out_specs = pl.BlockSpec((tm, tn), lambda i, j: (i, j))
    out_shape = jax.ShapeDtypeStruct((m, n), out_dtype)
    kv_tile = None
    if kv is not None:
        kv_col, kv_g, w_kv, kv_layer = kv
        rank, n_kv = w_kv.shape[-2:]
        assert kv_col % tn == 0 and rank <= tn
        kv_tile = kv_col // tn
        in_specs += [pl.BlockSpec((1, rank), lambda i, j: (0, 0)),
                     _layer_spec(w_kv, kv_layer, (rank, n_kv), lambda i, j: (0, 0))]
        args += [kv_g.reshape(1, rank), w_kv]
        out_specs = [out_specs, pl.BlockSpec((tm, n_kv), lambda i, j: (i, 0))]
        out_shape = [out_shape, jax.ShapeDtypeStruct((m, n_kv), out_dtype)]
    return pl.pallas_call(
        functools.partial(_proj_kernel, modulated=mod is not None, kv_tile=kv_tile),
        grid=(m // tm, n // tn),
        in_specs=in_specs,
        out_specs=out_specs,
        out_shape=out_shape,
        scratch_shapes=[pltpu.VMEM((tm, k), BF16)],
        compiler_params=_cparams(("arbitrary", "arbitrary")),
    )(*args)


def _outproj_kernel(y1_ref, y2_ref, w_ref, h_ref, gate_ref, o_ref):
    k1 = y1_ref.shape[1]
    acc = _dot(y1_ref[...], w_ref[:k1, :]) + _dot(y2_ref[...], w_ref[k1:, :])
    o_ref[...] = h_ref[...] + gate_ref[...] * acc


def _out_proj(y1, y2, w, layer, h, mod, tm=512):
    m, d = h.shape
    k1, k2 = y1.shape[1], y2.shape[1]
    tm = min(tm, m)
    mod3, gate_chunk, rows_per_group, group0 = mod
    return pl.pallas_call(
        _outproj_kernel,
        grid=(m // tm,),
        in_specs=[
            pl.BlockSpec((tm, k1), lambda i: (i, 0)),
            pl.BlockSpec((tm, k2), lambda i: (i, 0)),
            _layer_spec(w, layer, (k1 + k2, d), lambda i: (0, 0)),
            pl.BlockSpec((tm, d), lambda i: (i, 0)),
            _mod_spec(gate_chunk, rows_per_group // tm, group0, d),
        ],
        out_specs=pl.BlockSpec((tm, d), lambda i: (i, 0)),
        out_shape=jax.ShapeDtypeStruct((m, d), F32),
        compiler_params=_cparams(("arbitrary",)),
    )(y1, y2, w, h, mod3)


def _mlp_kernel(*refs, final_norm):
    if final_norm:
        h_ref, g_ref, sh_ref, sc_ref, gate_ref, w1_ref, w2_ref, fg_ref, o_ref, a_ref, acc_ref = refs
    else:
        h_ref, g_ref, sh_ref, sc_ref, gate_ref, w1_ref, w2_ref, o_ref, a_ref, acc_ref = refs
    k = pl.program_id(1)

    @pl.when(k == 0)
    def _():
        y = _rms(h_ref[...]) * g_ref[...]
        a_ref[...] = (y * (1.0 + sc_ref[...]) + sh_ref[...]).astype(BF16)
        acc_ref[...] = jnp.zeros_like(acc_ref)

    u = jnp.maximum(_dot(a_ref[...], w1_ref[...]), 0.0)
    acc_ref[...] += _dot((u * u).astype(BF16), w2_ref[...])

    @pl.when(k == pl.num_programs(1) - 1)
    def _():
        out = h_ref[...] + gate_ref[...] * acc_ref[...]
        if final_norm:
            out = _rms(out) * fg_ref[...]
        o_ref[...] = out


def _mlp(h, g, w1, w2, layer, mod, final_g=None, tm=512, th=1024):
    m, d = h.shape
    hid = w1.shape[-1]
    tm = min(tm, m)
    mod3, sh_chunk, sc_chunk, gate_chunk, rows_per_group, group0 = mod
    tpg = rows_per_group // tm
    in_specs = [
        pl.BlockSpec((tm, d), lambda i, k: (i, 0)),
        pl.BlockSpec((1, d), lambda i, k: (0, 0)),
        _mod_spec(sh_chunk, tpg, group0, d),
        _mod_spec(sc_chunk, tpg, group0, d),
        _mod_spec(gate_chunk, tpg, group0, d),
        _layer_spec(w1, layer, (d, th), lambda i, k: (0, k)),
        _layer_spec(w2, layer, (th, d), lambda i, k: (k, 0)),
    ]
    args = [h, g.reshape(1, d), mod3, mod3, mod3, w1, w2]
    if final_g is not None:
        in_specs.append(pl.BlockSpec((1, d), lambda i, k: (0, 0)))
        args.append(final_g.reshape(1, d))
    return pl.pallas_call(
        functools.partial(_mlp_kernel, final_norm=final_g is not None),
        grid=(m // tm, hid // th),
        in_specs=in_specs,
        out_specs=pl.BlockSpec((tm, d), lambda i, k: (i, 0)),
        out_shape=jax.ShapeDtypeStruct((m, d), F32),
        scratch_shapes=[pltpu.VMEM((tm, d), BF16), pltpu.VMEM((tm, d), F32)],
        compiler_params=_cparams(("arbitrary", "arbitrary")),
    )(*args)


def _softmax_pv(scores, values, scale=1.0):
    m = functools.reduce(jnp.maximum, [jnp.max(s, axis=-1, keepdims=True) for s in scores])
    ps = [jnp.exp2((s - m) * (scale * LOG2E)) for s in scores]
    denom = functools.reduce(jnp.add, [jnp.sum(p, axis=-1, keepdims=True) for p in ps])
    o = functools.reduce(jnp.add, [_dot(p.astype(BF16), v) for p, v in zip(ps, values)])
    return o / denom


def _rope_rotate(x, cos, sin):
    x = x.astype(F32)
    lane = lax.broadcasted_iota(jnp.int32, x.shape, 1)
    partner = jnp.where((lane % 32) < 16, pltpu.roll(x, LANES - 16, 1), pltpu.roll(x, 16, 1))
    return x * cos + partner * sin


def _mla_lat_kernel(qn_ref, qpe_ref, cosq_ref, sinq_ref, knc_ref, kpec_ref, vc_ref, knl_ref, kpel_ref,
                    cosk_ref, sink_ref, vl_ref, o_ref, k_scr, v_scr, *, scale):
    lc = knc_ref.shape[0]

    @pl.when(pl.program_id(2) == 0)
    def _():
        k_scr[:lc, :LANES] = knc_ref[...]
        k_scr[:lc, LANES:] = kpec_ref[...]
        k_scr[lc:, :LANES] = knl_ref[...]
        k_scr[lc:, LANES:] = _rope_rotate(kpel_ref[...], cosk_ref[...], sink_ref[...]).astype(BF16)
        v_scr[:lc, :LANES] = vc_ref[...]
        v_scr[lc:, :LANES] = vl_ref[...]
        v_scr[:, LANES:] = jnp.ones((v_scr.shape[0], LANES), BF16)

    qpe = _rope_rotate(qpe_ref[...], cosq_ref[...], sinq_ref[...]).astype(BF16)
    q = jnp.concatenate([qn_ref[...], qpe], axis=1)
    sub = min(MLA_SUB_ROWS, q.shape[0])
    n_sub = q.shape[0] // sub
    k = k_scr[...]
    v = v_scr[...]

    def scores(i):
        return _dot_t(q[i * sub:(i + 1) * sub], k)

    def finish(i, s):
        p = jnp.exp2((s - jnp.max(s, axis=-1, keepdims=True)) * (scale * LOG2E)).astype(BF16)
        ol = _dot(p, v)
        o_ref[i * sub:(i + 1) * sub, :] = (ol[:, :LANES] / ol[:, LANES:]).astype(o_ref.dtype)

    s_cur = scores(0)
    for i in range(n_sub):
        s_next = scores(i + 1) if i + 1 < n_sub else None
        finish(i, s_cur)
        s_cur = s_next


def _mla_latent(p_lat, p_ctx, kv_lat, kv_ctx, cos_tab, sin_tab, batch, tq=2048):
    n = p_lat.shape[0] // batch
    lc = p_ctx.shape[0] // batch
    tq = min(tq, n)
    assert n % tq == 0 and tq % min(MLA_SUB_ROWS, tq) == 0
    nq = n // tq
    h = MLA_HEADS
    blk = lambda rows, f: pl.BlockSpec((rows, LANES), f)
    return pl.pallas_call(
        functools.partial(_mla_lat_kernel, scale=MLA_QK_DIM ** -0.5),
        grid=(batch, h, nq),
        in_specs=[
            blk(tq, lambda b, hh, i: (b * nq + i, EV_QMLA_BLK + 2 * hh)),
            blk(tq, lambda b, hh, i: (b * nq + i, EV_QMLA_BLK + 2 * hh + 1)),
            blk(tq, lambda b, hh, i: (i, 0)),
            blk(tq, lambda b, hh, i: (i, 0)),
            blk(lc, lambda b, hh, i: (b, 2 * hh)),
            blk(lc, lambda b, hh, i: (b, EV_KPE_BLK)),
            blk(lc, lambda b, hh, i: (b, 2 * hh + 1)),
            blk(n, lambda b, hh, i: (b, 2 * hh)),
            blk(n, lambda b, hh, i: (b, EV_KPE_BLK)),
            blk(n, lambda b, hh, i: (0, 0)),
            blk(n, lambda b, hh, i: (0, 0)),
            blk(n, lambda b, hh, i: (b, 2 * hh + 1)),
        ],
        out_specs=blk(tq, lambda b, hh, i: (b * nq + i, hh)),
        out_shape=jax.ShapeDtypeStruct((batch * n, h * LANES), BF16),
        scratch_shapes=[pltpu.VMEM((lc + n, 2 * LANES), BF16), pltpu.VMEM((lc + n, 2 * LANES), BF16)],
        compiler_params=_cparams(("arbitrary", "arbitrary", "arbitrary")),
    )(p_lat, p_lat, cos_tab, sin_tab, kv_ctx, p_ctx, kv_ctx, kv_lat, p_lat, cos_tab, sin_tab, kv_lat)


def _ctx_attn_kernel(qm_ref, kn_ref, kpe_ref, vm_ref, qn_ref, kna_ref, vna_ref, om_ref, on_ref,
                     *, mla_scale, na_scale):
    k = jnp.concatenate([kn_ref[...], kpe_ref[...]], axis=1)
    s = _dot_t(qm_ref[...], k)
    om_ref[...] = _softmax_pv([s], [vm_ref[...]], mla_scale).astype(om_ref.dtype)
    s = _dot_t(qn_ref[...], kna_ref[...])
    on_ref[...] = _softmax_pv([s], [vna_ref[...]], na_scale).astype(on_ref.dtype)


def _ctx_attention(p_ctx, kv_ctx, batch):
    lc = p_ctx.shape[0] // batch
    h = MLA_HEADS
    blk = lambda f: pl.BlockSpec((lc, LANES), f)
    out = jax.ShapeDtypeStruct((batch * lc, h * LANES), BF16)
    return pl.pallas_call(
        functools.partial(_ctx_attn_kernel, mla_scale=MLA_QK_DIM ** -0.5, na_scale=NA_DIM ** -0.5),
        grid=(batch, h),
        in_specs=[
            pl.BlockSpec((lc, 2 * LANES), lambda b, hh: (b, hh)),
            blk(lambda b, hh: (b, 2 * hh)),
            blk(lambda b, hh: (b, EV_KPE_BLK)),
            blk(lambda b, hh: (b, 2 * hh + 1)),
            blk(lambda b, hh: (b, EV_QNA_BLK + hh)),
            blk(lambda b, hh: (b, EV_KNA_BLK + hh)),
            blk(lambda b, hh: (b, EV_VNA_BLK + hh)),
        ],
        out_specs=[blk(lambda b, hh: (b, hh)), blk(lambda b, hh: (b, hh))],
        out_shape=[out, out],
        compiler_params=_cparams(("arbitrary", "arbitrary")),
    )(p_ctx, kv_ctx, p_ctx, kv_ctx, p_ctx, p_ctx, p_ctx)


def _na_kernel(q_ref, k_ref, v_ref, kc_ref, vc_ref, bias_ref, o_ref, s_scr, p_scr, l_scr, oc_scr,
               *, scale, n_rows):
    win = NA_KH * GRID_W
    slab = 256
    n = q_ref.shape[0]

    def band(r):
        ws = min(max(r - NA_KH // 2, 0), n_rows - NA_KH)
        return ws, slice(r * GRID_W, (r + 1) * GRID_W), slice(ws * GRID_W, ws * GRID_W + win)

    s_scr[:, win:] = _dot_t(q_ref[...], kc_ref[...]) * scale
    for r in range(n_rows):
        ws, rows, keys = band(r)
        s_scr[rows, :win] = _dot_t(q_ref[rows, :], k_ref[keys, :]) * scale + bias_ref[r - ws]

    def body(i, carry):
        sl = pl.ds(pl.multiple_of(i * slab, slab), slab)
        s = s_scr[sl, :]
        p = jnp.exp2((s - jnp.max(s, axis=-1, keepdims=True)) * LOG2E)
        p_scr[sl, :] = p.astype(BF16)
        l_scr[sl, :] = jnp.broadcast_to(1.0 / jnp.sum(p, axis=-1, keepdims=True), (slab, LANES))
        return carry

    lax.fori_loop(0, n // slab, body, 0, unroll=2)
    oc_scr[...] = _dot(p_scr[:, win:], vc_ref[...])
    for r in range(n_rows):
        ws, rows, keys = band(r)
        o = _dot(p_scr[rows, :win], v_ref[keys, :]) + oc_scr[rows, :]
        o_ref[rows, :] = (o * l_scr[rows, :]).astype(o_ref.dtype)


def _na_bias_kernel(rb_ref, onehot_ref, mask_ref, o_ref):
    o_ref[...] = _dot_f32(rb_ref[...], onehot_ref[...]) + mask_ref[...]


def _na_bias_table(rel_bias):
    n_heads, n_ro, n_co = rel_bias.shape
    col = np.arange(GRID_W)
    col_start = np.clip(col - NA_KW // 2, 0, GRID_W - NA_KW)
    col_mask = (col[None, :] >= col_start[:, None]) & (col[None, :] < col_start[:, None] + NA_KW)
    col_off = np.clip(col[None, :] - col[:, None], 1 - NA_KW, NA_KW - 1) + (NA_KW - 1)
    onehot = (col_off.reshape(1, -1) == np.arange(n_co)[:, None]).astype(np.float32)
    mask_add = np.where(col_mask.reshape(1, -1), 0.0, NEG_INF).astype(np.float32)
    qw = GRID_W * GRID_W
    full = lambda shape: pl.BlockSpec(shape, lambda: (0,) * len(shape))
    cols = pl.pallas_call(
        _na_bias_kernel,
        in_specs=[full((n_heads * n_ro, n_co)), full((n_co, qw)), full((1, qw))],
        out_specs=full((n_heads * n_ro, qw)),
        out_shape=jax.ShapeDtypeStruct((n_heads * n_ro, qw), F32),
    )(rel_bias.reshape(n_heads * n_ro, n_co), jnp.asarray(onehot), jnp.asarray(mask_add))
    cols = cols.reshape(n_heads, n_ro, GRID_W, GRID_W)
    t = jnp.stack([cols[:, NA_KH - 1 - e:2 * NA_KH - 1 - e] for e in range(NA_KH)], axis=1)
    return t.transpose(0, 1, 3, 2, 4).reshape(n_heads, NA_KH, GRID_W, NA_KH * GRID_W)


def _na_latent(p_lat, p_ctx, bias_tab, batch):
    n = p_lat.shape[0] // batch
    lc = p_ctx.shape[0] // batch
    h = NA_HEADS
    n_rows = n // GRID_W
    assert n_rows >= NA_KH
    blk = lambda rows, f: pl.BlockSpec((rows, LANES), f)
    return pl.pallas_call(
        functools.partial(_na_kernel, scale=NA_DIM ** -0.5, n_rows=n_rows),
        grid=(batch, h),
        in_specs=[
            blk(n, lambda b, hh: (b, EV_QNA_BLK + hh)),
            blk(n, lambda b, hh: (b, EV_KNA_BLK + hh)),
            blk(n, lambda b, hh: (b, EV_VNA_BLK + hh)),
            blk(lc, lambda b, hh: (b, EV_KNA_BLK + hh)),
            blk(lc, lambda b, hh: (b, EV_VNA_BLK + hh)),
            pl.BlockSpec((None, NA_KH, GRID_W, NA_KH * GRID_W), lambda b, hh: (hh, 0, 0, 0)),
        ],
        out_specs=blk(n, lambda b, hh: (b, hh)),
        out_shape=jax.ShapeDtypeStruct((batch * n, h * LANES), BF16),
        scratch_shapes=[pltpu.VMEM((n, NA_KH * GRID_W + lc), F32), pltpu.VMEM((n, NA_KH * GRID_W + lc), BF16),
                        pltpu.VMEM((n, LANES), F32), pltpu.VMEM((n, LANES), F32)],
        compiler_params=_cparams(("arbitrary", "arbitrary")),
    )(p_lat, p_lat, p_lat, p_ctx, p_ctx, bias_tab)


def _rope_tables(n):
    pos = np.arange(n)
    rows, cols = pos // GRID_W, pos % GRID_W
    half = MLA_ROPE_DIM // 2
    inv_freq = ROPE_THETA ** (-np.arange(0, half, 2, dtype=np.float64) / half)
    cos = np.zeros((n, LANES), np.float64)
    sin = np.zeros((n, LANES), np.float64)
    for base, p in ((0, rows), (half, cols)):
        ang = p[:, None].astype(np.float64) * inv_freq[None, :]
        q = half // 2
        cos[:, base:base + q] = np.cos(ang)
        cos[:, base + q:base + half] = np.cos(ang)
        sin[:, base:base + q] = -np.sin(ang)
        sin[:, base + q:base + half] = np.sin(ang)
    return jnp.asarray(cos, F32), jnp.asarray(sin, F32)


def _even_w_in(w):
    d = w.shape[0]
    w = w.astype(BF16)
    q_end = MLA_HEADS * MLA_QK_DIM
    ckv_end = q_end + MLA_KV_RANK
    kpe_end = ckv_end + MLA_ROPE_DIM
    q = jnp.pad(w[:, :q_end].reshape(d, MLA_HEADS, MLA_QK_DIM), ((0, 0), (0, 0), (0, 2 * LANES - MLA_QK_DIM)))
    used = MLA_HEADS * 2 * LANES + (w.shape[1] - kpe_end) + MLA_KV_RANK + MLA_ROPE_DIM
    kpe = jnp.pad(w[:, ckv_end:kpe_end], ((0, 0), (0, EV_WIDTH - used)))
    return jnp.concatenate([q.reshape(d, MLA_HEADS * 2 * LANES), w[:, kpe_end:], w[:, q_end:ckv_end], kpe], axis=1)


def _hgrn_gates(q_ref, z_refs, lbs, row0, qb_s, cum_s, cpk_s, qd_s, kd_s, dec_s):
    c = HGRN_CHUNK
    grp = LANES
    ri = lax.broadcasted_iota(jnp.int32, (grp, grp), 0)
    ci = lax.broadcasted_iota(jnp.int32, (grp, grp), 1)
    same = (ri // c) == (ci // c)
    blk = jnp.where(same, 1.0, 0.0).astype(BF16)
    tris = (jnp.where(jnp.logical_and(same, ci <= ri), 1.0, 0.0).astype(BF16),
            jnp.where(jnp.logical_and(same, ci >= ri), 1.0, 0.0).astype(BF16))

    def body(g, carry):
        src = pl.ds(pl.multiple_of(g * grp, grp), grp)
        dst = pl.ds(pl.multiple_of(row0 + g * grp, grp), grp)
        q = _silu(q_ref[src, :].astype(F32))
        qb_s[dst, :] = q.astype(BF16)
        for d in range(2):
            f = jnp.maximum(lbs[d] + (1.0 - lbs[d]) * _sigmoid(z_refs[d][src, :].astype(F32)), FORGET_FLOOR)
            lf = jnp.log(f)
            k = 1.0 - f
            cum = _dot_01(tris[d], lf)
            tot = _dot_01(blk, lf)
            cum_s[d, dst, :] = cum * LOG2E
            cpk_s[d, dst, :] = (cum - jnp.log(k)) * LOG2E
            qd_s[d, dst, :] = (q * jnp.exp(cum)).astype(BF16)
            kd_s[d, dst, :] = (k * jnp.exp(tot - cum)).astype(BF16)
            dec_s[d, dst, :] = jnp.exp(tot)
        return carry

    lax.fori_loop(0, q_ref.shape[0] // grp, body, 0)


def _hgrn_chunk(qb, cum2, cpk2, v, qd, st, ones_bf, reverse):
    c = HGRN_CHUNK
    hc = c // 2
    o = _dot_t(qd, st.astype(BF16))
    rows = lax.broadcasted_iota(jnp.int32, (hc, LANES), 0)
    halves = (cum2[:hc], cum2[hc:])
    zero = jnp.zeros((hc, LANES), F32)
    pieces = []
    for s in range(c):
        ref = cpk2[s:s + 1]
        hs, rs = divmod(s, hc)
        es = []
        for hh in range(2):
            if hh == hs:
                mask = (rows <= rs) if reverse else (rows >= rs)
                es.append(jnp.exp2(jnp.where(mask, halves[hh] - ref, NEG_INF)))
            elif (hh > hs) != reverse:
                es.append(jnp.exp2(halves[hh] - ref))
            else:
                es.append(zero)
        pieces.append(jnp.concatenate(es, axis=0).astype(BF16) * qb)
    lhs = jnp.concatenate([jnp.concatenate(pieces[:hc], axis=0), jnp.concatenate(pieces[hc:], axis=0)], axis=1)
    r = _dot(lhs, ones_bf)
    for s in range(c):
        hs, rs = divmod(s, hc)
        o = o + r[rs * c:(rs + 1) * c, hs * LANES:(hs + 1) * LANES] * v[s:s + 1]
    return o


def _hgrn_gates_fast(q_ref, i_ref, z_refs, lbs, row0, worst, v_s, qf_s, kf_s, qdf_s, kdf_s, decf_s):
    c = HGRN_FAST_CHUNK
    grp = LANES
    ri = lax.broadcasted_iota(jnp.int32, (grp, grp), 0)
    ci = lax.broadcasted_iota(jnp.int32, (grp, grp), 1)
    same = (ri // c) == (ci // c)
    half = c // 2
    sums = []
    for fwd in (True, False):
        tri = (ci <= ri) if fwd else (ci >= ri)
        upto_ref = ((ci % c) <= half) if fwd else ((ci % c) >= half)
        rows = [jnp.logical_and(same, tri), same, jnp.logical_and(same, upto_ref)]
        sums.append(jnp.concatenate([jnp.where(m, 1.0, 0.0) for m in rows], axis=0).astype(BF16))

    def body(g, worst):
        src = pl.ds(pl.multiple_of(g * grp, grp), grp)
        dst = pl.ds(pl.multiple_of(row0 + g * grp, grp), grp)
        q = _silu(q_ref[src, :].astype(F32))
        v_s[dst, :] = i_ref[src, :].astype(F32)
        fs = [jnp.maximum(lbs[d] + (1.0 - lbs[d]) * _sigmoid(z_refs[d][src, :].astype(F32)), FORGET_FLOOR)
              for d in range(2)]
        res = [_dot_01(sums[d], jnp.log(fs[d])) for d in range(2)]
        for d in range(2):
            k = 1.0 - fs[d]
            cum, tot, ref = res[d][:grp], res[d][grp:2 * grp], res[d][2 * grp:]
            qf_s[d, dst, :] = (q * jnp.exp(cum - ref)).astype(BF16)
            kf_s[d, dst, :] = (k * jnp.exp(ref - cum)).astype(BF16)
            qdf_s[d, dst, :] = (q * jnp.exp(cum)).astype(BF16)
            kdf_s[d, dst, :] = (k * jnp.exp(tot - cum)).astype(BF16)
            decf_s[d, dst, :] = jnp.exp(tot)
            worst = jnp.maximum(worst, -tot)
        return worst

    return lax.fori_loop(0, q_ref.shape[0] // grp, body, worst, unroll=2)


def _hgrn_kernel(ql_ref, il_ref, zfl_ref, zbl_ref, gl_ref, qc_ref, ic_ref, zfc_ref, zbc_ref, gc_ref,
                 lbl_ref, ng_ref, yl_ref, yc_ref,
                 qb_s, v_s, cum_s, cpk_s, qd_s, kd_s, dec_s, o_s, qf_s, kf_s, qdf_s, kdf_s, decf_s, *, layer):
    lc, n = qc_ref.shape[0], ql_ref.shape[0]
    lbs = []
    for d in range(2):
        lg = lbl_ref[d]
        ex = jnp.exp(lg - jnp.max(lg, axis=0, keepdims=True))
        p = ex / jnp.sum(ex, axis=0, keepdims=True)
        lbs.append(jnp.sum(p[:layer + 1], axis=0, keepdims=True) - p[0:1])

    zero = jnp.zeros((LANES, LANES), F32)

    def chunk_rows(j, c):
        r_fwd = j * c
        r_bwd = jnp.where(j < lc // c, lc - c - j * c, 2 * lc + n - c - j * c)
        return [pl.ds(pl.multiple_of(r, c), c) for r in (r_fwd, r_bwd)]

    fast_scr = (v_s, qf_s, kf_s, qdf_s, kdf_s, decf_s)
    worst = _hgrn_gates_fast(qc_ref, ic_ref, (zfc_ref, zbc_ref), lbs, 0, zero, *fast_scr)
    worst = _hgrn_gates_fast(ql_ref, il_ref, (zfl_ref, zbl_ref), lbs, lc, worst, *fast_scr)
    fast_ok = jnp.max(worst) <= HGRN_FAST_MAX_DECAY

    @pl.when(fast_ok)
    def _():
        c = HGRN_FAST_CHUNK
        ri = lax.broadcasted_iota(jnp.int32, (c, c), 0)
        ci = lax.broadcasted_iota(jnp.int32, (c, c), 1)
        causal = (ci <= ri, ci >= ri)

        def body(g, sts):
            slices = [chunk_rows(g * HGRN_FAST_UNROLL + i, c) for i in range(HGRN_FAST_UNROLL)]
            scores = [[jnp.where(causal[d], _dot_t(qf_s[d, sl[d], :], kf_s[d, sl[d], :]), 0.0).astype(BF16)
                       for d in range(2)] for sl in slices]
            incs = [[_dot_tn(v_s[sl[d], :].astype(BF16), kdf_s[d, sl[d], :]) for d in range(2)] for sl in slices]
            local = [[_dot(scores[i][d], v_s[sl[d], :].astype(BF16)) for d in range(2)]
                     for i, sl in enumerate(slices)]
            sts = list(sts)
            for i, sl2 in enumerate(slices):
                for d in range(2):
                    sl = sl2[d]
                    o_s[d, sl, :] = _dot_t(qdf_s[d, sl, :], sts[d].astype(BF16)) + local[i][d]
                    sts[d] = sts[d] * decf_s[d, sl, :][0:1] + incs[i][d]
            return tuple(sts)

        n_chunks = (lc + n) // c
        assert n_chunks % HGRN_FAST_UNROLL == 0
        lax.fori_loop(0, n_chunks // HGRN_FAST_UNROLL, body, (zero, zero))

    @pl.when(jnp.logical_not(fast_ok))
    def _():
        c = HGRN_CHUNK
        scr = (qb_s, cum_s, cpk_s, qd_s, kd_s, dec_s)
        _hgrn_gates(qc_ref, (zfc_ref, zbc_ref), lbs, 0, *scr)
        _hgrn_gates(ql_ref, (zfl_ref, zbl_ref), lbs, lc, *scr)
        ri = lax.broadcasted_iota(jnp.int32, (2 * LANES, 2 * LANES), 0)
        ci = lax.broadcasted_iota(jnp.int32, (2 * LANES, 2 * LANES), 1)
        ones_bf = jnp.where((ri // LANES) == (ci // LANES), 1.0, 0.0).astype(BF16)

        def body(g, sts):
            slices = [chunk_rows(g * HGRN_UNROLL + i, c) for i in range(HGRN_UNROLL)]
            incs = [[_dot_tn(v_s[sl[d], :].astype(BF16), kd_s[d, sl[d], :]) for d in range(2)] for sl in slices]
            sts = list(sts)
            for i, sl2 in enumerate(slices):
                for d in range(2):
                    sl = sl2[d]
                    o_s[d, sl, :] = _hgrn_chunk(qb_s[sl, :], cum_s[d, sl, :], cpk_s[d, sl, :], v_s[sl, :],
                                                qd_s[d, sl, :], sts[d], ones_bf, reverse=d == 1)
                    sts[d] = sts[d] * dec_s[d, sl, :][0:1] + incs[i][d]
            return tuple(sts)

        n_chunks = (lc + n) // c
        assert n_chunks % HGRN_UNROLL == 0
        lax.fori_loop(0, n_chunks // HGRN_UNROLL, body, (zero, zero))

    ng = ng_ref[...]
    yc_ref[...] = (_rms(o_s[0, :lc, :] + o_s[1, :lc, :]) * ng
                   * _silu(gc_ref[...].astype(F32))).astype(yc_ref.dtype)
    yl_ref[...] = (_rms(o_s[0, lc:, :] + o_s[1, lc:, :]) * ng
                   * _silu(gl_ref[...].astype(F32))).astype(yl_ref.dtype)


def _hgrn2(p_lat, p_ctx, lb_logits, norm_g, layer, batch):
    n = p_lat.shape[0] // batch
    lc = p_ctx.shape[0] // batch
    h = HGRN_HEADS
    n_layers = lb_logits.shape[1]
    rows = lc + n
    assert lc % LANES == 0 and n % LANES == 0
    lat = lambda part: pl.BlockSpec((n, LANES), lambda b, hh: (b, part * h + hh))
    ctx = lambda part: pl.BlockSpec((lc, LANES), lambda b, hh: (b, part * h + hh))
    return pl.pallas_call(
        functools.partial(_hgrn_kernel, layer=layer),
        grid=(batch, h),
        in_specs=[lat(0), lat(1), lat(2), lat(3), lat(4), ctx(0), ctx(1), ctx(2), ctx(3), ctx(4),
                  pl.BlockSpec((2, n_layers, LANES), lambda b, hh: (0, 0, hh)),
                  pl.BlockSpec((1, LANES), lambda b, hh: (0, hh))],
        out_specs=[pl.BlockSpec((n, LANES), lambda b, hh: (b, hh)),
                   pl.BlockSpec((lc, LANES), lambda b, hh: (b, hh))],
        out_shape=[jax.ShapeDtypeStruct((batch * n, h * LANES), BF16),
                   jax.ShapeDtypeStruct((batch * lc, h * LANES), BF16)],
        scratch_shapes=[pltpu.VMEM((rows, LANES), BF16), pltpu.VMEM((rows, LANES), F32),
                        pltpu.VMEM((2, rows, LANES), F32), pltpu.VMEM((2, rows, LANES), F32),
                        pltpu.VMEM((2, rows, LANES), BF16), pltpu.VMEM((2, rows, LANES), BF16),
                        pltpu.VMEM((2, rows, LANES), F32), pltpu.VMEM((2, rows, LANES), F32),
                        pltpu.VMEM((2, rows, LANES), BF16), pltpu.VMEM((2, rows, LANES), BF16),
                        pltpu.VMEM((2, rows, LANES), BF16), pltpu.VMEM((2, rows, LANES), BF16),
                        pltpu.VMEM((2, rows, LANES), F32)],
        compiler_params=_cparams(("arbitrary", "arbitrary")),
    )(p_lat, p_lat, p_lat, p_lat, p_lat, p_ctx, p_ctx, p_ctx, p_ctx, p_ctx,
      lb_logits, norm_g[layer].reshape(1, -1))


def _dft_matrices(n):
    idx = (np.arange(n)[:, None] * np.arange(n)[None, :]) % (2 * n)
    ang = idx.astype(np.float64) * (math.pi / n)
    cm = np.cos(ang)
    sf = np.sin(ang)
    sf[0, :] = (-1.0) ** np.arange(n)
    return (jnp.asarray(cm, F32).astype(BF16), jnp.asarray(sf, F32).astype(BF16),
            jnp.asarray(sf.T, F32).astype(BF16))


def _filter_features(n):
    pos = np.arange(n, dtype=np.float64)
    t = pos / max(n - 1, 1)
    bands = np.linspace(1e-4, HYENA_BANDS - 1, HYENA_BANDS)
    ang = (2.0 * math.pi / n) * pos[:, None] * bands[None, :]
    z = np.concatenate([t[:, None], np.cos(ang), -np.sin(ang)], -1)
    max_decay = math.log(HYENA_DECAY_TARGET) / HYENA_FAST_PCT
    min_decay = math.log(HYENA_DECAY_TARGET) / HYENA_SLOW_PCT
    deltas = np.abs(np.linspace(min_decay, max_decay, HYENA_WIDTH))
    return jnp.asarray(z, F32), jnp.asarray(t[:, None], F32), jnp.asarray(deltas[None, :], F32)


def _filter_kernel(z_ref, t_ref, dl_ref, w1_ref, b1_ref, w2_ref, b2_ref, w3_ref, b3_ref, fr_ref, wo_ref,
                   o_ref, hdn_ref):
    j = pl.program_id(0)

    @pl.when(j == 0)
    def _():
        fr = fr_ref[...]
        hdn = jnp.sin(fr * (_dot_f32(z_ref[...], w1_ref[...]) + b1_ref[...]))
        hdn = jnp.sin(fr * (_dot_f32(hdn, w2_ref[...]) + b2_ref[...]))
        hdn_ref[...] = jnp.sin(fr * (_dot_f32(hdn, w3_ref[...]) + b3_ref[...]))

    filt = _dot_f32(hdn_ref[...], wo_ref[...]) * jnp.exp(-t_ref[...] * dl_ref[...])
    row = lax.broadcasted_iota(jnp.int32, filt.shape, 0)
    is_bwd = j >= pl.num_programs(0) // 2
    o_ref[...] = jnp.where(jnp.logical_and(is_bwd, row == 0), 0.0, filt).astype(o_ref.dtype)


def _hyena_filters(n, w1, b1, w2, b2, w3, b3, freq, w_out, tc=512):
    z, t, deltas = _filter_features(n)
    hid = HYENA_FILT_HIDDEN
    nct = HYENA_WIDTH // tc
    full = lambda shape: pl.BlockSpec(shape, lambda j: (0,) * len(shape))
    return pl.pallas_call(
        _filter_kernel,
        grid=(2 * nct,),
        in_specs=[full((n, HYENA_EMB)), full((n, 1)),
                  pl.BlockSpec((1, tc), lambda j: (0, j % nct)),
                  full((HYENA_EMB, hid)), full((1, hid)), full((hid, hid)), full((1, hid)),
                  full((hid, hid)), full((1, hid)), full((1, hid)),
                  pl.BlockSpec((hid, tc), lambda j: (0, j))],
        out_specs=pl.BlockSpec((n, tc), lambda j: (0, j)),
        out_shape=jax.ShapeDtypeStruct((n, 2 * HYENA_WIDTH), BF16),
        scratch_shapes=[pltpu.VMEM((n, hid), F32)],
        compiler_params=_cparams(("arbitrary",)),
    )(z, t, deltas, w1, b1.reshape(1, hid), w2, b2.reshape(1, hid), w3, b3.reshape(1, hid),
      freq.reshape(1, hid), w_out)


def _spectrum_kernel(cm_ref, sf_ref, hf_ref, hb_ref, a_ref, b_ref, *, inv_len):
    cm, sf, hf, hb = cm_ref[...], sf_ref[...], hf_ref[...], hb_ref[...]
    kr = _dot(cm, hf) + _dot(cm, hb)
    d1 = _dot(sf, hf)
    d2 = _dot(sf, hb)
    row = lax.broadcasted_iota(jnp.int32, kr.shape, 0) + pl.program_id(0) * kr.shape[0]
    first = row == 0
    w = jnp.where(first, inv_len, 2.0 * inv_len)
    a_ref[...] = kr * w
    b_ref[...] = jnp.where(first, d1 + d2, d1 - d2) * w


def _filter_spectrum(filt, cm, sf, tk=512, tc=512):
    n = filt.shape[0]
    tk = min(tk, n)
    nct = HYENA_WIDTH // tc
    out = jax.ShapeDtypeStruct((n, HYENA_WIDTH), F32)
    return pl.pallas_call(
        functools.partial(_spectrum_kernel, inv_len=1.0 / (2 * n)),
        grid=(n // tk, nct),
        in_specs=[pl.BlockSpec((tk, n), lambda i, j: (i, 0)),
                  pl.BlockSpec((tk, n), lambda i, j: (i, 0)),
                  pl.BlockSpec((n, tc), lambda i, j: (0, j)),
                  pl.BlockSpec((n, tc), lambda i, j: (0, nct + j))],
        out_specs=[pl.BlockSpec((tk, tc), lambda i, j: (i, j)), pl.BlockSpec((tk, tc), lambda i, j: (i, j))],
        out_shape=[out, out],
        compiler_params=_cparams(("arbitrary", "arbitrary")),
    )(cm, sf, filt, filt)


def _hyena_gate_kernel(u0_ref, u1_ref, uv_ref, w0_ref, w1_ref, wv_ref, b0_ref, b1_ref, bv_ref,
                       x0_ref, z_ref):
    n = u0_ref.shape[0]
    row = lax.broadcasted_iota(jnp.int32, u0_ref.shape, 0)

    def conv(u_ref, w_ref, b_ref):
        u = u_ref[...].astype(F32)
        prev = jnp.where(row == 0, 0.0, pltpu.roll(u, 1, 0))
        nxt = jnp.where(row == n - 1, 0.0, pltpu.roll(u, n - 1, 0))
        return b_ref[...] + prev * w_ref[0:1] + u * w_ref[1:2] + nxt * w_ref[2:3]

    x0_ref[...] = conv(u0_ref, w0_ref, b0_ref).astype(x0_ref.dtype)
    z_ref[...] = (conv(uv_ref, wv_ref, bv_ref) * conv(u1_ref, w1_ref, b1_ref)).astype(z_ref.dtype)


def _hyena_gate(p, first_blk, conv_w, conv_b, batch, tc=256):
    n = p.shape[0] // batch
    nct = HYENA_WIDTH // tc
    c0 = first_blk * LANES // tc
    u = lambda part: pl.BlockSpec((n, tc), lambda b, j: (b, c0 + part * nct + j))
    w = lambda part: pl.BlockSpec((HYENA_SHORT, tc), lambda b, j: (0, part * nct + j))
    bb = lambda part: pl.BlockSpec((1, tc), lambda b, j: (0, part * nct + j))
    out = jax.ShapeDtypeStruct((batch * n, HYENA_WIDTH), BF16)
    cb = conv_b.reshape(1, -1)
    return pl.pallas_call(
        _hyena_gate_kernel,
        grid=(batch, nct),
        in_specs=[u(0), u(1), u(2), w(0), w(1), w(2), bb(0), bb(1), bb(2)],
        out_specs=[pl.BlockSpec((n, tc), lambda b, j: (b, j)), pl.BlockSpec((n, tc), lambda b, j: (b, j))],
        out_shape=[out, out],
        compiler_params=_cparams(("arbitrary", "arbitrary")),
    )(p, p, p, conv_w, conv_w, conv_w, cb, cb, cb)


def _dft_fwd_kernel(cm_ref, sf_ref, z_ref, a_ref, b_ref, pr_ref, ps_ref):
    z = z_ref[...]
    zr = _dot(cm_ref[...], z)
    zs = _dot(sf_ref[...], z)
    a, b = a_ref[...], b_ref[...]
    row = lax.broadcasted_iota(jnp.int32, zr.shape, 0) + pl.program_id(1) * zr.shape[0]
    first = row == 0
    pr_ref[...] = (zr * a - jnp.where(first, 0.0, zs * b)).astype(pr_ref.dtype)
    ps_ref[...] = (jnp.where(first, 0.0, zr * b) + zs * jnp.where(first, b, a)).astype(ps_ref.dtype)


def _dft_inv_kernel(cm_ref, si_ref, pr_ref, ps_ref, z_ref, x0_ref, skip_ref, o_ref):
    y = _dot(cm_ref[...], pr_ref[...]) + _dot(si_ref[...], ps_ref[...])
    z = z_ref[...].astype(F32)
    o_ref[...] = (x0_ref[...].astype(F32) * (y + z * skip_ref[...])).astype(o_ref.dtype)


def _long_conv(z, x0, spec_a, spec_b, skip, cm, sf, si, batch, tk=1024, tc=512):
    n = z.shape[0] // batch
    tk = min(tk, n)
    nk = n // tk
    nct = HYENA_WIDTH // tc
    mat = pl.BlockSpec((tk, n), lambda b, i, j: (i, 0))
    col = pl.BlockSpec((n, tc), lambda b, i, j: (b, j))
    tile_nb = pl.BlockSpec((tk, tc), lambda b, i, j: (i, j))
    tile = pl.BlockSpec((tk, tc), lambda b, i, j: (b * nk + i, j))
    spec_shape = jax.ShapeDtypeStruct((batch * n, HYENA_WIDTH), BF16)
    pr, ps = pl.pallas_call(
        _dft_fwd_kernel,
        grid=(batch, nk, nct),
        in_specs=[mat, mat, col, tile_nb, tile_nb],
        out_specs=[tile, tile],
        out_shape=[spec_shape, spec_shape],
        compiler_params=_cparams(("arbitrary", "arbitrary", "arbitrary")),
    )(cm, sf, z, spec_a, spec_b)
    return pl.pallas_call(
        _dft_inv_kernel,
        grid=(batch, nk, nct),
        in_specs=[mat, mat, col, col, tile, tile, pl.BlockSpec((1, tc), lambda b, i, j: (0, j))],
        out_specs=tile,
        out_shape=jax.ShapeDtypeStruct((batch * n, HYENA_WIDTH), BF16),
        compiler_params=_cparams(("arbitrary", "arbitrary", "arbitrary")),
    )(cm, si, pr, ps, z, x0, skip.reshape(1, -1))


def _hyena(p, first_blk, conv_w, conv_b, filt_params, skip, batch):
    n = p.shape[0] // batch
    cm, sf, si = _dft_matrices(n)
    filt = _hyena_filters(n, *filt_params)
    spec_a, spec_b = _filter_spectrum(filt, cm, sf)
    x0, z = _hyena_gate(p, first_blk, conv_w, conv_b, batch)
    return _long_conv(z, x0, spec_a, spec_b, skip, cm, sf, si, batch)


def kernel(x, c, ctx, c_ctx, ada_w, ada_b, norm_mix_g, norm_mlp_g, w_out, mlp_w1, mlp_w2, final_norm_g, ev_w_in, mla_kv_norm_g, mla_w_ukv, na_rel_bias, od_w_in, hgrn_lb_logits, hgrn_norm_g, hy_conv_w, hy_conv_b, hy_filt_w1, hy_filt_b1, hy_filt_w2, hy_filt_b2, hy_filt_w3, hy_filt_b3, hy_filt_freq, hy_filt_wout, hy_skip):
    batch, seq, d = x.shape
    lc = ctx.shape[1]
    depth = ada_w.shape[0]
    h_lat = x.reshape(batch * seq, d)
    h_ctx = ctx.reshape(batch * lc, d)

    cond = jnp.concatenate([c, c_ctx[None, :], jnp.zeros((8 - batch - 1, d), F32)], axis=0)
    mod_all = _ada_modulation(cond, ada_w, ada_b)
    cos_tab, sin_tab = _rope_tables(seq)
    hgrn_cols = 5 * HGRN_WIDTH
    od_w_in, w_out, mlp_w1, mlp_w2, mla_w_ukv = (
        t.astype(BF16) for t in (od_w_in, w_out, mlp_w1, mlp_w2, mla_w_ukv))

    for l in range(depth):
        ctx_out = l < depth - 1
        mod3 = mod_all[l].reshape(8, 1, 6 * d)
        lat_mod = lambda *chunks: (mod3, *chunks, seq, 0)
        ctx_mod = lambda *chunks: (mod3, *chunks, batch * lc, batch)
        if l % 2 == 0:
            e = l // 2
            w_in = _even_w_in(ev_w_in[e])
            kv = (EV_CKV_BLK * LANES, mla_kv_norm_g[e], mla_w_ukv, e)
            p_lat, kv_lat = _norm_proj(h_lat, 0, d, norm_mix_g[l], w_in, BF16, lat_mod(0, 1), kv=kv)
            p_ctx, kv_ctx = _norm_proj(h_ctx, 0, d, norm_mix_g[l], w_in, BF16, ctx_mod(0, 1), kv=kv)
            y1_lat = _mla_latent(p_lat, p_ctx, kv_lat, kv_ctx, cos_tab, sin_tab, batch)
            y2_lat = _na_latent(p_lat, p_ctx, _na_bias_table(na_rel_bias[e]), batch)
            if ctx_out:
                y1_ctx, y2_ctx = _ctx_attention(p_ctx, kv_ctx, batch)
        else:
            o = l // 2
            p_lat = _norm_proj(h_lat, 0, d, norm_mix_g[l], od_w_in, BF16, lat_mod(0, 1), layer=o)
            p_ctx = _norm_proj(h_ctx, 0, d, norm_mix_g[l], od_w_in, BF16, ctx_mod(0, 1), layer=o,
                               n=None if ctx_out else hgrn_cols)
            y1_lat, y1_ctx = _hgrn2(p_lat, p_ctx, hgrn_lb_logits, hgrn_norm_g, o, batch)
            filt_params = (hy_filt_w1[o], hy_filt_b1[o], hy_filt_w2[o], hy_filt_b2[o], hy_filt_w3[o],
                           hy_filt_b3[o], hy_filt_freq[o], hy_filt_wout[o])
            y2_lat = _hyena(p_lat, hgrn_cols // LANES, hy_conv_w[o], hy_conv_b[o], filt_params,
                            hy_skip[o], batch)
            if ctx_out:
                y2_ctx = _hyena(p_ctx, hgrn_cols // LANES, hy_conv_w[o], hy_conv_b[o], filt_params,
                                hy_skip[o], batch)
        h_lat = _out_proj(y1_lat, y2_lat, w_out, l, h_lat, lat_mod(2))
        h_lat = _mlp(h_lat, norm_mlp_g[l], mlp_w1, mlp_w2, l, lat_mod(3, 4, 5),
                     final_g=None if ctx_out else final_norm_g)
        if ctx_out:
            h_ctx = _out_proj(y1_ctx, y2_ctx, w_out, l, h_ctx, ctx_mod(2))
            h_ctx = _mlp(h_ctx, norm_mlp_g[l], mlp_w1, mlp_w2, l, ctx_mod(3, 4, 5))
    return h_lat.reshape(batch, seq, d)
```
